```python
import math
import numpy as np
import jax
import jax.numpy as jnp
from jax import lax

D_MODEL = 2048
BATCH = 32
SEQ = 256
DEPTH = 2
DEC_BATCH = 2
DEC_SEQ = 4096
PAST_LEN = 256

GRID_W = 64
D_MIX = D_MODEL // 2
N_MOD = 6
S5_CH = 16
S5_G = D_MIX // S5_CH
S5_P = 64
RET_H = 8
RET_DK = D_MIX // RET_H
RET_DV = D_MIX // RET_H
RET_CHUNK = 128
MLA_DN = 128
MLA_DR = 64
MLA_DV = 128
MLA_H = D_MIX // MLA_DV
MLA_Q_LORA = 768
MLA_KV_LORA = 512
ROPE_BASE = 10000.0
Q_BLOCK = 128
D_FF = 11 * D_MODEL // 4
NORM_EPS = 1e-6
IN_WIDTHS = (D_MIX, D_MIX, D_MIX, D_MIX, D_MIX, MLA_Q_LORA, MLA_KV_LORA, MLA_DR, D_MODEL, D_MODEL, D_MODEL)

kernel_name = 'hybrid_prefix_s5_retention_mla_trunk_step'


def rmsnorm(x, g):
    xf = x.astype(jnp.float32)
    y = xf * lax.rsqrt(jnp.mean(xf * xf, axis=-1, keepdims=True) + NORM_EPS)
    return (y * g.astype(jnp.float32)).astype(x.dtype)


def adaln(cvec, w_ada, b_ada):
    m = jax.nn.silu(cvec) @ w_ada + b_ada
    return jnp.split(m, N_MOD, axis=-1)


def s5_discretise(lam_re, lam_im, log_dt):
    f32 = jnp.float32
    dt = jnp.exp(log_dt.astype(f32))[:, None]
    lr, li = lam_re.astype(f32), lam_im.astype(f32)
    ar, ai = lr * dt, li * dt
    mag = jnp.exp(ar)
    abar_re, abar_im = mag * jnp.cos(ai), mag * jnp.sin(ai)
    nr, ni = abar_re - 1.0, abar_im
    den = lr * lr + li * li
    coef_re = (nr * lr + ni * li) / den
    coef_im = (ni * lr - nr * li) / den
    return ar, ai, abar_re, abar_im, coef_re, coef_im


def s5_scan(abar_re, abar_im, b_re, b_im, reverse):
    a_re = jnp.broadcast_to(abar_re, b_re.shape)
    a_im = jnp.broadcast_to(abar_im, b_re.shape)

    def combine(e1, e2):
        a1r, a1i, b1r, b1i = e1
        a2r, a2i, b2r, b2i = e2
        return (a2r * a1r - a2i * a1i,
                a2r * a1i + a2i * a1r,
                a2r * b1r - a2i * b1i + b2r,
                a2r * b1i + a2i * b1r + b2i)

    _, _, s_re, s_im = lax.associative_scan(combine, (a_re, a_im, b_re, b_im), reverse=reverse, axis=1)
    return s_re, s_im


def s5_branch(u, lp, ctx_state):
    f32 = jnp.float32
    B_, L, _ = u.shape
    uf = u.astype(f32)
    ug = uf.reshape(B_, L, S5_G, S5_CH)
    bu_re = jnp.einsum('blgc,gpc->blgp', ug, lp['s5_b_re'].astype(f32))
    bu_im = jnp.einsum('blgc,gpc->blgp', ug, lp['s5_b_im'].astype(f32))
    tot_re, tot_im, finals = None, None, []
    for d in range(2):
        ar, ai, abr, abi, cr, ci = s5_discretise(lp['s5_lam_re'][d], lp['s5_lam_im'][d], lp['s5_log_dt'][d])
        b_re = cr * bu_re - ci * bu_im
        b_im = cr * bu_im + ci * bu_re
        s_re, s_im = s5_scan(abr, abi, b_re, b_im, reverse=(d == 1))
        if ctx_state is None:
            edge = -1 if d == 0 else 0
            finals.append(jnp.stack([s_re[:, edge], s_im[:, edge]], axis=-1))
        else:
            s0 = ctx_state[d].astype(f32)
            kk = (jnp.arange(1, L + 1) if d == 0 else jnp.arange(L, 0, -1)).astype(f32)[:, None, None]
            mag = jnp.exp(kk * ar)
            p_re, p_im = mag * jnp.cos(kk * ai), mag * jnp.sin(kk * ai)
            s0r, s0i = s0[..., 0][:, None], s0[..., 1][:, None]
            s_re = s_re + p_re * s0r - p_im * s0i
            s_im = s_im + p_re * s0i + p_im * s0r
        tot_re = s_re if tot_re is None else tot_re + s_re
        tot_im = s_im if tot_im is None else tot_im + s_im
    y = (jnp.einsum('blgp,gcp->blgc', tot_re, lp['s5_c_re'].astype(f32))
         - jnp.einsum('blgp,gcp->blgc', tot_im, lp['s5_c_im'].astype(f32)))
    y = y.reshape(B_, L, D_MIX) + lp['s5_d'].astype(f32) * uf
    y = jax.nn.gelu(y)
    y = y * jax.nn.sigmoid(y @ lp['s5_glu_w'].astype(f32) + lp['s5_glu_b'].astype(f32))
    return y.astype(u.dtype), finals


def retention_scan(q, k, v, log_gamma, s0):
    f32 = jnp.float32
    B_, L, H, _ = q.shape
    DV = v.shape[-1]
    nc = L // RET_CHUNK

    def chunks(t):
        return t.reshape(B_, nc, RET_CHUNK, H, t.shape[-1]).transpose(1, 0, 3, 2, 4)

    idx = jnp.arange(RET_CHUNK, dtype=f32)
    lg = log_gamma[:, None]
    diff = idx[:, None] - idx[None, :]
    decay_in = jnp.where(diff >= 0, jnp.exp(lg[:, :, None] * jnp.maximum(diff, 0.0)), 0.0)
    decay_q = jnp.exp(lg * (idx + 1.0))[..., None]
    decay_k = jnp.exp(lg * (RET_CHUNK - 1.0 - idx))[..., None]
    decay_s = jnp.exp(lg * RET_CHUNK)[..., None]

    def step(S, blk):
        qc, kc, vc = blk
        att = jnp.einsum('bhid,bhjd->bhij', qc, kc) * decay_in
        o = jnp.einsum('bhij,bhje->bhie', att, vc) + jnp.einsum('bhid,bhde->bhie', qc * decay_q, S)
        S = decay_s * S + jnp.einsum('bhjd,bhje->bhde', kc * decay_k, vc)
        return S, o

    S, o = lax.scan(step, s0, (chunks(q), chunks(k), chunks(v)))
    o = o.transpose(1, 0, 3, 2, 4).reshape(B_, L, H, DV)
    return o, S


def retention_branch(q, k, v, g, lp, ctx_state):
    f32 = jnp.float32
    B_, L, _ = q.shape
    qf = q.astype(f32).reshape(B_, L, RET_H, RET_DK)
    kf = k.astype(f32).reshape(B_, L, RET_H, RET_DK) * (RET_DK ** -0.5)
    vf = v.astype(f32).reshape(B_, L, RET_H, RET_DV)
    o_tot, finals = None, []
    for d in range(2):
        log_gamma = -jnp.exp(lp['ret_decay'][d].astype(f32))
        if ctx_state is None:
            s0 = jnp.zeros((B_, RET_H, RET_DK, RET_DV), f32)
        else:
            s0 = ctx_state[d].astype(f32)
        if d == 0:
            o, S = retention_scan(qf, kf, vf, log_gamma, s0)
        else:
            o, S = retention_scan(qf[:, ::-1], kf[:, ::-1], vf[:, ::-1], log_gamma, s0)
            o = o[:, ::-1]
        if ctx_state is None:
            finals.append(S)
        o_tot = o if o_tot is None else o_tot + o
    mu = jnp.mean(o_tot, axis=-1, keepdims=True)
    var = jnp.mean(jnp.square(o_tot - mu), axis=-1, keepdims=True)
    on = ((o_tot - mu) * lax.rsqrt(var + NORM_EPS)).reshape(B_, L, D_MIX)
    y = jax.nn.silu(g.astype(f32)) * (on * lp['ret_norm_g'].astype(f32))
    return y.astype(g.dtype), finals


def axial_rope_tables(n_tokens):
    f32 = jnp.float32
    rows = n_tokens // GRID_W
    row = jnp.broadcast_to(jnp.arange(rows, dtype=f32)[:, None], (rows, GRID_W)).reshape(-1)
    col = jnp.broadcast_to(jnp.arange(GRID_W, dtype=f32)[None, :], (rows, GRID_W)).reshape(-1)
    nf = MLA_DR // 4
    inv = ROPE_BASE ** (-jnp.arange(nf, dtype=f32) / nf)
    ang = jnp.concatenate([row[:, None] * inv, col[:, None] * inv], axis=-1)
    return jnp.cos(ang), jnp.sin(ang)


def apply_rope(x, cos, sin):
    xf = x.astype(jnp.float32)
    xp = xf.reshape(xf.shape[:-1] + (MLA_DR // 2, 2))
    x1, x2 = xp[..., 0], xp[..., 1]
    out = jnp.stack([x1 * cos - x2 * sin, x1 * sin + x2 * cos], axis=-1)
    return out.reshape(x.shape).astype(x.dtype)


def block_attention(q_nope, q_rope, k_nope, k_rope, v):
    B_, L, H, _ = q_nope.shape
    nb = L // Q_BLOCK
    scale = (MLA_DN + MLA_DR) ** -0.5

    def blk(qs):
        qn, qr = qs
        s = jnp.einsum('bqhd,bkhd->bhqk', qn, k_nope) + jnp.einsum('bqhd,bkd->bhqk', qr, k_rope)
        p = jax.nn.softmax(s.astype(jnp.float32) * scale, axis=-1).astype(v.dtype)
        return jnp.einsum('bhqk,bkhd->bqhd', p, v)

    qn_b = q_nope.reshape(B_, nb, Q_BLOCK, H, MLA_DN).transpose(1, 0, 2, 3, 4)
    qr_b = q_rope.reshape(B_, nb, Q_BLOCK, H, MLA_DR).transpose(1, 0, 2, 3, 4)
    o = lax.map(blk, (qn_b, qr_b))
    return o.transpose(1, 0, 2, 3, 4).reshape(B_, L, H, MLA_DV)


def mla_branch(c_q, c_kv, k_rope, lp, ctx_cache):
    B_, L, _ = c_q.shape
    q = (rmsnorm(c_q, lp['mla_q_norm']) @ lp['mla_w_uq']).reshape(B_, L, MLA_H, MLA_DN + MLA_DR)
    q_nope, q_rope = q[..., :MLA_DN], q[..., MLA_DN:]
    ckv = rmsnorm(c_kv, lp['mla_kv_norm'])
    if ctx_cache is None:
        key_ckv, key_rope = ckv, k_rope
        new_cache = (ckv, k_rope)
    else:
        cos, sin = axial_rope_tables(L)
        q_rope = apply_rope(q_rope, cos[:, None], sin[:, None])
        key_ckv = jnp.concatenate([ckv, ctx_cache[0].astype(ckv.dtype)], axis=1)
        key_rope = jnp.concatenate([apply_rope(k_rope, cos, sin), ctx_cache[1].astype(k_rope.dtype)], axis=1)
        new_cache = None
    S_ = key_ckv.shape[1]
    k_nope = (key_ckv @ lp['mla_w_uk']).reshape(B_, S_, MLA_H, MLA_DN)
    v = (key_ckv @ lp['mla_w_uv']).reshape(B_, S_, MLA_H, MLA_DV)
    o = block_attention(q_nope, q_rope, k_nope, key_rope, v)
    return o.reshape(B_, L, D_MIX), new_cache


def token_mixer(h, lp, ctx):
    z = h @ lp['w_in']
    splits = np.cumsum(IN_WIDTHS)[:-1].tolist()
    u, q, k, v, g, c_q, c_kv, k_rope, gate_a, gate_b, gate_c = jnp.split(z, splits, axis=-1)
    ya, s5_st = s5_branch(u, lp, None if ctx is None else ctx['s5'])
    yb, ret_st = retention_branch(q, k, v, g, lp, None if ctx is None else ctx['ret'])
    yc, mla_st = mla_branch(c_q, c_kv, k_rope, lp, None if ctx is None else ctx['mla'])
    wb = lp['w_branch']
    merged = (jax.nn.sigmoid(gate_a) * (ya @ wb[0])
              + jax.nn.sigmoid(gate_b) * (yb @ wb[1])
              + jax.nn.sigmoid(gate_c) * (yc @ wb[2]))
    return merged @ lp['w_out'], (s5_st, ret_st, mla_st)


def conv_ffn(h, lp):
    u = h @ lp['ffn_w_up']
    w = lp['ffn_conv_w']
    up = jnp.pad(u, ((0, 0), (1, 1), (0, 0)))
    u = up[:, :-2] * w[0] + up[:, 1:-1] * w[1] + up[:, 2:] * w[2] + lp['ffn_conv_b']
    val, gate = jnp.split(u, 2, axis=-1)
    return (jax.nn.silu(gate) * val) @ lp['ffn_w_down']


def trunk_layer(x, mods, lp, ctx):
    sh1, sc1, g1, sh2, sc2, g2 = mods
    ng = lp['norm_g']
    h = rmsnorm(x, ng[0]) * (1.0 + sc1) + sh1
    mo, st = token_mixer(h, lp, ctx)
    x = x + g1 * rmsnorm(mo, ng[1])
    h = rmsnorm(x, ng[2]) * (1.0 + sc2) + sh2
    x = x + g2 * rmsnorm(conv_ffn(h, lp), ng[3])
    return x, st


def setup_inputs(seed: int = 0) -> dict:
    f32 = jnp.float32
    key = jax.random.key(seed)
    keys = jax.random.split(key, 48)

    def nrm(i, shape, scale):
        return scale * jax.random.normal(keys[i], shape, f32)

    x_prompt = nrm(0, (BATCH, SEQ, D_MODEL), 1.0)
    x_sample = nrm(1, (DEC_BATCH, DEC_SEQ, D_MODEL), 1.0)
    cache_mla_ckv = nrm(2, (DEC_BATCH, DEPTH, PAST_LEN, MLA_KV_LORA), 1.0)
    cache_mla_krope = nrm(3, (DEC_BATCH, DEPTH, PAST_LEN, MLA_DR), 1.0)
    state_ret_fwd = nrm(4, (DEC_BATCH, DEPTH, RET_H, RET_DK, RET_DV), 0.5)
    state_ret_bwd = nrm(5, (DEC_BATCH, DEPTH, RET_H, RET_DK, RET_DV), 0.5)
    state_s5_fwd = nrm(6, (DEC_BATCH, DEPTH, S5_G, S5_P, 2), 0.3)
    state_s5_bwd = nrm(7, (DEC_BATCH, DEPTH, S5_G, S5_P, 2), 0.3)
    c = nrm(8, (DEC_BATCH, D_MODEL), 1.0)
    c_ctx = nrm(9, (D_MODEL,), 1.0)
    ada_w = nrm(10, (DEPTH, D_MODEL, N_MOD * D_MODEL), 0.5 * D_MODEL ** -0.5)
    ada_b = nrm(11, (DEPTH, N_MOD * D_MODEL), 0.01)
    norm_g = 1.0 + nrm(12, (DEPTH, 4, D_MODEL), 0.01)
    w_in = nrm(13, (DEPTH, D_MODEL, sum(IN_WIDTHS)), D_MODEL ** -0.5)
    s5_lam_re = -0.5 + nrm(14, (DEPTH, 2, S5_G, S5_P), 0.01)
    s5_lam_im = math.pi * jnp.arange(S5_P, dtype=f32) + nrm(15, (DEPTH, 2, S5_G, S5_P), 0.01)
    s5_log_dt = jax.random.uniform(keys[16], (DEPTH, 2, S5_G), f32, math.log(0.001), math.log(0.1))
    s5_b_re = nrm(17, (DEPTH, S5_G, S5_P, S5_CH), (2 * S5_CH) ** -0.5)
    s5_b_im = nrm(18, (DEPTH, S5_G, S5_P, S5_CH), (2 * S5_CH) ** -0.5)
    s5_c_re = nrm(19, (DEPTH, S5_G, S5_CH, S5_P), (2 * S5_P) ** -0.5)
    s5_c_im = nrm(20, (DEPTH, S5_G, S5_CH, S5_P), (2 * S5_P) ** -0.5)
    s5_d = nrm(21, (DEPTH, D_MIX), 1.0)
    s5_glu_w = nrm(22, (DEPTH, D_MIX, D_MIX), D_MIX ** -0.5)
    s5_glu_b = nrm(23, (DEPTH, D_MIX), 0.01)
    ret_base = jnp.log(-jnp.log1p(-(2.0 ** (-5.0 - jnp.arange(RET_H, dtype=f32)))))
    ret_decay = ret_base + nrm(24, (DEPTH, 2, RET_H), 0.01)
    ret_norm_g = 1.0 + nrm(25, (DEPTH, D_MIX), 0.01)
    mla_q_norm = 1.0 + nrm(26, (DEPTH, MLA_Q_LORA), 0.01)
    mla_kv_norm = 1.0 + nrm(27, (DEPTH, MLA_KV_LORA), 0.01)
    mla_w_uq = nrm(28, (DEPTH, MLA_Q_LORA, MLA_H * (MLA_DN + MLA_DR)), MLA_Q_LORA ** -0.5)
    mla_w_uk = nrm(29, (DEPTH, MLA_KV_LORA, MLA_H * MLA_DN), MLA_KV_LORA ** -0.5)
    mla_w_uv = nrm(30, (DEPTH, MLA_KV_LORA, MLA_H * MLA_DV), MLA_KV_LORA ** -0.5)
    w_branch = nrm(31, (DEPTH, 3, D_MIX, D_MODEL), D_MIX ** -0.5)
    w_out = nrm(32, (DEPTH, D_MODEL, D_MODEL), D_MODEL ** -0.5)
    ffn_w_up = nrm(33, (DEPTH, D_MODEL, 2 * D_FF), D_MODEL ** -0.5)
    ffn_conv_w = nrm(34, (DEPTH, 3, 2 * D_FF), 3 ** -0.5)
    ffn_conv_b = nrm(35, (DEPTH, 2 * D_FF), 0.01)
    ffn_w_down = nrm(36, (DEPTH, D_FF, D_MODEL), D_FF ** -0.5)
    return {'x_prompt': x_prompt, 'x_sample': x_sample,
            'cache_mla_ckv': cache_mla_ckv, 'cache_mla_krope': cache_mla_krope,
            'state_ret_fwd': state_ret_fwd, 'state_ret_bwd': state_ret_bwd,
            'state_s5_fwd': state_s5_fwd, 'state_s5_bwd': state_s5_bwd,
            'c': c, 'c_ctx': c_ctx, 'ada_w': ada_w, 'ada_b': ada_b, 'norm_g': norm_g, 'w_in': w_in,
            's5_lam_re': s5_lam_re, 's5_lam_im': s5_lam_im, 's5_log_dt': s5_log_dt,
            's5_b_re': s5_b_re, 's5_b_im': s5_b_im, 's5_c_re': s5_c_re, 's5_c_im': s5_c_im,
            's5_d': s5_d, 's5_glu_w': s5_glu_w, 's5_glu_b': s5_glu_b,
            'ret_decay': ret_decay, 'ret_norm_g': ret_norm_g,
            'mla_q_norm': mla_q_norm, 'mla_kv_norm': mla_kv_norm,
            'mla_w_uq': mla_w_uq, 'mla_w_uk': mla_w_uk, 'mla_w_uv': mla_w_uv,
            'w_branch': w_branch, 'w_out': w_out,
            'ffn_w_up': ffn_w_up, 'ffn_conv_w': ffn_conv_w, 'ffn_conv_b': ffn_conv_b, 'ffn_w_down': ffn_w_down}


def reference(x_prompt, x_sample, cache_mla_ckv, cache_mla_krope, state_ret_fwd, state_ret_bwd,
              state_s5_fwd, state_s5_bwd, c, c_ctx, ada_w, ada_b, norm_g, w_in,
              s5_lam_re, s5_lam_im, s5_log_dt, s5_b_re, s5_b_im, s5_c_re, s5_c_im,
              s5_d, s5_glu_w, s5_glu_b, ret_decay, ret_norm_g,
              mla_q_norm, mla_kv_norm, mla_w_uq, mla_w_uk, mla_w_uv,
              w_branch, w_out, ffn_w_up, ffn_conv_w, ffn_conv_b, ffn_w_down):
    xp, xs = x_prompt, x_sample
    ckv_out, krope_out, retf_out, retb_out, s5f_out, s5b_out = [], [], [], [], [], []
    for l in range(DEPTH):
        lp = {'norm_g': norm_g[l], 'w_in': w_in[l],
              's5_lam_re': s5_lam_re[l], 's5_lam_im': s5_lam_im[l], 's5_log_dt': s5_log_dt[l],
              's5_b_re': s5_b_re[l], 's5_b_im': s5_b_im[l], 's5_c_re': s5_c_re[l], 's5_c_im': s5_c_im[l],
              's5_d': s5_d[l], 's5_glu_w': s5_glu_w[l], 's5_glu_b': s5_glu_b[l],
              'ret_decay': ret_decay[l], 'ret_norm_g': ret_norm_g[l],
              'mla_q_norm': mla_q_norm[l], 'mla_kv_norm': mla_kv_norm[l],
              'mla_w_uq': mla_w_uq[l], 'mla_w_uk': mla_w_uk[l], 'mla_w_uv': mla_w_uv[l],
              'w_branch': w_branch[l], 'w_out': w_out[l],
              'ffn_w_up': ffn_w_up[l], 'ffn_conv_w': ffn_conv_w[l], 'ffn_conv_b': ffn_conv_b[l],
              'ffn_w_down': ffn_w_down[l]}
        mods_ctx = adaln(c_ctx, ada_w[l], ada_b[l])
        xp, (s5_st, ret_st, mla_st) = trunk_layer(xp, mods_ctx, lp, None)
        s5f_out.append(s5_st[0])
        s5b_out.append(s5_st[1])
        retf_out.append(ret_st[0])
        retb_out.append(ret_st[1])
        ckv_out.append(mla_st[0])
        krope_out.append(mla_st[1])
        mods_lat = [m[:, None, :] for m in adaln(c, ada_w[l], ada_b[l])]
        ctx = {'s5': (state_s5_fwd[:, l], state_s5_bwd[:, l]),
               'ret': (state_ret_fwd[:, l], state_ret_bwd[:, l]),
               'mla': (cache_mla_ckv[:, l], cache_mla_krope[:, l])}
        xs, _ = trunk_layer(xs, mods_lat, lp, ctx)
    new_mla_ckv = jnp.stack(ckv_out, axis=1)
    new_mla_krope = jnp.stack(krope_out, axis=1)
    new_ret_fwd = jnp.stack(retf_out, axis=1)
    new_ret_bwd = jnp.stack(retb_out, axis=1)
    new_s5_fwd = jnp.stack(s5f_out, axis=1)
    new_s5_bwd = jnp.stack(s5b_out, axis=1)
    return (xp, xs, new_mla_ckv, new_mla_krope, new_ret_fwd, new_ret_bwd, new_s5_fwd, new_s5_bwd)
```

```python
import functools

import numpy as np
import jax
import jax.numpy as jnp
from jax import lax
from jax.experimental import pallas as pl
from jax.experimental.pallas import tpu as pltpu

F32 = jnp.float32
BF = jnp.bfloat16

NORM_EPS = 1e-6
ROPE_BASE = 10000.0
GRID_W = 64
LANES = 128
S5_T = 8
S5_SEG = 256
RET_CHUNK = 128
VMEM_LIMIT = 56 << 20


def _cparams(sem):
    return pltpu.CompilerParams(dimension_semantics=sem, vmem_limit_bytes=VMEM_LIMIT)


def _dot(a, b):
    return jnp.dot(a, b, preferred_element_type=F32)


def _dot_nt(a, b):
    return lax.dot_general(a, b, (((1,), (1,)), ((), ())), preferred_element_type=F32)


def _rms(x, g):
    return x * lax.rsqrt(jnp.mean(x * x, axis=-1, keepdims=True) + NORM_EPS) * g


def _rms_mod(x, g, sc, sh):
    return _rms(x, g) * (1.0 + sc) + sh


def _sigmoid(x):
    return 1.0 / (1.0 + jnp.exp(-x))


def _mod_spec(chunk, d, rows_per_mod, tm):
    return pl.BlockSpec((1, 1, d), lambda i, *_: ((i * tm) // rows_per_mod, 0, chunk))


def _ada_kernel(c_ref, w_ref, b_ref, o_ref):
    c = c_ref[...]
    s = (c * _sigmoid(c)).astype(BF)
    o_ref[0] = _dot(s, w_ref[0].astype(BF)) + b_ref[0]


def _ada_mods(cvecs, ada_w, ada_b):
    depth, d, n = ada_w.shape
    tn = 1024
    return pl.pallas_call(
        _ada_kernel,
        grid=(depth, n // tn),
        in_specs=[pl.BlockSpec((8, d), lambda l, j: (0, 0)),
                  pl.BlockSpec((1, d, tn), lambda l, j: (l, 0, j)),
                  pl.BlockSpec((1, 1, tn), lambda l, j: (l, 0, j))],
        out_specs=pl.BlockSpec((1, 8, tn), lambda l, j: (l, 0, j)),
        out_shape=jax.ShapeDtypeStruct((depth, 8, n), F32),
        compiler_params=_cparams(("parallel", "parallel")),
        name="ada_mods",
    )(cvecs, ada_w, ada_b.reshape(depth, 1, n))


def _inproj_kernel(x_ref, g_ref, sh_ref, sc_ref, w_ref, og_ref, os_ref, h_ref, *, n_gate_tiles):
    j = pl.program_id(1)

    @pl.when(j == 0)
    def _():
        h_ref[...] = _rms_mod(x_ref[...], g_ref[...], sc_ref[0], sh_ref[0]).astype(BF)

    r = _dot(h_ref[...], w_ref[...])

    @pl.when(j < n_gate_tiles)
    def _():
        og_ref[...] = r

    @pl.when(j >= n_gate_tiles)
    def _():
        os_ref[...] = r


def _inproj(x, mods, ng0, w_a, rows_per_mod, n_gate):
    m, d = x.shape
    n = w_a.shape[1]
    tm, tn = min(1024, m), 512
    ngt, nst = n_gate // tn, (n - n_gate) // tn
    return pl.pallas_call(
        functools.partial(_inproj_kernel, n_gate_tiles=ngt),
        grid=(m // tm, ngt + nst),
        in_specs=[pl.BlockSpec((tm, d), lambda i, j: (i, 0)),
                  pl.BlockSpec((1, d), lambda i, j: (0, 0)),
                  _mod_spec(0, d, rows_per_mod, tm),
                  _mod_spec(1, d, rows_per_mod, tm),
                  pl.BlockSpec((d, tn), lambda i, j: (0, j))],
        out_specs=[pl.BlockSpec((tm, tn), lambda i, j: (i, jnp.minimum(j, ngt - 1))),
                   pl.BlockSpec((tm, tn), lambda i, j: (i, jnp.maximum(j - ngt, 0)))],
        out_shape=[jax.ShapeDtypeStruct((m, n_gate), F32),
                   jax.ShapeDtypeStruct((m, n - n_gate), F32)],
        scratch_shapes=[pltpu.VMEM((tm, d), BF)],
        compiler_params=_cparams(("parallel", "arbitrary")),
        name="in_proj",
    )(x, ng0, mods, mods, w_a)


def _rope(seg, cosp, sinp):
    return seg * cosp + pltpu.roll(seg, LANES // 2, 1) * sinp


def _mla_proj_kernel(x_ref, g_ref, sh_ref, sc_ref, wm_ref, qg_ref, kvg_ref, wq_ref, wkv_ref, cos_ref, sin_ref,
                     q_ref, k_ref, v_ref, ckv_ref, kr_ref, *, q_lora, kv_lora, n_heads, scale):
    h = _rms_mod(x_ref[...], g_ref[...], sc_ref[0], sh_ref[0]).astype(BF)
    z = _dot(h, wm_ref[...])
    cosp, sinp = cos_ref[...], sin_ref[...]
    hv = n_heads * LANES

    cqn = _rms(z[:, :q_lora], qg_ref[...]).astype(BF)
    qraw = _dot(cqn, wq_ref[...])
    for hh in range(n_heads):
        b0 = hh * 2 * LANES
        q_ref[:, b0:b0 + LANES] = (qraw[:, b0:b0 + LANES] * scale).astype(BF)
        q_ref[:, b0 + LANES:b0 + 2 * LANES] = (_rope(qraw[:, b0 + LANES:b0 + 2 * LANES], cosp, sinp) * scale).astype(BF)

    ckv = _rms(z[:, q_lora:q_lora + kv_lora], kvg_ref[...])
    ckv_ref[...] = ckv
    kv = _dot(ckv.astype(BF), wkv_ref[...])
    v_ref[...] = kv[:, hv:].astype(BF)
    kr = z[:, q_lora + kv_lora:]
    kr_ref[...] = kr
    krot = _rope(kr, cosp, sinp).astype(BF)
    for hh in range(n_heads):
        b0 = hh * 2 * LANES
        k_ref[:, b0:b0 + LANES] = kv[:, hh * LANES:(hh + 1) * LANES].astype(BF)
        k_ref[:, b0 + LANES:b0 + 2 * LANES] = krot


def _mla_proj(x, mods, ng0, w_m, qg, kvg, wq_ext, wkv, cosp, sinp, rows_per_mod, n_heads, scale):
    m, d = x.shape
    q_lora, kv_lora = wq_ext.shape[0], wkv.shape[0]
    tm = min(256, m)
    tab_tiles = cosp.shape[0] // tm
    c2 = lambda i: (0, 0)
    return pl.pallas_call(
        functools.partial(_mla_proj_kernel, q_lora=q_lora, kv_lora=kv_lora, n_heads=n_heads, scale=scale),
        grid=(m // tm,),
        in_specs=[pl.BlockSpec((tm, d), lambda i: (i, 0)),
                  pl.BlockSpec((1, d), c2),
                  _mod_spec(0, d, rows_per_mod, tm),
                  _mod_spec(1, d, rows_per_mod, tm),
                  pl.BlockSpec(w_m.shape, c2),
                  pl.BlockSpec((1, q_lora), c2),
                  pl.BlockSpec((1, kv_lora), c2),
                  pl.BlockSpec(wq_ext.shape, c2),
                  pl.BlockSpec(wkv.shape, c2),
                  pl.BlockSpec((tm, LANES), lambda i: (i % tab_tiles, 0)),
                  pl.BlockSpec((tm, LANES), lambda i: (i % tab_tiles, 0))],
        out_specs=[pl.BlockSpec((tm, 2 * n_heads * LANES), lambda i: (i, 0)),
                   pl.BlockSpec((tm, 2 * n_heads * LANES), lambda i: (i, 0)),
                   pl.BlockSpec((tm, n_heads * LANES), lambda i: (i, 0)),
                   pl.BlockSpec((tm, kv_lora), lambda i: (i, 0)),
                   pl.BlockSpec((tm, LANES), lambda i: (i, 0))],
        out_shape=[jax.ShapeDtypeStruct((m, 2 * n_heads * LANES), BF),
                   jax.ShapeDtypeStruct((m, 2 * n_heads * LANES), BF),
                   jax.ShapeDtypeStruct((m, n_heads * LANES), BF),
                   jax.ShapeDtypeStruct((m, kv_lora), F32),
                   jax.ShapeDtypeStruct((m, LANES), F32)],
        compiler_params=_cparams(("parallel",)),
        name="mla_proj",
    )(x, ng0, mods, mods, w_m, qg, kvg, wq_ext, wkv, cosp, sinp)


def _cache_kv_kernel(ckv_ref, kr_ref, wkv_ref, k_ref, v_ref, *, n_heads):
    kv = _dot(ckv_ref[0].astype(BF), wkv_ref[...])
    v_ref[0] = kv[:, n_heads * LANES:].astype(BF)
    kr = kr_ref[0].astype(BF)
    for hh in range(n_heads):
        b0 = hh * 2 * LANES
        k_ref[0, :, b0:b0 + LANES] = kv[:, hh * LANES:(hh + 1) * LANES].astype(BF)
        k_ref[0, :, b0 + LANES:b0 + 2 * LANES] = kr


def _cache_kv(ckv, kr_pad, wkv, n_heads):
    b, s, kvl = ckv.shape
    return pl.pallas_call(
        functools.partial(_cache_kv_kernel, n_heads=n_heads),
        grid=(b,),
        in_specs=[pl.BlockSpec((1, s, kvl), lambda i: (i, 0, 0)),
                  pl.BlockSpec((1, s, LANES), lambda i: (i, 0, 0)),
                  pl.BlockSpec(wkv.shape, lambda i: (0, 0))],
        out_specs=[pl.BlockSpec((1, s, 2 * n_heads * LANES), lambda i: (i, 0, 0)),
                   pl.BlockSpec((1, s, n_heads * LANES), lambda i: (i, 0, 0))],
        out_shape=[jax.ShapeDtypeStruct((b, s, 2 * n_heads * LANES), BF),
                   jax.ShapeDtypeStruct((b, s, n_heads * LANES), BF)],
        compiler_params=_cparams(("parallel",)),
        name="mla_cache_kv",
    )(ckv, kr_pad, wkv)


def _attn_kernel(*refs, with_cache):
    if with_cache:
        q_ref, k_ref, v_ref, kc_ref, vc_ref, o_ref = refs
    else:
        q_ref, k_ref, v_ref, o_ref = refs
    q = q_ref[0]
    s = _dot_nt(q, k_ref[0])
    mx = jnp.max(s, axis=-1, keepdims=True)
    if with_cache:
        sc = _dot_nt(q, kc_ref[0])
        mx = jnp.maximum(mx, jnp.max(sc, axis=-1, keepdims=True))
    p = jnp.exp(s - mx)
    den = jnp.sum(p, axis=-1, keepdims=True)
    o = _dot(p.astype(BF), v_ref[0])
    if with_cache:
        pc = jnp.exp(sc - mx)
        den = den + jnp.sum(pc, axis=-1, keepdims=True)
        o = o + _dot(pc.astype(BF), vc_ref[0])
    o_ref[0] = (o / den).astype(BF)


def _attention(q, k, v, kc, vc, n_heads):
    b, l, _ = q.shape
    s = k.shape[1]
    tq = min(256, l)
    with_cache = kc is not None
    in_specs = [pl.BlockSpec((1, tq, 2 * LANES), lambda bi, h, i: (bi, i, h)),
                pl.BlockSpec((1, s, 2 * LANES), lambda bi, h, i: (bi, 0, h)),
                pl.BlockSpec((1, s, LANES), lambda bi, h, i: (bi, 0, h))]
    args = [q, k, v]
    if with_cache:
        sc_ = kc.shape[1]
        in_specs += [pl.BlockSpec((1, sc_, 2 * LANES), lambda bi, h, i: (bi, 0, h)),
                     pl.BlockSpec((1, sc_, LANES), lambda bi, h, i: (bi, 0, h))]
        args += [kc, vc]
    return pl.pallas_call(
        functools.partial(_attn_kernel, with_cache=with_cache),
        grid=(b, n_heads, l // tq),
        in_specs=in_specs,
        out_specs=pl.BlockSpec((1, tq, LANES), lambda bi, h, i: (bi, i, h)),
        out_shape=jax.ShapeDtypeStruct((b, l, n_heads * LANES), BF),
        compiler_params=_cparams(("parallel", "parallel", "arbitrary")),
        name="mla_attention",
    )(*args)


def _ret_kernel(*refs, with_state, n_chunks, k_scale):
    if with_state:
        (q_ref, k_ref, v_ref, g_ref, dec_ref, ng_ref, s0f_ref, s0b_ref,
         y_ref, kvf_ref, kvb_ref, sf_ref, sb_ref) = refs
    else:
        (q_ref, k_ref, v_ref, g_ref, dec_ref, ng_ref,
         y_ref, ff_ref, fb_ref, kvf_ref, kvb_ref, sf_ref, sb_ref) = refs
    c_len = RET_CHUNK
    lgf = -jnp.exp(dec_ref[0, 0])
    lgb = -jnp.exp(dec_ref[1, 0])
    ii = lax.broadcasted_iota(jnp.int32, (c_len, c_len), 0).astype(F32)
    jj = lax.broadcasted_iota(jnp.int32, (c_len, c_len), 1).astype(F32)
    diff = ii - jj
    dmask = jnp.where(diff > 0, jnp.exp(lgf * jnp.maximum(diff, 0.0)),
                      jnp.where(diff < 0, jnp.exp(lgb * jnp.maximum(-diff, 0.0)), 2.0))
    dq_f = jnp.exp(lgf * (ii + 1.0))
    dk_f = jnp.exp(lgf * (c_len - 1.0 - ii))
    dq_b = jnp.exp(lgb * (c_len - ii))
    dk_b = jnp.exp(lgb * ii)
    ds_f = jnp.exp(lgf * c_len)
    ds_b = jnp.exp(lgb * c_len)

    def chunk_kv(c, carry):
        rows = pl.ds(pl.multiple_of(c * c_len, c_len), c_len)
        kc = k_ref[0, rows, :] * k_scale
        vc = v_ref[0, rows, :].astype(BF)
        kvf_ref[c] = _dot((kc * dk_f).T.astype(BF), vc)
        kvb_ref[c] = _dot((kc * dk_b).T.astype(BF), vc)
        return carry

    lax.fori_loop(0, n_chunks, chunk_kv, 0)

    if with_state:
        s_f, s_b = s0f_ref[0, 0], s0b_ref[0, 0]
    else:
        s_f = s_b = jnp.zeros((c_len, c_len), F32)

    def scan_f(c, s):
        sf_ref[c] = s
        return ds_f * s + kvf_ref[c]

    def scan_b(t, s):
        c = n_chunks - 1 - t
        sb_ref[c] = s
        return ds_b * s + kvb_ref[c]

    s_f = lax.fori_loop(0, n_chunks, scan_f, s_f)
    s_b = lax.fori_loop(0, n_chunks, scan_b, s_b)
    if not with_state:
        ff_ref[0, 0] = s_f
        fb_ref[0, 0] = s_b

    ng = ng_ref[...]

    def chunk_out(c, carry):
        rows = pl.ds(pl.multiple_of(c * c_len, c_len), c_len)
        qc = q_ref[0, rows, :]
        kc = (k_ref[0, rows, :] * k_scale).astype(BF)
        vc = v_ref[0, rows, :].astype(BF)
        att = _dot_nt(qc.astype(BF), kc) * dmask
        o = (_dot(att.astype(BF), vc)
             + _dot((qc * dq_f).astype(BF), sf_ref[c].astype(BF))
             + _dot((qc * dq_b).astype(BF), sb_ref[c].astype(BF)))
        mu = jnp.mean(o, axis=-1, keepdims=True)
        oc = o - mu
        var = jnp.mean(oc * oc, axis=-1, keepdims=True)
        on = oc * lax.rsqrt(var + NORM_EPS)
        gg = g_ref[0, rows, :]
        y_ref[0, rows, :] = (gg * _sigmoid(gg) * (on * ng)).astype(BF)
        return carry

    lax.fori_loop(0, n_chunks, chunk_out, 0)


def _retention(zs, col0, dec, ng, s0f, s0b, n_heads):
    b, l, _ = zs.shape
    nc = l // RET_CHUNK
    cb = col0 // LANES
    with_state = s0f is not None

    def zspec(seg):
        return pl.BlockSpec((1, l, LANES), lambda bi, h: (bi, 0, cb + seg * n_heads + h))

    st_spec = pl.BlockSpec((1, 1, LANES, LANES), lambda bi, h: (bi, h, 0, 0))
    in_specs = [zspec(0), zspec(1), zspec(2), zspec(3),
                pl.BlockSpec((2, 1, 1, LANES), lambda bi, h: (0, h, 0, 0)),
                pl.BlockSpec((1, LANES), lambda bi, h: (0, h))]
    args = [zs, zs, zs, zs, dec, ng]
    y_spec = pl.BlockSpec((1, l, LANES), lambda bi, h: (bi, 0, h))
    y_shape = jax.ShapeDtypeStruct((b, l, n_heads * LANES), BF)
    if with_state:
        in_specs += [st_spec, st_spec]
        args += [s0f, s0b]
        out_specs, out_shape = y_spec, y_shape
    else:
        st_shape = jax.ShapeDtypeStruct((b, n_heads, LANES, LANES), F32)
        out_specs, out_shape = [y_spec, st_spec, st_spec], [y_shape, st_shape, st_shape]
    scratch = [pltpu.VMEM((nc, LANES, LANES), F32) for _ in range(4)]
    return pl.pallas_call(
        functools.partial(_ret_kernel, with_state=with_state, n_chunks=nc, k_scale=float(LANES) ** -0.5),
        grid=(b, n_heads),
        in_specs=in_specs, out_specs=out_specs, out_shape=out_shape,
        scratch_shapes=scratch,
        compiler_params=_cparams(("parallel", "parallel")),
        name="retention",
    )(*args)


def _s5_prep_kernel(lre_ref, lim_ref, ldt_ref, btre_ref, btim_ref, cre_ref, cim_ref,
                    x_ref, y_ref, g_ref, a_ref):
    btre, btim, cre, cim = btre_ref[...], btim_ref[...], cre_ref[...], cim_ref[...]
    for d in range(2):
        dt = jnp.exp(ldt_ref[d])
        lr, li = lre_ref[d], lim_ref[d]
        ar, ai = lr * dt, li * dt
        mag = jnp.exp(ar)
        nr, ni = mag * jnp.cos(ai) - 1.0, mag * jnp.sin(ai)
        den = lr * lr + li * li
        cr = (nr * lr + ni * li) / den
        ci = (ni * lr - nr * li) / den

        def power(kk):
            mk = jnp.exp(kk * ar)
            return mk * jnp.cos(kk * ai), mk * jnp.sin(kk * ai)

        for kk in range(S5_T):
            pr, pi = power(float(kk))
            wr, wi = cr * pr - ci * pi, cr * pi + ci * pr
            xr = wr * btre - wi * btim
            xi = wr * btim + wi * btre
            x_ref[d, kk, 0] = xr
            x_ref[d, kk, 1] = xi
            g_ref[d, kk] = (jnp.einsum('gap,gcp->gac', xr, cre, precision=lax.Precision.HIGHEST,
                                       preferred_element_type=F32)
                            - jnp.einsum('gap,gcp->gac', xi, cim, precision=lax.Precision.HIGHEST,
                                         preferred_element_type=F32))
        for kk in range(1, S5_T + 1):
            pr, pi = power(float(kk))
            y_ref[d, kk - 1, 0] = cre * pr - cim * pi
            y_ref[d, kk - 1, 1] = cre * pi + cim * pr
        for idx, kk in enumerate((S5_T, S5_SEG)):
            pr, pi = power(float(kk))
            a_ref[d, idx, 0] = pr
            a_ref[d, idx, 1] = pi


def _s5_prep(lam_re, lam_im, log_dt, b_re, b_im, c_re, c_im):
    _, g, p = lam_re.shape
    ch = b_re.shape[-1]
    gb = LANES // ch
    t = S5_T
    lre = lam_re.reshape(2, g, 1, p)
    lim = lam_im.reshape(2, g, 1, p)
    ldt = jnp.broadcast_to(log_dt.reshape(2, g, 1, 1), (2, g, 1, p))
    btre = jnp.swapaxes(b_re, 1, 2)
    btim = jnp.swapaxes(b_im, 1, 2)
    lspec = pl.BlockSpec((2, gb, 1, p), lambda i: (0, i, 0, 0))
    wspec = pl.BlockSpec((gb, ch, p), lambda i: (i, 0, 0))
    return pl.pallas_call(
        _s5_prep_kernel,
        grid=(g // gb,),
        in_specs=[lspec, lspec, lspec, wspec, wspec, wspec, wspec],
        out_specs=[pl.BlockSpec((2, t, 2, gb, ch, p), lambda i: (0, 0, 0, i, 0, 0)),
                   pl.BlockSpec((2, t, 2, gb, ch, p), lambda i: (0, 0, 0, i, 0, 0)),
                   pl.BlockSpec((2, t, gb, ch, ch), lambda i: (0, 0, i, 0, 0)),
                   pl.BlockSpec((2, 2, 2, gb, 1, p), lambda i: (0, 0, 0, i, 0, 0))],
        out_shape=[jax.ShapeDtypeStruct((2, t, 2, g, ch, p), F32),
                   jax.ShapeDtypeStruct((2, t, 2, g, ch, p), F32),
                   jax.ShapeDtypeStruct((2, t, g, ch, ch), F32),
                   jax.ShapeDtypeStruct((2, 2, 2, g, 1, p), F32)],
        compiler_params=_cparams(("parallel",)),
        name="s5_prep",
    )(lre, lim, ldt, btre, btim, c_re, c_im)


def _s5_assemble(x, y, gm, a):
    t = S5_T
    g, ch, p = x.shape[3:]
    gb = LANES // ch
    nb = g // gb
    eye = jnp.eye(gb, dtype=F32)
    lags = [gm[1, t - 1 - i] for i in range(t - 1)] + [gm[0, 0] + gm[1, 0]] + [gm[0, k] for k in range(1, t)]
    gc = jnp.stack(lags)
    idx = np.arange(t)[None, :] - np.arange(t)[:, None] + t - 1
    gall = gc[idx].reshape(t, t, nb, gb, ch, ch)
    toep = jnp.einsum('iobgac,gh->bigaohc', gall, eye).reshape(nb, t * LANES, t * LANES)
    xsel = jnp.concatenate([x[0][::-1], x[1]], axis=1).reshape(t, 4, nb, gb, ch, p)
    wd = jnp.einsum('tmbgap,gh->btgamhp', xsel, eye).reshape(nb, t * LANES, 4 * gb * p)
    sign = jnp.array([1.0, -1.0, 1.0, -1.0], F32).reshape(1, 4, 1, 1, 1)
    ysel = (jnp.concatenate([y[0], y[1][::-1]], axis=1) * sign).reshape(t, 4, nb, gb, ch, p)
    wo = jnp.einsum('tmbgcp,gh->bmgpthc', ysel, eye).reshape(nb, 4 * gb * p, t * LANES)

    def avec(i):
        return a[:, i].reshape(4, nb, gb * p).transpose(1, 0, 2)

    return toep.astype(BF), wd.astype(BF), wo.astype(BF), avec(0), avec(1)


def _gelu(x):
    return 0.5 * x * (1.0 + jnp.tanh(0.7978845608028654 * (x + 0.044715 * (x * x * x))))


def _s5_kernel(*refs, chain):
    if chain:
        (u_ref, toep_ref, wd_ref, wo_ref, a_ref, a2_ref, dsk_ref, s0_ref,
         y_ref, lhs_ref, d_ref, s_ref, xs_ref) = refs
    else:
        (u_ref, toep_ref, wd_ref, wo_ref, a_ref, dsk_ref,
         y_ref, fin_ref, lhs_ref, d_ref, s_ref) = refs
    nc, t, pb, _ = u_ref.shape
    rows = nc * pb
    w = a_ref.shape[-1]
    for tt in range(t):
        lhs_ref[:, tt * LANES:(tt + 1) * LANES] = u_ref[:, tt].reshape(rows, LANES).astype(BF)
    d_ref[...] = _dot(lhs_ref[...], wd_ref[0])
    av = a_ref[0]
    afr, afi, abr, abi = av[0:1], av[1:2], av[2:3], av[3:4]

    def scan(init, store):
        sfr, sfi, sbr, sbi = init
        for c in range(nc):
            rf = slice(c * pb, (c + 1) * pb)
            rb = slice((nc - 1 - c) * pb, (nc - c) * pb)
            if store:
                s_ref[rf, 0:w] = sfr.astype(BF)
                s_ref[rf, w:2 * w] = sfi.astype(BF)
                s_ref[rb, 2 * w:3 * w] = sbr.astype(BF)
                s_ref[rb, 3 * w:4 * w] = sbi.astype(BF)
            sfr, sfi = (afr * sfr - afi * sfi + d_ref[rf, 0:w],
                        afr * sfi + afi * sfr + d_ref[rf, w:2 * w])
            sbr, sbi = (abr * sbr - abi * sbi + d_ref[rb, 2 * w:3 * w],
                        abr * sbi + abi * sbr + d_ref[rb, 3 * w:4 * w])
        return sfr, sfi, sbr, sbi

    zero = jnp.zeros((pb, w), F32)
    if not chain:
        fin = scan((zero, zero, zero, zero), True)
        for m in range(4):
            fin_ref[0, :, m * w:(m + 1) * w] = fin[m]
    else:
        efr, efi, ebr, ebi = scan((zero, zero, zero, zero), False)
        a2 = a2_ref[0]
        s0 = s0_ref[0, 0]
        xr, xi = s0[0:1], s0[1:2]
        for j in range(pb):
            xs_ref[j:j + 1, 0:w] = xr
            xs_ref[j:j + 1, w:2 * w] = xi
            xr, xi = (a2[0:1] * xr - a2[1:2] * xi + efr[j:j + 1],
                      a2[0:1] * xi + a2[1:2] * xr + efi[j:j + 1])
        xr, xi = s0[2:3], s0[3:4]
        for j in reversed(range(pb)):
            xs_ref[j:j + 1, 2 * w:3 * w] = xr
            xs_ref[j:j + 1, 3 * w:4 * w] = xi
            xr, xi = (a2[2:3] * xr - a2[3:4] * xi + ebr[j:j + 1],
                      a2[2:3] * xi + a2[3:4] * xr + ebi[j:j + 1])
        scan((xs_ref[:, 0:w], xs_ref[:, w:2 * w], xs_ref[:, 2 * w:3 * w], xs_ref[:, 3 * w:4 * w]), True)

    yv = _dot(lhs_ref[...], toep_ref[0]) + _dot(s_ref[...], wo_ref[0])
    dsk = dsk_ref[...]
    for tt in range(t):
        yt = yv[:, tt * LANES:(tt + 1) * LANES] + dsk * u_ref[:, tt].reshape(rows, LANES)
        y_ref[:, tt] = _gelu(yt).reshape(nc, pb, LANES)


def _s5(u4, toep, wd, wo, a8, a256, dskip, s0):
    nc, t, npb, nch = u4.shape
    nb = toep.shape[0]
    chain = s0 is not None
    pb = npb // s0.shape[1] if chain else min(16, npb)
    sw = wd.shape[-1]
    rows = nc * pb
    ublk = pl.BlockSpec((nc, t, pb, LANES), lambda kb, hh: (0, 0, hh, kb))

    def wspec(arr):
        return pl.BlockSpec((1,) + arr.shape[1:], lambda kb, hh: (kb, 0, 0))

    in_specs = [ublk, wspec(toep), wspec(wd), wspec(wo), wspec(a8)]
    args = [u4, toep, wd, wo, a8]
    scratch = [pltpu.VMEM((rows, t * LANES), BF), pltpu.VMEM((rows, sw), F32), pltpu.VMEM((rows, sw), BF)]
    if chain:
        in_specs += [wspec(a256), pl.BlockSpec((1, LANES), lambda kb, hh: (0, kb)),
                     pl.BlockSpec((1, 1, 4, sw // 4), lambda kb, hh: (kb, hh, 0, 0))]
        args += [a256, dskip, s0]
        out_specs = ublk
        out_shape = jax.ShapeDtypeStruct(u4.shape, F32)
        scratch.append(pltpu.VMEM((pb, sw), F32))
    else:
        in_specs += [pl.BlockSpec((1, LANES), lambda kb, hh: (0, kb))]
        args += [dskip]
        out_specs = [ublk, pl.BlockSpec((1, pb, sw), lambda kb, hh: (kb, hh, 0))]
        out_shape = [jax.ShapeDtypeStruct(u4.shape, F32), jax.ShapeDtypeStruct((nb, npb, sw), F32)]
    return pl.pallas_call(
        functools.partial(_s5_kernel, chain=chain),
        grid=(nb, npb // pb),
        in_specs=in_specs, out_specs=out_specs, out_shape=out_shape,
        scratch_shapes=scratch,
        compiler_params=_cparams(("parallel", "parallel")),
        name="s5_chunked",
    )(*args)


def _glu_kernel(y_ref, w_ref, b_ref, o_ref):
    y = y_ref[...]
    o_ref[...] = (y * _sigmoid(_dot(y.astype(BF), w_ref[...]) + b_ref[...])).astype(BF)


def _glu(y, w, b):
    m, n = y.shape
    tm = min(1024, m)
    return pl.pallas_call(
        _glu_kernel,
        grid=(m // tm,),
        in_specs=[pl.BlockSpec((tm, n), lambda i: (i, 0)),
                  pl.BlockSpec((n, n), lambda i: (0, 0)),
                  pl.BlockSpec((1, n), lambda i: (0, 0))],
        out_specs=pl.BlockSpec((tm, n), lambda i: (i, 0)),
        out_shape=jax.ShapeDtypeStruct((m, n), BF),
        compiler_params=_cparams(("parallel",)),
        name="s5_glu",
    )(y, w, b)


def _merge_kernel(ya_ref, yb_ref, yc_ref, ga_ref, gb_ref, gc_ref, wb_ref, o_ref):
    acc = (_sigmoid(ga_ref[...]) * _dot(ya_ref[...], wb_ref[0])
           + _sigmoid(gb_ref[...]) * _dot(yb_ref[...], wb_ref[1])
           + _sigmoid(gc_ref[...]) * _dot(yc_ref[...], wb_ref[2]))
    o_ref[...] = acc.astype(BF)


def _merge(ya, yb, yc, zg, wb):
    m, dm = ya.shape
    d = wb.shape[-1]
    tm, tn = min(1024, m), 512
    nt = d // tn
    ysp = pl.BlockSpec((tm, dm), lambda i, j: (i, 0))

    def gsp(br):
        return pl.BlockSpec((tm, tn), lambda i, j: (i, j + br * nt))

    return pl.pallas_call(
        _merge_kernel,
        grid=(m // tm, nt),
        in_specs=[ysp, ysp, ysp, gsp(0), gsp(1), gsp(2),
                  pl.BlockSpec((3, dm, tn), lambda i, j: (0, 0, j))],
        out_specs=pl.BlockSpec((tm, tn), lambda i, j: (i, j)),
        out_shape=jax.ShapeDtypeStruct((m, d), BF),
        compiler_params=_cparams(("parallel", "parallel")),
        name="branch_merge",
    )(ya, yb, yc, zg, zg, zg, wb)


def _mm_res_kernel(a_ref, w_ref, x_ref, gate_ref, ng_ref, o_ref, acc_ref):
    k = pl.program_id(1)

    @pl.when(k == 0)
    def _():
        acc_ref[...] = jnp.zeros_like(acc_ref)

    acc_ref[...] += _dot(a_ref[...], w_ref[...])

    @pl.when(k == pl.num_programs(1) - 1)
    def _():
        o_ref[...] = x_ref[...] + gate_ref[0] * _rms(acc_ref[...], ng_ref[...])


def _mm_res(a, w, x, mods, gate_chunk, ng, rows_per_mod):
    m, kdim = a.shape
    d = w.shape[1]
    tm = min(512, m)
    tk = 512 if kdim > 2048 else 1024
    return pl.pallas_call(
        _mm_res_kernel,
        grid=(m // tm, kdim // tk),
        in_specs=[pl.BlockSpec((tm, tk), lambda i, k: (i, k)),
                  pl.BlockSpec((tk, d), lambda i, k: (k, 0)),
                  pl.BlockSpec((tm, d), lambda i, k: (i, 0)),
                  _mod_spec(gate_chunk, d, rows_per_mod, tm),
                  pl.BlockSpec((1, d), lambda i, k: (0, 0))],
        out_specs=pl.BlockSpec((tm, d), lambda i, k: (i, 0)),
        out_shape=jax.ShapeDtypeStruct((m, d), F32),
        scratch_shapes=[pltpu.VMEM((tm, d), F32)],
        compiler_params=_cparams(("parallel", "arbitrary")),
        name="matmul_norm_residual",
    )(a, w, x, mods, ng)


FFN_HALO = 16


def _ffn_up_kernel(xp_ref, x_ref, xn_ref, g_ref, sh_ref, sc_ref, wv_ref, wg_ref,
                   cwv_ref, cwg_ref, cbv_ref, cbg_ref, o_ref, h_ref, uv_ref, ug_ref, *, tm, seq_len):
    i, j = pl.program_id(0), pl.program_id(1)
    hl = FFN_HALO

    @pl.when(j == 0)
    def _():
        g, sc, sh = g_ref[...], sc_ref[0], sh_ref[0]
        h_ref[0:hl] = _rms_mod(xp_ref[...], g, sc, sh).astype(BF)
        h_ref[hl:hl + tm] = _rms_mod(x_ref[...], g, sc, sh).astype(BF)
        h_ref[hl + tm:2 * hl + tm] = _rms_mod(xn_ref[...], g, sc, sh).astype(BF)

    h = h_ref[...]
    uv_ref[...] = _dot(h, wv_ref[...])
    ug_ref[...] = _dot(h, wg_ref[...])
    pos = lax.rem(i * tm + lax.broadcasted_iota(jnp.int32, (tm, 1), 0), seq_len)
    has_prev = pos != 0
    has_next = pos != seq_len - 1

    def conv(u_ref, cw_ref, cb_ref):
        cw = cw_ref[...]
        up = jnp.where(has_prev, u_ref[pl.ds(hl - 1, tm), :], 0.0)
        un = jnp.where(has_next, u_ref[pl.ds(hl + 1, tm), :], 0.0)
        return up * cw[0:1] + u_ref[pl.ds(hl, tm), :] * cw[1:2] + un * cw[2:3] + cb_ref[...]

    val = conv(uv_ref, cwv_ref, cbv_ref)
    gate = conv(ug_ref, cwg_ref, cbg_ref)
    o_ref[...] = (gate * _sigmoid(gate) * val).astype(BF)


def _ffn_up(x, mods, ng2, w_up, conv_w, conv_b, rows_per_mod, seq_len):
    m, d = x.shape
    f = w_up.shape[1] // 2
    tm, tf = min(512, m), 512
    nf = f // tf
    hl = FFN_HALO
    nblk = m // hl
    return pl.pallas_call(
        functools.partial(_ffn_up_kernel, tm=tm, seq_len=seq_len),
        grid=(m // tm, nf),
        in_specs=[pl.BlockSpec((hl, d), lambda i, j: (jnp.maximum(i * (tm // hl) - 1, 0), 0)),
                  pl.BlockSpec((tm, d), lambda i, j: (i, 0)),
                  pl.BlockSpec((hl, d), lambda i, j: (jnp.minimum((i + 1) * (tm // hl), nblk - 1), 0)),
                  pl.BlockSpec((1, d), lambda i, j: (0, 0)),
                  _mod_spec(3, d, rows_per_mod, tm),
                  _mod_spec(4, d, rows_per_mod, tm),
                  pl.BlockSpec((d, tf), lambda i, j: (0, j)),
                  pl.BlockSpec((d, tf), lambda i, j: (0, j + nf)),
                  pl.BlockSpec((3, tf), lambda i, j: (0, j)),
                  pl.BlockSpec((3, tf), lambda i, j: (0, j + nf)),
                  pl.BlockSpec((1, tf), lambda i, j: (0, j)),
                  pl.BlockSpec((1, tf), lambda i, j: (0, j + nf))],
        out_specs=pl.BlockSpec((tm, tf), lambda i, j: (i, j)),
        out_shape=jax.ShapeDtypeStruct((m, f), BF),
        scratch_shapes=[pltpu.VMEM((tm + 2 * hl, d), BF),
                        pltpu.VMEM((tm + 2 * hl, tf), F32),
                        pltpu.VMEM((tm + 2 * hl, tf), F32)],
        compiler_params=_cparams(("parallel", "arbitrary")),
        name="ffn_up_conv",
    )(x, x, x, ng2, mods, mods, w_up, w_up, conv_w, conv_w, conv_b, conv_b)


def _pair_swap(w):
    r = w.reshape(w.shape[:-1] + (w.shape[-1] // 2, 2))
    return jnp.stack([-r[..., 1], r[..., 0]], axis=-1).reshape(w.shape)


def _rope_tables(n_tokens, dr):
    rows = n_tokens // GRID_W
    row = jnp.broadcast_to(jnp.arange(rows, dtype=F32)[:, None], (rows, GRID_W)).reshape(-1)
    col = jnp.broadcast_to(jnp.arange(GRID_W, dtype=F32)[None, :], (rows, GRID_W)).reshape(-1)
    nf = dr // 4
    inv = ROPE_BASE ** (-jnp.arange(nf, dtype=F32) / nf)
    ang = jnp.concatenate([row[:, None] * inv, col[:, None] * inv], axis=-1)
    ang = jnp.repeat(ang, 2, axis=-1)
    pad = jnp.zeros((n_tokens, LANES - dr), F32)
    return jnp.concatenate([jnp.cos(ang), pad], axis=-1), jnp.concatenate([jnp.sin(ang), pad], axis=-1)


def kernel(x_prompt, x_sample, cache_mla_ckv, cache_mla_krope, state_ret_fwd, state_ret_bwd, state_s5_fwd, state_s5_bwd, c, c_ctx, ada_w, ada_b, norm_g, w_in, s5_lam_re, s5_lam_im, s5_log_dt, s5_b_re, s5_b_im, s5_c_re, s5_c_im, s5_d, s5_glu_w, s5_glu_b, ret_decay, ret_norm_g, mla_q_norm, mla_kv_norm, mla_w_uq, mla_w_uk, mla_w_uv, w_branch, w_out, ffn_w_up, ffn_conv_w, ffn_conv_b, ffn_w_down):
    bsz, seq, d = x_prompt.shape
    dbsz, dseq, _ = x_sample.shape
    depth = ada_w.shape[0]
    dm = d // 2
    n_heads = ret_decay.shape[-1]
    dr = cache_mla_krope.shape[-1]
    q_lora, kv_lora = mla_w_uq.shape[1], mla_w_uk.shape[1]
    g5, p5, ch5 = s5_b_re.shape[1:]
    assert dm == n_heads * LANES and dr == LANES // 2 and mla_w_uk.shape[-1] == dm
    assert seq == S5_SEG and dseq % S5_SEG == 0
    scale = float(LANES + dr) ** -0.5

    cvecs = jnp.concatenate([c_ctx[None], c, jnp.zeros((8 - 1 - dbsz, d), F32)], axis=0)
    mods_all = _ada_mods(cvecs, ada_w, ada_b)

    cos_l, sin_l = _rope_tables(dseq, dr)
    ones_tab = jnp.concatenate([jnp.ones((512, dr), F32), jnp.zeros((512, LANES - dr), F32)], axis=-1)
    zeros_tab = jnp.zeros((512, LANES), F32)

    xp = x_prompt.reshape(bsz * seq, d)
    xs = x_sample.reshape(dbsz * dseq, d)
    outs = {k: [] for k in ("ckv", "krope", "retf", "retb", "s5f", "s5b")}
    nb5 = g5 * ch5 // LANES
    gb5 = LANES // ch5

    for l in range(depth):
        cw = np.cumsum([0, dm, dm, dm, dm, dm, q_lora, kv_lora, dr, d, d, d])
        wl = w_in[l]
        w_a = jnp.concatenate([wl[:, cw[8]:cw[11]], wl[:, cw[0]:cw[5]]], axis=1).astype(BF)
        w_kr = wl[:, cw[7]:cw[8]]
        w_m = jnp.concatenate([wl[:, cw[5]:cw[7]], w_kr, _pair_swap(w_kr)], axis=1).astype(BF)
        uq = mla_w_uq[l].reshape(q_lora, n_heads, LANES + dr)
        wq_ext = jnp.concatenate([uq[..., :LANES], uq[..., LANES:], _pair_swap(uq[..., LANES:])],
                                 axis=-1).reshape(q_lora, n_heads * 2 * LANES).astype(BF)
        wkv = jnp.concatenate([mla_w_uk[l], mla_w_uv[l]], axis=1).astype(BF)
        qg = mla_q_norm[l].reshape(1, q_lora)
        kvg = mla_kv_norm[l].reshape(1, kv_lora)
        ng = norm_g[l].reshape(4, 1, d)
        glu_w = s5_glu_w[l].astype(BF)
        glu_b = s5_glu_b[l].reshape(1, dm)
        wb = w_branch[l].astype(BF)
        wo_ = w_out[l].astype(BF)
        wup = ffn_w_up[l].astype(BF)
        wdn = ffn_w_down[l].astype(BF)
        conv_w = ffn_conv_w[l]
        conv_b = ffn_conv_b[l].reshape(1, -1)
        dec = jnp.broadcast_to(ret_decay[l].reshape(2, n_heads, 1, 1), (2, n_heads, 1, LANES))
        rng = ret_norm_g[l].reshape(1, dm)
        dskip = s5_d[l].reshape(1, dm)

        px, py, pg, pa = _s5_prep(s5_lam_re[l], s5_lam_im[l], s5_log_dt[l], s5_b_re[l], s5_b_im[l],
                                  s5_c_re[l], s5_c_im[l])
        toep, wd, wo5, a8, a256 = _s5_assemble(px, py, pg, pa)

        for grp in range(2):
            ctx = grp == 0
            x = xp if ctx else xs
            nbat, slen = (bsz, seq) if ctx else (dbsz, dseq)
            m = nbat * slen
            if ctx:
                mods = mods_all[l, 0:1].reshape(1, 1, -1)
                rpm = m
            else:
                mods = mods_all[l, 1:1 + dbsz].reshape(dbsz, 1, -1)
                rpm = slen

            zg, zs = _inproj(x, mods, ng[0], w_a, rpm, 3 * d)

            nseg = slen // S5_SEG
            npb = nbat * nseg
            u = zs[:, :dm].reshape(npb, S5_SEG, dm).transpose(1, 0, 2)
            u4 = u.reshape(S5_SEG // S5_T, S5_T, npb, dm)
            if ctx:
                y4, fin = _s5(u4, toep, wd, wo5, a8, a256, dskip, None)
                fin = fin.reshape(nb5, npb, 4, gb5, p5).transpose(1, 2, 0, 3, 4).reshape(npb, 4, g5, p5)
                outs["s5f"].append(jnp.stack([fin[:, 0], fin[:, 1]], axis=-1))
                outs["s5b"].append(jnp.stack([fin[:, 2], fin[:, 3]], axis=-1))
            else:
                s0 = jnp.stack([state_s5_fwd[:, l, :, :, 0], state_s5_fwd[:, l, :, :, 1],
                                state_s5_bwd[:, l, :, :, 0], state_s5_bwd[:, l, :, :, 1]], axis=1)
                s0 = s0.reshape(dbsz, 4, nb5, gb5 * p5).transpose(2, 0, 1, 3)
                y4 = _s5(u4, toep, wd, wo5, a8, a256, dskip, s0)
            ya_tm = _glu(y4.reshape(S5_SEG * npb, dm), glu_w, glu_b)
            ya = ya_tm.reshape(S5_SEG, npb, dm).transpose(1, 0, 2).reshape(m, dm)

            zs3 = zs.reshape(nbat, slen, -1)
            if ctx:
                yb, rf, rb = _retention(zs3, dm, dec, rng, None, None, n_heads)
                outs["retf"].append(rf)
                outs["retb"].append(rb)
            else:
                yb = _retention(zs3, dm, dec, rng, state_ret_fwd[:, l], state_ret_bwd[:, l], n_heads)
            yb = yb.reshape(m, dm)

            if ctx:
                q, k, v, ckv, kr = _mla_proj(x, mods, ng[0], w_m, qg, kvg, wq_ext, wkv, ones_tab, zeros_tab,
                                             rpm, n_heads, scale)
                outs["ckv"].append(ckv.reshape(nbat, slen, kv_lora))
                outs["krope"].append(kr[:, :dr].reshape(nbat, slen, dr))
                kc = vc = None
            else:
                q, k, v, _, _ = _mla_proj(x, mods, ng[0], w_m, qg, kvg, wq_ext, wkv, cos_l, sin_l,
                                          rpm, n_heads, scale)
                kr_pad = jnp.concatenate([cache_mla_krope[:, l],
                                          jnp.zeros(cache_mla_krope.shape[:1] + cache_mla_krope.shape[2:3]
                                                    + (LANES - dr,), F32)], axis=-1)
                kc, vc = _cache_kv(cache_mla_ckv[:, l], kr_pad, wkv, n_heads)
            yc = _attention(q.reshape(nbat, slen, -1), k.reshape(nbat, slen, -1), v.reshape(nbat, slen, -1),
                            kc, vc, n_heads).reshape(m, dm)

            merged = _merge(ya, yb, yc, zg, wb)
            x = _mm_res(merged, wo_, x, mods, 2, ng[1], rpm)
            act = _ffn_up(x, mods, ng[2], wup, conv_w, conv_b, rpm, slen)
            x = _mm_res(act, wdn, x, mods, 5, ng[3], rpm)
            if ctx:
                xp = x
            else:
                xs = x

    return (xp.reshape(bsz, seq, d), xs.reshape(dbsz, dseq, d),
            jnp.stack(outs["ckv"], axis=1), jnp.stack(outs["krope"], axis=1),
            jnp.stack(outs["retf"], axis=1), jnp.stack(outs["retb"], axis=1),
            jnp.stack(outs["s5f"], axis=1), jnp.stack(outs["s5b"], axis=1))
```

```python
import functools

import numpy as np
import jax
import jax.numpy as jnp
from jax import lax
from jax.experimental import pallas as pl
from jax.experimental.pallas import tpu as pltpu

F32 = jnp.float32
BF = jnp.bfloat16

NORM_EPS = 1e-6
ROPE_BASE = 10000.0
GRID_W = 64
LANES = 128
S5_T = 8
S5_SEG = 256
RET_CHUNK_MAX = 512
VMEM_LIMIT = 56 << 20


def _cparams(sem):
    return pltpu.CompilerParams(dimension_semantics=sem, vmem_limit_bytes=VMEM_LIMIT)


def _dot(a, b):
    return jnp.dot(a, b, preferred_element_type=F32)


def _dot_nt(a, b, precision=None):
    return lax.dot_general(a, b, (((1,), (1,)), ((), ())), precision=precision, preferred_element_type=F32)


def _rms(x, g):
    return x * lax.rsqrt(jnp.mean(x * x, axis=-1, keepdims=True) + NORM_EPS) * g


def _rms_mod(x, g, sc, sh):
    return _rms(x, g) * (1.0 + sc) + sh


def _sigmoid(x):
    return 1.0 / (1.0 + jnp.exp(-x))


def _mod_spec(chunk, d, rows_per_mod, tm):
    return pl.BlockSpec((1, 1, d), lambda i, *_: ((i * tm) // rows_per_mod, 0, chunk))


def _ada_kernel(c_ref, w_ref, b_ref, o_ref):
    c = c_ref[...]
    s = (c * _sigmoid(c)).astype(BF)
    o_ref[0] = _dot(s, w_ref[0].astype(BF)) + b_ref[0]


def _ada_mods(cvecs, ada_w, ada_b):
    depth, d, n = ada_w.shape
    tn = 1024
    return pl.pallas_call(
        _ada_kernel,
        grid=(depth, n // tn),
        in_specs=[pl.BlockSpec((8, d), lambda l, j: (0, 0)),
                  pl.BlockSpec((1, d, tn), lambda l, j: (l, 0, j)),
                  pl.BlockSpec((1, 1, tn), lambda l, j: (l, 0, j))],
        out_specs=pl.BlockSpec((1, 8, tn), lambda l, j: (l, 0, j)),
        out_shape=jax.ShapeDtypeStruct((depth, 8, n), F32),
        compiler_params=_cparams(("parallel", "parallel")),
        name="ada_mods",
    )(cvecs, ada_w, ada_b.reshape(depth, 1, n))


def _inproj_kernel(x_ref, g_ref, sh_ref, sc_ref, w_ref, og_ref, os_ref, h_ref, *, n_gate_tiles):
    j = pl.program_id(1)

    @pl.when(j == 0)
    def _():
        h_ref[...] = _rms_mod(x_ref[...], g_ref[...], sc_ref[0], sh_ref[0]).astype(BF)

    r = _dot(h_ref[...], w_ref[...])

    @pl.when(j < n_gate_tiles)
    def _():
        og_ref[...] = r

    @pl.when(j >= n_gate_tiles)
    def _():
        os_ref[...] = r


def _inproj(x, mods, ng0, w_a, layer, rows_per_mod, n_gate):
    m, d = x.shape
    n = w_a.shape[-1]
    tm, tn = min(1024, m), 512
    ngt, nst = n_gate // tn, (n - n_gate) // tn
    return pl.pallas_call(
        functools.partial(_inproj_kernel, n_gate_tiles=ngt),
        grid=(m // tm, ngt + nst),
        in_specs=[pl.BlockSpec((tm, d), lambda i, j: (i, 0)),
                  pl.BlockSpec((1, d), lambda i, j: (0, 0)),
                  _mod_spec(0, d, rows_per_mod, tm),
                  _mod_spec(1, d, rows_per_mod, tm),
                  pl.BlockSpec((None, d, tn), lambda i, j: (layer, 0, j))],
        out_specs=[pl.BlockSpec((tm, tn), lambda i, j: (i, jnp.minimum(j, ngt - 1))),
                   pl.BlockSpec((tm, tn), lambda i, j: (i, jnp.maximum(j - ngt, 0)))],
        out_shape=[jax.ShapeDtypeStruct((m, n_gate), F32),
                   jax.ShapeDtypeStruct((m, n - n_gate), F32)],
        scratch_shapes=[pltpu.VMEM((tm, d), BF)],
        compiler_params=_cparams(("parallel", "arbitrary")),
        name="in_proj",
    )(x, ng0, mods, mods, w_a)


def _rope(seg, cosp, sinp):
    return seg * cosp + pltpu.roll(seg, LANES // 2, 1) * sinp


def _mla_proj_kernel(x_ref, g_ref, sh_ref, sc_ref, wm_ref, qg_ref, kvg_ref, wq_ref, wkv_ref, cos_ref, sin_ref,
                     q_ref, k_ref, v_ref, ckv_ref, kr_ref, *, q_lora, kv_lora, n_heads, scale):
    h = _rms_mod(x_ref[...], g_ref[...], sc_ref[0], sh_ref[0]).astype(BF)
    z = _dot(h, wm_ref[...])
    cosp, sinp = cos_ref[...], sin_ref[...]
    hv = n_heads * LANES

    cqn = _rms(z[:, :q_lora], qg_ref[...]).astype(BF)
    qraw = _dot(cqn, wq_ref[...])
    for hh in range(n_heads):
        b0 = hh * 2 * LANES
        q_ref[:, b0:b0 + LANES] = (qraw[:, b0:b0 + LANES] * scale).astype(BF)
        q_ref[:, b0 + LANES:b0 + 2 * LANES] = (_rope(qraw[:, b0 + LANES:b0 + 2 * LANES], cosp, sinp) * scale).astype(BF)

    ckv = _rms(z[:, q_lora:q_lora + kv_lora], kvg_ref[...])
    ckv_ref[...] = ckv
    kv = _dot(ckv.astype(BF), wkv_ref[...])
    v_ref[...] = kv[:, hv:].astype(BF)
    kr = z[:, q_lora + kv_lora:]
    kr_ref[...] = kr
    krot = _rope(kr, cosp, sinp).astype(BF)
    for hh in range(n_heads):
        b0 = hh * 2 * LANES
        k_ref[:, b0:b0 + LANES] = kv[:, hh * LANES:(hh + 1) * LANES].astype(BF)
        k_ref[:, b0 + LANES:b0 + 2 * LANES] = krot


def _mla_proj(x, mods, ng0, w_m, qg, kvg, wq_ext, wkv, cosp, sinp, rows_per_mod, n_heads, scale):
    m, d = x.shape
    q_lora, kv_lora = wq_ext.shape[0], wkv.shape[0]
    tm = min(256, m)
    tab_tiles = cosp.shape[0] // tm
    c2 = lambda i: (0, 0)
    return pl.pallas_call(
        functools.partial(_mla_proj_kernel, q_lora=q_lora, kv_lora=kv_lora, n_heads=n_heads, scale=scale),
        grid=(m // tm,),
        in_specs=[pl.BlockSpec((tm, d), lambda i: (i, 0)),
                  pl.BlockSpec((1, d), c2),
                  _mod_spec(0, d, rows_per_mod, tm),
                  _mod_spec(1, d, rows_per_mod, tm),
                  pl.BlockSpec(w_m.shape, c2),
                  pl.BlockSpec((1, q_lora), c2),
                  pl.BlockSpec((1, kv_lora), c2),
                  pl.BlockSpec(wq_ext.shape, c2),
                  pl.BlockSpec(wkv.shape, c2),
                  pl.BlockSpec((tm, LANES), lambda i: (i % tab_tiles, 0)),
                  pl.BlockSpec((tm, LANES), lambda i: (i % tab_tiles, 0))],
        out_specs=[pl.BlockSpec((tm, 2 * n_heads * LANES), lambda i: (i, 0)),
                   pl.BlockSpec((tm, 2 * n_heads * LANES), lambda i: (i, 0)),
                   pl.BlockSpec((tm, n_heads * LANES), lambda i: (i, 0)),
                   pl.BlockSpec((tm, kv_lora), lambda i: (i, 0)),
                   pl.BlockSpec((tm, LANES), lambda i: (i, 0))],
        out_shape=[jax.ShapeDtypeStruct((m, 2 * n_heads * LANES), BF),
                   jax.ShapeDtypeStruct((m, 2 * n_heads * LANES), BF),
                   jax.ShapeDtypeStruct((m, n_heads * LANES), BF),
                   jax.ShapeDtypeStruct((m, kv_lora), F32),
                   jax.ShapeDtypeStruct((m, LANES), F32)],
        compiler_params=_cparams(("parallel",)),
        name="mla_proj",
    )(x, ng0, mods, mods, w_m, qg, kvg, wq_ext, wkv, cosp, sinp)


def _cache_kv_kernel(ckv_ref, kr_ref, wkv_ref, k_ref, v_ref, *, n_heads):
    kv = _dot(ckv_ref[0].astype(BF), wkv_ref[...])
    v_ref[0] = kv[:, n_heads * LANES:].astype(BF)
    kr = kr_ref[0].astype(BF)
    for hh in range(n_heads):
        b0 = hh * 2 * LANES
        k_ref[0, :, b0:b0 + LANES] = kv[:, hh * LANES:(hh + 1) * LANES].astype(BF)
        k_ref[0, :, b0 + LANES:b0 + 2 * LANES] = kr


def _cache_kv(ckv, kr_pad, wkv, n_heads):
    b, s, kvl = ckv.shape
    return pl.pallas_call(
        functools.partial(_cache_kv_kernel, n_heads=n_heads),
        grid=(b,),
        in_specs=[pl.BlockSpec((1, s, kvl), lambda i: (i, 0, 0)),
                  pl.BlockSpec((1, s, LANES), lambda i: (i, 0, 0)),
                  pl.BlockSpec(wkv.shape, lambda i: (0, 0))],
        out_specs=[pl.BlockSpec((1, s, 2 * n_heads * LANES), lambda i: (i, 0, 0)),
                   pl.BlockSpec((1, s, n_heads * LANES), lambda i: (i, 0, 0))],
        out_shape=[jax.ShapeDtypeStruct((b, s, 2 * n_heads * LANES), BF),
                   jax.ShapeDtypeStruct((b, s, n_heads * LANES), BF)],
        compiler_params=_cparams(("parallel",)),
        name="mla_cache_kv",
    )(ckv, kr_pad, wkv)


def _softmax_pv(q, k, v, kc=None, vc=None):
    s = _dot_nt(q, k)
    mx = jnp.max(s, axis=-1, keepdims=True)
    if kc is not None:
        sc = _dot_nt(q, kc)
        mx = jnp.maximum(mx, jnp.max(sc, axis=-1, keepdims=True))
    p = jnp.exp(s - mx)
    den = jnp.sum(p, axis=-1, keepdims=True)
    o = _dot(p.astype(BF), v)
    if kc is not None:
        pc = jnp.exp(sc - mx)
        den = den + jnp.sum(pc, axis=-1, keepdims=True)
        o = o + _dot(pc.astype(BF), vc)
    return (o / den).astype(BF)


def _attn_cache_kernel(q_ref, k_ref, v_ref, kc_ref, vc_ref, o_ref):
    o_ref[0] = _softmax_pv(q_ref[0], k_ref[0], v_ref[0], kc_ref[0], vc_ref[0])


def _attn_heads_kernel(q_ref, k_ref, v_ref, o_ref, *, n_heads):
    for hh in range(n_heads):
        qk = slice(hh * 2 * LANES, (hh + 1) * 2 * LANES)
        vv = slice(hh * LANES, (hh + 1) * LANES)
        o_ref[0, :, vv] = _softmax_pv(q_ref[0, :, qk], k_ref[0, :, qk], v_ref[0, :, vv])


def _attention(q, k, v, kc, vc, n_heads):
    b, l, _ = q.shape
    out_shape = jax.ShapeDtypeStruct((b, l, n_heads * LANES), BF)
    if kc is None:
        return pl.pallas_call(
            functools.partial(_attn_heads_kernel, n_heads=n_heads),
            grid=(b,),
            in_specs=[pl.BlockSpec((1, l, 2 * n_heads * LANES), lambda bi: (bi, 0, 0)),
                      pl.BlockSpec((1, l, 2 * n_heads * LANES), lambda bi: (bi, 0, 0)),
                      pl.BlockSpec((1, l, n_heads * LANES), lambda bi: (bi, 0, 0))],
            out_specs=pl.BlockSpec((1, l, n_heads * LANES), lambda bi: (bi, 0, 0)),
            out_shape=out_shape,
            compiler_params=_cparams(("parallel",)),
            name="mla_attention_ctx",
        )(q, k, v)
    s, sc_ = k.shape[1], kc.shape[1]
    tq = min(256, l)
    return pl.pallas_call(
        _attn_cache_kernel,
        grid=(b, n_heads, l // tq),
        in_specs=[pl.BlockSpec((1, tq, 2 * LANES), lambda bi, h, i: (bi, i, h)),
                  pl.BlockSpec((1, s, 2 * LANES), lambda bi, h, i: (bi, 0, h)),
                  pl.BlockSpec((1, s, LANES), lambda bi, h, i: (bi, 0, h)),
                  pl.BlockSpec((1, sc_, 2 * LANES), lambda bi, h, i: (bi, 0, h)),
                  pl.BlockSpec((1, sc_, LANES), lambda bi, h, i: (bi, 0, h))],
        out_specs=pl.BlockSpec((1, tq, LANES), lambda bi, h, i: (bi, i, h)),
        out_shape=out_shape,
        compiler_params=_cparams(("parallel", "parallel", "arbitrary")),
        name="mla_attention",
    )(q, k, v, kc, vc)


def _ret_kernel(*refs, with_state, n_chunks, c_len, n_hp, k_scale):
    if with_state:
        (q_ref, k_ref, v_ref, g_ref, dec_ref, ng_ref, s0f_ref, s0b_ref,
         y_ref, dm_ref, kvf_ref, kvb_ref, sf_ref, sb_ref) = refs
    else:
        (q_ref, k_ref, v_ref, g_ref, dec_ref, ng_ref,
         y_ref, ff_ref, fb_ref, dm_ref, kvf_ref, kvb_ref, sf_ref, sb_ref) = refs
    reps = c_len // LANES

    @pl.when(pl.program_id(1) == 0)
    def _():
        di = (lax.broadcasted_iota(jnp.int32, (c_len, c_len), 0)
              - lax.broadcasted_iota(jnp.int32, (c_len, c_len), 1)).astype(F32)
        for hp in range(n_hp):
            lgf = jnp.tile(-jnp.exp(dec_ref[0, hp]), (1, reps))
            lgb = jnp.tile(-jnp.exp(dec_ref[1, hp]), (1, reps))
            dm_ref[hp] = jnp.where(di > 0, jnp.exp(lgf * jnp.maximum(di, 0.0)),
                                   jnp.where(di < 0, jnp.exp(lgb * jnp.maximum(-di, 0.0)), 2.0))

    ii = lax.broadcasted_iota(jnp.int32, (c_len, LANES), 0).astype(F32)
    need_state = with_state or n_chunks > 1
    for hp in range(n_hp):
        cs = slice(hp * LANES, (hp + 1) * LANES)
        lgf = -jnp.exp(dec_ref[0, hp])
        lgb = -jnp.exp(dec_ref[1, hp])
        dk_f = jnp.exp(lgf * (c_len - 1.0 - ii))
        dk_b = jnp.exp(lgb * ii)
        ds_f = jnp.exp(lgf * c_len)
        ds_b = jnp.exp(lgb * c_len)

        for c in range(n_chunks):
            rows = slice(c * c_len, (c + 1) * c_len)
            kc = k_ref[0, rows, cs] * k_scale
            vc = v_ref[0, rows, cs].astype(BF)
            kvf_ref[c] = _dot((kc * dk_f).T.astype(BF), vc)
            kvb_ref[c] = _dot((kc * dk_b).T.astype(BF), vc)

        if with_state:
            s_f, s_b = s0f_ref[0, hp], s0b_ref[0, hp]
        else:
            s_f = s_b = jnp.zeros((LANES, LANES), F32)
        for c in range(n_chunks):
            sf_ref[c] = s_f
            s_f = ds_f * s_f + kvf_ref[c]
        for c in reversed(range(n_chunks)):
            sb_ref[c] = s_b
            s_b = ds_b * s_b + kvb_ref[c]
        if not with_state:
            ff_ref[0, hp] = s_f
            fb_ref[0, hp] = s_b

        ng = ng_ref[:, cs]
        dq_f = jnp.exp(lgf * (ii + 1.0))
        dq_b = jnp.exp(lgb * (c_len - ii))
        for c in range(n_chunks):
            rows = slice(c * c_len, (c + 1) * c_len)
            qc = q_ref[0, rows, cs]
            kc = (k_ref[0, rows, cs] * k_scale).astype(BF)
            vc = v_ref[0, rows, cs].astype(BF)
            att = _dot_nt(qc.astype(BF), kc) * dm_ref[hp]
            o = _dot(att.astype(BF), vc)
            if need_state:
                o = (o + _dot((qc * dq_f).astype(BF), sf_ref[c].astype(BF))
                     + _dot((qc * dq_b).astype(BF), sb_ref[c].astype(BF)))
            mu = jnp.mean(o, axis=-1, keepdims=True)
            oc = o - mu
            var = jnp.mean(oc * oc, axis=-1, keepdims=True)
            on = oc * lax.rsqrt(var + NORM_EPS)
            gg = g_ref[0, rows, cs]
            y_ref[0, rows, cs] = (gg * _sigmoid(gg) * (on * ng)).astype(BF)


def _retention(zs, col0, dec, ng, s0f, s0b, n_heads):
    b, l, _ = zs.shape
    c_len = min(RET_CHUNK_MAX, l)
    nc = l // c_len
    with_state = s0f is not None
    n_hp = 1 if nc > 1 else n_heads
    hw = n_hp * LANES
    cb = col0 // hw

    def zspec(seg):
        return pl.BlockSpec((1, l, hw), lambda hg, bi: (bi, 0, cb + seg * (n_heads // n_hp) + hg))

    st_spec = pl.BlockSpec((1, n_hp, LANES, LANES), lambda hg, bi: (bi, hg, 0, 0))
    in_specs = [zspec(0), zspec(1), zspec(2), zspec(3),
                pl.BlockSpec((2, n_hp, 1, LANES), lambda hg, bi: (0, hg, 0, 0)),
                pl.BlockSpec((1, hw), lambda hg, bi: (0, hg))]
    args = [zs, zs, zs, zs, dec, ng]
    y_spec = pl.BlockSpec((1, l, hw), lambda hg, bi: (bi, 0, hg))
    y_shape = jax.ShapeDtypeStruct((b, l, n_heads * LANES), BF)
    if with_state:
        in_specs += [st_spec, st_spec]
        args += [s0f, s0b]
        out_specs, out_shape = y_spec, y_shape
    else:
        st_shape = jax.ShapeDtypeStruct((b, n_heads, LANES, LANES), F32)
        out_specs, out_shape = [y_spec, st_spec, st_spec], [y_shape, st_shape, st_shape]
    scratch = ([pltpu.VMEM((n_hp, c_len, c_len), F32)]
               + [pltpu.VMEM((nc, LANES, LANES), F32) for _ in range(4)])
    return pl.pallas_call(
        functools.partial(_ret_kernel, with_state=with_state, n_chunks=nc, c_len=c_len, n_hp=n_hp,
                          k_scale=float(LANES) ** -0.5),
        grid=(n_heads // n_hp, b),
        in_specs=in_specs, out_specs=out_specs, out_shape=out_shape,
        scratch_shapes=scratch,
        compiler_params=_cparams(("arbitrary", "arbitrary")),
        name="retention",
    )(*args)


def _s5_prep_kernel(lre_ref, lim_ref, ldt_ref, btre_ref, btim_ref, cre_ref, cim_ref,
                    toep_ref, wd_ref, wo_ref, av_ref, *, ch, p, seg_chunks):
    t = S5_T
    gw = (LANES // ch) * p
    btre, btim, cre, cim = btre_ref[...], btim_ref[...], cre_ref[...], cim_ref[...]
    r1 = lax.broadcasted_iota(jnp.int32, (LANES, LANES), 0)
    c1 = lax.broadcasted_iota(jnp.int32, (LANES, LANES), 1)
    same_group = (r1 // ch) == (c1 // ch)
    lane_valid = c1 < p
    r5 = lax.broadcasted_iota(jnp.int32, (LANES, gw), 0)
    c5 = lax.broadcasted_iota(jnp.int32, (LANES, gw), 1)
    own_state = (r5 // ch) == (c5 // p)

    def spread(x):
        x2 = x + pltpu.roll(x, LANES // 2, 1)
        return jnp.where(own_state, jnp.concatenate([x2] * (gw // LANES), axis=1), 0.0)

    xs, ys, lags = [], [], []
    for d in range(2):
        lr, li, dt = lre_ref[d], lim_ref[d], jnp.exp(ldt_ref[d])
        ar, ai = lr * dt, li * dt
        mag = jnp.exp(ar)
        abr, abi = mag * jnp.cos(ai), mag * jnp.sin(ai)
        nr, ni = abr - 1.0, abi
        den = lr * lr + li * li
        cr = (nr * lr + ni * li) / den
        ci = (ni * lr - nr * li) / den
        pr, pi = jnp.ones_like(lr), jnp.zeros_like(lr)
        xd, yd, gd = [], [], []
        for kk in range(t + 1):
            if kk < t:
                wr, wi = cr * pr - ci * pi, cr * pi + ci * pr
                xr, xi = wr * btre - wi * btim, wr * btim + wi * btre
                xd.append((xr, xi))
                hi = lax.Precision.HIGHEST
                gd.append(jnp.where(same_group, _dot_nt(xr, cre, hi) - _dot_nt(xi, cim, hi), 0.0))
            if kk >= 1:
                yd.append((cre * pr - cim * pi, -(cre * pi + cim * pr)))
            if kk == t:
                qr, qi = jnp.where(lane_valid, pr, 0.0), jnp.where(lane_valid, pi, 0.0)
            pr, pi = pr * abr - pi * abi, pr * abi + pi * abr
        xs.append(xd)
        ys.append(yd)
        lags.append(gd)
        inv = 1.0 / ch
        av_ref[0, 2 * d:2 * d + 1] = jnp.sum(spread(qr), axis=0, keepdims=True) * inv
        av_ref[0, 2 * d + 1:2 * d + 2] = jnp.sum(spread(qi), axis=0, keepdims=True) * inv
        for _ in range(seg_chunks.bit_length() - 1):
            qr, qi = qr * qr - qi * qi, 2.0 * qr * qi
        av_ref[0, 4 + 2 * d:5 + 2 * d] = jnp.sum(spread(qr), axis=0, keepdims=True) * inv
        av_ref[0, 5 + 2 * d:6 + 2 * d] = jnp.sum(spread(qi), axis=0, keepdims=True) * inv

    for ti in range(t):
        for to in range(t):
            lag = to - ti
            tile = lags[0][lag] if lag > 0 else (lags[1][-lag] if lag < 0 else lags[0][0] + lags[1][0])
            toep_ref[0, ti * LANES:(ti + 1) * LANES, to * LANES:(to + 1) * LANES] = tile.astype(BF)
    for tt in range(t):
        rows = slice(tt * LANES, (tt + 1) * LANES)
        xf, xb = xs[0][t - 1 - tt], xs[1][tt]
        yf, yb = ys[0][tt], ys[1][t - 1 - tt]
        for m, (xv, yv) in enumerate(((xf[0], yf[0]), (xf[1], yf[1]), (xb[0], yb[0]), (xb[1], yb[1]))):
            wd_ref[0, rows, m * gw:(m + 1) * gw] = spread(xv).astype(BF)
            wo_ref[0, m * gw:(m + 1) * gw, rows] = spread(yv).T.astype(BF)


def _s5_prep(lam_re, lam_im, log_dt, b_re, b_im, c_re, c_im):
    _, g, p = lam_re.shape
    ch = b_re.shape[-1]
    assert p == LANES // 2 and LANES % ch == 0
    nb = g * ch // LANES
    gw = (LANES // ch) * p
    t = S5_T
    seg_chunks = S5_SEG // t
    assert seg_chunks & (seg_chunks - 1) == 0

    def prow(a, padval):
        return jnp.pad(jnp.repeat(a, ch, axis=1), ((0, 0), (0, 0), (0, LANES - p)), constant_values=padval)

    def wrow(a):
        return jnp.pad(a.reshape(g * ch, p), ((0, 0), (0, LANES - p)))

    lre = prow(lam_re, -0.5)
    lim = prow(lam_im, 0.0)
    ldt = prow(jnp.broadcast_to(log_dt[..., None], lam_re.shape), 0.0)
    lspec = pl.BlockSpec((2, LANES, LANES), lambda i: (0, i, 0))
    wspec = pl.BlockSpec((LANES, LANES), lambda i: (i, 0))
    return pl.pallas_call(
        functools.partial(_s5_prep_kernel, ch=ch, p=p, seg_chunks=seg_chunks),
        grid=(nb,),
        in_specs=[lspec, lspec, lspec, wspec, wspec, wspec, wspec],
        out_specs=[pl.BlockSpec((1, t * LANES, t * LANES), lambda i: (i, 0, 0)),
                   pl.BlockSpec((1, t * LANES, 4 * gw), lambda i: (i, 0, 0)),
                   pl.BlockSpec((1, 4 * gw, t * LANES), lambda i: (i, 0, 0)),
                   pl.BlockSpec((1, 8, gw), lambda i: (i, 0, 0))],
        out_shape=[jax.ShapeDtypeStruct((nb, t * LANES, t * LANES), BF),
                   jax.ShapeDtypeStruct((nb, t * LANES, 4 * gw), BF),
                   jax.ShapeDtypeStruct((nb, 4 * gw, t * LANES), BF),
                   jax.ShapeDtypeStruct((nb, 8, gw), F32)],
        compiler_params=_cparams(("parallel",)),
        name="s5_prep",
    )(lre, lim, ldt, wrow(jnp.swapaxes(b_re, 1, 2)), wrow(jnp.swapaxes(b_im, 1, 2)), wrow(c_re), wrow(c_im))


def _gelu(x):
    return 0.5 * x * (1.0 + jnp.tanh(0.7978845608028654 * (x + 0.044715 * (x * x * x))))


def _s5_kernel(*refs, chain):
    if chain:
        (u_ref, toep_ref, wd_ref, wo_ref, a_ref, dsk_ref, s0_ref,
         y_ref, lhs_ref, d_ref, s_ref, xs_ref) = refs
    else:
        (u_ref, toep_ref, wd_ref, wo_ref, a_ref, dsk_ref,
         y_ref, fin_ref, lhs_ref, d_ref, s_ref) = refs
    nc, t, pb, _ = u_ref.shape
    rows = nc * pb
    w = a_ref.shape[-1]
    for tt in range(t):
        lhs_ref[:, tt * LANES:(tt + 1) * LANES] = u_ref[:, tt].reshape(rows, LANES).astype(BF)
    d_ref[...] = _dot(lhs_ref[...], wd_ref[0])
    av = a_ref[0]
    afr, afi, abr, abi = av[0:1], av[1:2], av[2:3], av[3:4]

    def scan(init, store):
        sfr, sfi, sbr, sbi = init
        for c in range(nc):
            rf = slice(c * pb, (c + 1) * pb)
            rb = slice((nc - 1 - c) * pb, (nc - c) * pb)
            if store:
                s_ref[rf, 0:w] = sfr.astype(BF)
                s_ref[rf, w:2 * w] = sfi.astype(BF)
                s_ref[rb, 2 * w:3 * w] = sbr.astype(BF)
                s_ref[rb, 3 * w:4 * w] = sbi.astype(BF)
            sfr, sfi = (afr * sfr - afi * sfi + d_ref[rf, 0:w],
                        afr * sfi + afi * sfr + d_ref[rf, w:2 * w])
            sbr, sbi = (abr * sbr - abi * sbi + d_ref[rb, 2 * w:3 * w],
                        abr * sbi + abi * sbr + d_ref[rb, 3 * w:4 * w])
        return sfr, sfi, sbr, sbi

    zero = jnp.zeros((pb, w), F32)
    if not chain:
        fin = scan((zero, zero, zero, zero), True)
        for m in range(4):
            fin_ref[0, :, m * w:(m + 1) * w] = fin[m]
    else:
        efr, efi, ebr, ebi = scan((zero, zero, zero, zero), False)
        s0 = s0_ref[0, 0]
        xr, xi = s0[0:1], s0[1:2]
        for j in range(pb):
            xs_ref[j:j + 1, 0:w] = xr
            xs_ref[j:j + 1, w:2 * w] = xi
            xr, xi = (av[4:5] * xr - av[5:6] * xi + efr[j:j + 1],
                      av[4:5] * xi + av[5:6] * xr + efi[j:j + 1])
        xr, xi = s0[2:3], s0[3:4]
        for j in reversed(range(pb)):
            xs_ref[j:j + 1, 2 * w:3 * w] = xr
            xs_ref[j:j + 1, 3 * w:4 * w] = xi
            xr, xi = (av[6:7] * xr - av[7:8] * xi + ebr[j:j + 1],
                      av[6:7] * xi + av[7:8] * xr + ebi[j:j + 1])
        scan((xs_ref[:, 0:w], xs_ref[:, w:2 * w], xs_ref[:, 2 * w:3 * w], xs_ref[:, 3 * w:4 * w]), True)

    yv = _dot(lhs_ref[...], toep_ref[0]) + _dot(s_ref[...], wo_ref[0])
    dsk = dsk_ref[...]
    for tt in range(t):
        yt = yv[:, tt * LANES:(tt + 1) * LANES] + dsk * u_ref[:, tt].reshape(rows, LANES)
        y_ref[:, tt] = _gelu(yt).reshape(nc, pb, LANES)


def _s5(u4, toep, wd, wo, av, dskip, s0):
    nc, t, npb, nch = u4.shape
    nb = toep.shape[0]
    chain = s0 is not None
    pb = npb // s0.shape[1] if chain else min(16, npb)
    sw = wd.shape[-1]
    rows = nc * pb
    ublk = pl.BlockSpec((nc, t, pb, LANES), lambda kb, hh: (0, 0, hh, kb))

    def wspec(arr):
        return pl.BlockSpec((1,) + arr.shape[1:], lambda kb, hh: (kb, 0, 0))

    in_specs = [ublk, wspec(toep), wspec(wd), wspec(wo), wspec(av),
                pl.BlockSpec((1, LANES), lambda kb, hh: (0, kb))]
    args = [u4, toep, wd, wo, av, dskip]
    scratch = [pltpu.VMEM((rows, t * LANES), BF), pltpu.VMEM((rows, sw), F32), pltpu.VMEM((rows, sw), BF)]
    if chain:
        in_specs += [pl.BlockSpec((1, 1, 4, sw // 4), lambda kb, hh: (kb, hh, 0, 0))]
        args += [s0]
        out_specs = ublk
        out_shape = jax.ShapeDtypeStruct(u4.shape, F32)
        scratch.append(pltpu.VMEM((pb, sw), F32))
    else:
        out_specs = [ublk, pl.BlockSpec((1, pb, sw), lambda kb, hh: (kb, hh, 0))]
        out_shape = [jax.ShapeDtypeStruct(u4.shape, F32), jax.ShapeDtypeStruct((nb, npb, sw), F32)]
    return pl.pallas_call(
        functools.partial(_s5_kernel, chain=chain),
        grid=(nb, npb // pb),
        in_specs=in_specs, out_specs=out_specs, out_shape=out_shape,
        scratch_shapes=scratch,
        compiler_params=_cparams(("parallel", "parallel")),
        name="s5_chunked",
    )(*args)


def _glu_kernel(y_ref, w_ref, b_ref, o_ref):
    y = y_ref[...]
    o_ref[...] = (y * _sigmoid(_dot(y.astype(BF), w_ref[...]) + b_ref[...])).astype(BF)


def _glu(y, w, layer, b):
    m, n = y.shape
    tm = min(1024, m)
    return pl.pallas_call(
        _glu_kernel,
        grid=(m // tm,),
        in_specs=[pl.BlockSpec((tm, n), lambda i: (i, 0)),
                  pl.BlockSpec((None, n, n), lambda i: (layer, 0, 0)),
                  pl.BlockSpec((1, n), lambda i: (0, 0))],
        out_specs=pl.BlockSpec((tm, n), lambda i: (i, 0)),
        out_shape=jax.ShapeDtypeStruct((m, n), BF),
        compiler_params=_cparams(("parallel",)),
        name="s5_glu",
    )(y, w, b)


def _merge_kernel(ya_ref, yb_ref, yc_ref, ga_ref, gb_ref, gc_ref, wb_ref, o_ref):
    acc = (_sigmoid(ga_ref[...]) * _dot(ya_ref[...], wb_ref[0])
           + _sigmoid(gb_ref[...]) * _dot(yb_ref[...], wb_ref[1])
           + _sigmoid(gc_ref[...]) * _dot(yc_ref[...], wb_ref[2]))
    o_ref[...] = acc.astype(BF)


def _merge(ya, yb, yc, zg, wb, layer):
    m, dm = ya.shape
    d = wb.shape[-1]
    tm, tn = min(1024, m), 512
    nt = d // tn
    ysp = pl.BlockSpec((tm, dm), lambda i, j: (i, 0))

    def gsp(br):
        return pl.BlockSpec((tm, tn), lambda i, j: (i, j + br * nt))

    return pl.pallas_call(
        _merge_kernel,
        grid=(m // tm, nt),
        in_specs=[ysp, ysp, ysp, gsp(0), gsp(1), gsp(2),
                  pl.BlockSpec((None, 3, dm, tn), lambda i, j: (layer, 0, 0, j))],
        out_specs=pl.BlockSpec((tm, tn), lambda i, j: (i, j)),
        out_shape=jax.ShapeDtypeStruct((m, d), BF),
        compiler_params=_cparams(("parallel", "parallel")),
        name="branch_merge",
    )(ya, yb, yc, zg, zg, zg, wb)


def _mm_res_kernel(a_ref, w_ref, x_ref, gate_ref, ng_ref, o_ref, acc_ref):
    k = pl.program_id(1)

    @pl.when(k == 0)
    def _():
        acc_ref[...] = jnp.zeros_like(acc_ref)

    acc_ref[...] += _dot(a_ref[...], w_ref[...])

    @pl.when(k == pl.num_programs(1) - 1)
    def _():
        o_ref[...] = x_ref[...] + gate_ref[0] * _rms(acc_ref[...], ng_ref[...])


def _mm_res(a, w, layer, x, mods, gate_chunk, ng, rows_per_mod):
    m, kdim = a.shape
    d = w.shape[-1]
    tm = min(512, m)
    tk = 512 if kdim > 2048 else 1024
    return pl.pallas_call(
        _mm_res_kernel,
        grid=(m // tm, kdim // tk),
        in_specs=[pl.BlockSpec((tm, tk), lambda i, k: (i, k)),
                  pl.BlockSpec((None, tk, d), lambda i, k: (layer, k, 0)),
                  pl.BlockSpec((tm, d), lambda i, k: (i, 0)),
                  _mod_spec(gate_chunk, d, rows_per_mod, tm),
                  pl.BlockSpec((1, d), lambda i, k: (0, 0))],
        out_specs=pl.BlockSpec((tm, d), lambda i, k: (i, 0)),
        out_shape=jax.ShapeDtypeStruct((m, d), F32),
        scratch_shapes=[pltpu.VMEM((tm, d), F32)],
        compiler_params=_cparams(("parallel", "arbitrary")),
        name="matmul_norm_residual",
    )(a, w, x, mods, ng)


FFN_HALO = 16


def _ffn_up_kernel(xp_ref, x_ref, xn_ref, g_ref, sh_ref, sc_ref, wv_ref, wg_ref,
                   cwv_ref, cwg_ref, cbv_ref, cbg_ref, o_ref, h_ref, uv_ref, ug_ref, *, tm, seq_len):
    i, j = pl.program_id(0), pl.program_id(1)
    hl = FFN_HALO

    @pl.when(j == 0)
    def _():
        g, sc, sh = g_ref[...], sc_ref[0], sh_ref[0]
        h_ref[0:hl] = _rms_mod(xp_ref[...], g, sc, sh).astype(BF)
        h_ref[hl:hl + tm] = _rms_mod(x_ref[...], g, sc, sh).astype(BF)
        h_ref[hl + tm:2 * hl + tm] = _rms_mod(xn_ref[...], g, sc, sh).astype(BF)

    h = h_ref[...]
    uv_ref[...] = _dot(h, wv_ref[...])
    ug_ref[...] = _dot(h, wg_ref[...])
    pos = lax.rem(i * tm + lax.broadcasted_iota(jnp.int32, (tm, 1), 0), seq_len)
    has_prev = pos != 0
    has_next = pos != seq_len - 1

    def conv(u_ref, cw_ref, cb_ref):
        cw = cw_ref[...]
        up = jnp.where(has_prev, u_ref[pl.ds(hl - 1, tm), :], 0.0)
        un = jnp.where(has_next, u_ref[pl.ds(hl + 1, tm), :], 0.0)
        return up * cw[0:1] + u_ref[pl.ds(hl, tm), :] * cw[1:2] + un * cw[2:3] + cb_ref[...]

    val = conv(uv_ref, cwv_ref, cbv_ref)
    gate = conv(ug_ref, cwg_ref, cbg_ref)
    o_ref[...] = (gate * _sigmoid(gate) * val).astype(BF)


def _ffn_up(x, mods, ng2, w_up, layer, conv_w, conv_b, rows_per_mod, seq_len):
    m, d = x.shape
    f = w_up.shape[-1] // 2
    tm, tf = min(512, m), 512
    nf = f // tf
    hl = FFN_HALO
    nblk = m // hl
    return pl.pallas_call(
        functools.partial(_ffn_up_kernel, tm=tm, seq_len=seq_len),
        grid=(m // tm, nf),
        in_specs=[pl.BlockSpec((hl, d), lambda i, j: (jnp.maximum(i * (tm // hl) - 1, 0), 0)),
                  pl.BlockSpec((tm, d), lambda i, j: (i, 0)),
                  pl.BlockSpec((hl, d), lambda i, j: (jnp.minimum((i + 1) * (tm // hl), nblk - 1), 0)),
                  pl.BlockSpec((1, d), lambda i, j: (0, 0)),
                  _mod_spec(3, d, rows_per_mod, tm),
                  _mod_spec(4, d, rows_per_mod, tm),
                  pl.BlockSpec((None, d, tf), lambda i, j: (layer, 0, j)),
                  pl.BlockSpec((None, d, tf), lambda i, j: (layer, 0, j + nf)),
                  pl.BlockSpec((3, tf), lambda i, j: (0, j)),
                  pl.BlockSpec((3, tf), lambda i, j: (0, j + nf)),
                  pl.BlockSpec((1, tf), lambda i, j: (0, j)),
                  pl.BlockSpec((1, tf), lambda i, j: (0, j + nf))],
        out_specs=pl.BlockSpec((tm, tf), lambda i, j: (i, j)),
        out_shape=jax.ShapeDtypeStruct((m, f), BF),
        scratch_shapes=[pltpu.VMEM((tm + 2 * hl, d), BF),
                        pltpu.VMEM((tm + 2 * hl, tf), F32),
                        pltpu.VMEM((tm + 2 * hl, tf), F32)],
        compiler_params=_cparams(("parallel", "arbitrary")),
        name="ffn_up_conv",
    )(x, x, x, ng2, mods, mods, w_up, w_up, conv_w, conv_w, conv_b, conv_b)


def _pair_swap(w):
    r = w.reshape(w.shape[:-1] + (w.shape[-1] // 2, 2))
    return jnp.stack([-r[..., 1], r[..., 0]], axis=-1).reshape(w.shape)


def _rope_tables(n_tokens, dr):
    rows = n_tokens // GRID_W
    row = jnp.broadcast_to(jnp.arange(rows, dtype=F32)[:, None], (rows, GRID_W)).reshape(-1)
    col = jnp.broadcast_to(jnp.arange(GRID_W, dtype=F32)[None, :], (rows, GRID_W)).reshape(-1)
    nf = dr // 4
    inv = ROPE_BASE ** (-jnp.arange(nf, dtype=F32) / nf)
    ang = jnp.concatenate([row[:, None] * inv, col[:, None] * inv], axis=-1)
    ang = jnp.repeat(ang, 2, axis=-1)
    pad = jnp.zeros((n_tokens, LANES - dr), F32)
    return jnp.concatenate([jnp.cos(ang), pad], axis=-1), jnp.concatenate([jnp.sin(ang), pad], axis=-1)


def kernel(x_prompt, x_sample, cache_mla_ckv, cache_mla_krope, state_ret_fwd, state_ret_bwd, state_s5_fwd, state_s5_bwd, c, c_ctx, ada_w, ada_b, norm_g, w_in, s5_lam_re, s5_lam_im, s5_log_dt, s5_b_re, s5_b_im, s5_c_re, s5_c_im, s5_d, s5_glu_w, s5_glu_b, ret_decay, ret_norm_g, mla_q_norm, mla_kv_norm, mla_w_uq, mla_w_uk, mla_w_uv, w_branch, w_out, ffn_w_up, ffn_conv_w, ffn_conv_b, ffn_w_down):
    bsz, seq, d = x_prompt.shape
    dbsz, dseq, _ = x_sample.shape
    depth = ada_w.shape[0]
    dm = d // 2
    n_heads = ret_decay.shape[-1]
    dr = cache_mla_krope.shape[-1]
    q_lora, kv_lora = mla_w_uq.shape[1], mla_w_uk.shape[1]
    g5, p5, ch5 = s5_b_re.shape[1:]
    assert dm == n_heads * LANES and dr == LANES // 2 and mla_w_uk.shape[-1] == dm
    assert seq == S5_SEG and dseq % S5_SEG == 0
    scale = float(LANES + dr) ** -0.5

    cvecs = jnp.concatenate([c_ctx[None], c, jnp.zeros((8 - 1 - dbsz, d), F32)], axis=0)
    mods_all = _ada_mods(cvecs, ada_w, ada_b)

    cos_l, sin_l = _rope_tables(dseq, dr)
    ones_tab = jnp.concatenate([jnp.ones((256, dr), F32), jnp.zeros((256, LANES - dr), F32)], axis=-1)
    zeros_tab = jnp.zeros((256, LANES), F32)

    cw = np.cumsum([0, dm, dm, dm, dm, dm, q_lora, kv_lora, dr, d, d, d])
    w_a = jnp.concatenate([w_in[:, :, cw[8]:cw[11]], w_in[:, :, cw[0]:cw[5]]], axis=2).astype(BF)
    w_kr = w_in[:, :, cw[7]:cw[8]]
    w_m = jnp.concatenate([w_in[:, :, cw[5]:cw[7]], w_kr, _pair_swap(w_kr)], axis=2).astype(BF)
    uq = mla_w_uq.reshape(depth, q_lora, n_heads, LANES + dr)
    wq_ext = jnp.concatenate([uq[..., :LANES], uq[..., LANES:], _pair_swap(uq[..., LANES:])],
                             axis=-1).reshape(depth, q_lora, n_heads * 2 * LANES).astype(BF)
    wkv = jnp.concatenate([mla_w_uk, mla_w_uv], axis=2).astype(BF)
    glu_w = s5_glu_w.astype(BF)
    wb = w_branch.astype(BF)
    wo_ = w_out.astype(BF)
    wup = ffn_w_up.astype(BF)
    wdn = ffn_w_down.astype(BF)

    xp = x_prompt.reshape(bsz * seq, d)
    xs = x_sample.reshape(dbsz * dseq, d)
    outs = {k: [] for k in ("ckv", "krope", "retf", "retb", "s5f", "s5b")}
    nb5 = g5 * ch5 // LANES
    gb5 = LANES // ch5

    for l in range(depth):
        qg = mla_q_norm[l].reshape(1, q_lora)
        kvg = mla_kv_norm[l].reshape(1, kv_lora)
        ng = norm_g[l].reshape(4, 1, d)
        glu_b = s5_glu_b[l].reshape(1, dm)
        conv_w = ffn_conv_w[l]
        conv_b = ffn_conv_b[l].reshape(1, -1)
        dec = jnp.broadcast_to(ret_decay[l].reshape(2, n_heads, 1, 1), (2, n_heads, 1, LANES))
        rng = ret_norm_g[l].reshape(1, dm)
        dskip = s5_d[l].reshape(1, dm)
        toep, wd, wo5, av = _s5_prep(s5_lam_re[l], s5_lam_im[l], s5_log_dt[l], s5_b_re[l], s5_b_im[l],
                                     s5_c_re[l], s5_c_im[l])

        for grp in range(2):
            ctx = grp == 0
            x = xp if ctx else xs
            nbat, slen = (bsz, seq) if ctx else (dbsz, dseq)
            m = nbat * slen
            if ctx:
                mods = mods_all[l, 0:1].reshape(1, 1, -1)
                rpm = m
            else:
                mods = mods_all[l, 1:1 + dbsz].reshape(dbsz, 1, -1)
                rpm = slen

            zg, zs = _inproj(x, mods, ng[0], w_a, l, rpm, 3 * d)

            nseg = slen // S5_SEG
            npb = nbat * nseg
            u = zs[:, :dm].reshape(npb, S5_SEG, dm).transpose(1, 0, 2)
            u4 = u.reshape(S5_SEG // S5_T, S5_T, npb, dm)
            if ctx:
                y4, fin = _s5(u4, toep, wd, wo5, av, dskip, None)
                fin = fin.reshape(nb5, npb, 4, gb5, p5).transpose(1, 2, 0, 3, 4).reshape(npb, 4, g5, p5)
                outs["s5f"].append(jnp.stack([fin[:, 0], fin[:, 1]], axis=-1))
                outs["s5b"].append(jnp.stack([fin[:, 2], fin[:, 3]], axis=-1))
            else:
                s0 = jnp.stack([state_s5_fwd[:, l, :, :, 0], state_s5_fwd[:, l, :, :, 1],
                                state_s5_bwd[:, l, :, :, 0], state_s5_bwd[:, l, :, :, 1]], axis=1)
                s0 = s0.reshape(dbsz, 4, nb5, gb5 * p5).transpose(2, 0, 1, 3)
                y4 = _s5(u4, toep, wd, wo5, av, dskip, s0)
            ya_tm = _glu(y4.reshape(S5_SEG * npb, dm), glu_w, l, glu_b)
            ya = ya_tm.reshape(S5_SEG, npb, dm).transpose(1, 0, 2).reshape(m, dm)

            zs3 = zs.reshape(nbat, slen, -1)
            if ctx:
                yb, rf, rb = _retention(zs3, dm, dec, rng, None, None, n_heads)
                outs["retf"].append(rf)
                outs["retb"].append(rb)
            else:
                yb = _retention(zs3, dm, dec, rng, state_ret_fwd[:, l], state_ret_bwd[:, l], n_heads)
            yb = yb.reshape(m, dm)

            if ctx:
                q, k, v, ckv, kr = _mla_proj(x, mods, ng[0], w_m[l], qg, kvg, wq_ext[l], wkv[l],
                                             ones_tab, zeros_tab, rpm, n_heads, scale)
                outs["ckv"].append(ckv.reshape(nbat, slen, kv_lora))
                outs["krope"].append(kr[:, :dr].reshape(nbat, slen, dr))
                kc = vc = None
            else:
                q, k, v, _, _ = _mla_proj(x, mods, ng[0], w_m[l], qg, kvg, wq_ext[l], wkv[l],
                                          cos_l, sin_l, rpm, n_heads, scale)
                kr_pad = jnp.pad(cache_mla_krope[:, l], ((0, 0), (0, 0), (0, LANES - dr)))
                kc, vc = _cache_kv(cache_mla_ckv[:, l], kr_pad, wkv[l], n_heads)
            yc = _attention(q.reshape(nbat, slen, -1), k.reshape(nbat, slen, -1), v.reshape(nbat, slen, -1),
                            kc, vc, n_heads).reshape(m, dm)

            merged = _merge(ya, yb, yc, zg, wb, l)
            x = _mm_res(merged, wo_, l, x, mods, 2, ng[1], rpm)
            act = _ffn_up(x, mods, ng[2], wup, l, conv_w, conv_b, rpm, slen)
            x = _mm_res(act, wdn, l, x, mods, 5, ng[3], rpm)
            if ctx:
                xp = x
            else:
                xs = x

    return (xp.reshape(bsz, seq, d), xs.reshape(dbsz, dseq, d),
            jnp.stack(outs["ckv"], axis=1), jnp.stack(outs["krope"], axis=1),
            jnp.stack(outs["retf"], axis=1), jnp.stack(outs["retb"], axis=1),
            jnp.stack(outs["s5f"], axis=1), jnp.stack(outs["s5b"], axis=1))
```

```python
import functools

import numpy as np
import jax
import jax.numpy as jnp
from jax import lax
from jax.experimental import pallas as pl
from jax.experimental.pallas import tpu as pltpu

F32 = jnp.float32
BF = jnp.bfloat16

NORM_EPS = 1e-6
ROPE_BASE = 10000.0
GRID_W = 64
LANES = 128
S5_T = 8
S5_SEG = 256
RET_CHUNK_MAX = 512
VMEM_LIMIT = 56 << 20


def _cparams(sem):
    return pltpu.CompilerParams(dimension_semantics=sem, vmem_limit_bytes=VMEM_LIMIT)


def _dot(a, b):
    return jnp.dot(a, b, preferred_element_type=F32)


def _dot_nt(a, b, precision=None):
    return lax.dot_general(a, b, (((1,), (1,)), ((), ())), precision=precision, preferred_element_type=F32)


def _rms(x, g):
    return x * lax.rsqrt(jnp.mean(x * x, axis=-1, keepdims=True) + NORM_EPS) * g


def _rms_mod(x, g, sc, sh):
    return _rms(x, g) * (1.0 + sc) + sh


def _sigmoid(x):
    return 1.0 / (1.0 + jnp.exp(-x))


def _mod_spec(chunk, d, rows_per_mod, tm):
    return pl.BlockSpec((1, 1, d), lambda i, *_: ((i * tm) // rows_per_mod, 0, chunk))


def _ada_kernel(c_ref, w_ref, b_ref, o_ref):
    c = c_ref[...]
    s = (c * _sigmoid(c)).astype(BF)
    o_ref[0] = _dot(s, w_ref[0].astype(BF)) + b_ref[0]


def _ada_mods(cvecs, ada_w, ada_b):
    depth, d, n = ada_w.shape
    tn = 1024
    return pl.pallas_call(
        _ada_kernel,
        grid=(depth, n // tn),
        in_specs=[pl.BlockSpec((8, d), lambda l, j: (0, 0)),
                  pl.BlockSpec((1, d, tn), lambda l, j: (l, 0, j)),
                  pl.BlockSpec((1, 1, tn), lambda l, j: (l, 0, j))],
        out_specs=pl.BlockSpec((1, 8, tn), lambda l, j: (l, 0, j)),
        out_shape=jax.ShapeDtypeStruct((depth, 8, n), F32),
        compiler_params=_cparams(("parallel", "parallel")),
        name="ada_mods",
    )(cvecs, ada_w, ada_b.reshape(depth, 1, n))


def _norm_mod_kernel(x_ref, g_ref, sh_ref, sc_ref, h_ref):
    h_ref[...] = _rms_mod(x_ref[...], g_ref[...], sc_ref[0], sh_ref[0]).astype(BF)


def _norm_mod(x, mods, ng0, rows_per_mod):
    m, d = x.shape
    tm = min(1024, m)
    return pl.pallas_call(
        _norm_mod_kernel,
        grid=(m // tm,),
        in_specs=[pl.BlockSpec((tm, d), lambda i: (i, 0)),
                  pl.BlockSpec((1, d), lambda i: (0, 0)),
                  _mod_spec(0, d, rows_per_mod, tm),
                  _mod_spec(1, d, rows_per_mod, tm)],
        out_specs=pl.BlockSpec((tm, d), lambda i: (i, 0)),
        out_shape=jax.ShapeDtypeStruct((m, d), BF),
        compiler_params=_cparams(("parallel",)),
        name="norm_mod",
    )(x, ng0, mods, mods)


def _inproj_kernel(h_ref, w_ref, og_ref, os_ref, *, n_gate_tiles):
    j = pl.program_id(1)
    r = _dot(h_ref[...], w_ref[...])

    @pl.when(j < n_gate_tiles)
    def _():
        og_ref[...] = r.astype(BF)

    @pl.when(j >= n_gate_tiles)
    def _():
        os_ref[...] = r


def _inproj(h, w_a, layer, n_gate):
    m, d = h.shape
    n = w_a.shape[-1]
    tm, tn = min(1024, m), 1024
    ngt, nst = n_gate // tn, (n - n_gate) // tn
    return pl.pallas_call(
        functools.partial(_inproj_kernel, n_gate_tiles=ngt),
        grid=(m // tm, ngt + nst),
        in_specs=[pl.BlockSpec((tm, d), lambda i, j: (i, 0)),
                  pl.BlockSpec((None, d, tn), lambda i, j: (layer, 0, j))],
        out_specs=[pl.BlockSpec((tm, tn), lambda i, j: (i, jnp.minimum(j, ngt - 1))),
                   pl.BlockSpec((tm, tn), lambda i, j: (i, jnp.maximum(j - ngt, 0)))],
        out_shape=[jax.ShapeDtypeStruct((m, n_gate), BF),
                   jax.ShapeDtypeStruct((m, n - n_gate), F32)],
        compiler_params=_cparams(("parallel", "arbitrary")),
        name="in_proj",
    )(h, w_a)


def _rope(seg, cosp, sinp):
    return seg * cosp + pltpu.roll(seg, LANES // 2, 1) * sinp


def _ones_column(rows):
    lane = lax.broadcasted_iota(jnp.int32, (rows, LANES), 1)
    return jnp.where(lane == 0, 1.0, 0.0).astype(BF)


def _store_kv(kv, krot, k_ref, v_ref, n_heads):
    hv = n_heads * LANES
    ones = _ones_column(kv.shape[0])
    for hh in range(n_heads):
        b0 = hh * 2 * LANES
        k_ref[:, b0:b0 + LANES] = kv[:, hh * LANES:(hh + 1) * LANES].astype(BF)
        k_ref[:, b0 + LANES:b0 + 2 * LANES] = krot
        v_ref[:, b0:b0 + LANES] = kv[:, hv + hh * LANES:hv + (hh + 1) * LANES].astype(BF)
        v_ref[:, b0 + LANES:b0 + 2 * LANES] = ones


def _mla_proj_kernel(h_ref, wm_ref, qg_ref, kvg_ref, wq_ref, wkv_ref, cos_ref, sin_ref,
                     q_ref, k_ref, v_ref, ckv_ref, kr_ref, *, q_lora, kv_lora, n_heads, scale):
    z = _dot(h_ref[...], wm_ref[...])
    cosp, sinp = cos_ref[...], sin_ref[...]

    cqn = _rms(z[:, :q_lora], qg_ref[...]).astype(BF)
    qraw = _dot(cqn, wq_ref[...])
    for hh in range(n_heads):
        b0 = hh * 2 * LANES
        q_ref[:, b0:b0 + LANES] = (qraw[:, b0:b0 + LANES] * scale).astype(BF)
        q_ref[:, b0 + LANES:b0 + 2 * LANES] = (_rope(qraw[:, b0 + LANES:b0 + 2 * LANES], cosp, sinp) * scale).astype(BF)

    ckv = _rms(z[:, q_lora:q_lora + kv_lora], kvg_ref[...])
    ckv_ref[...] = ckv
    kv = _dot(ckv.astype(BF), wkv_ref[...])
    kr = z[:, q_lora + kv_lora:]
    kr_ref[...] = kr
    _store_kv(kv, _rope(kr, cosp, sinp).astype(BF), k_ref, v_ref, n_heads)


def _mla_proj(h, w_m, qg, kvg, wq_ext, wkv, cosp, sinp, n_heads, scale):
    m, d = h.shape
    q_lora, kv_lora = wq_ext.shape[0], wkv.shape[0]
    tm = min(256, m)
    tab_tiles = cosp.shape[0] // tm
    c2 = lambda i: (0, 0)
    return pl.pallas_call(
        functools.partial(_mla_proj_kernel, q_lora=q_lora, kv_lora=kv_lora, n_heads=n_heads, scale=scale),
        grid=(m // tm,),
        in_specs=[pl.BlockSpec((tm, d), lambda i: (i, 0)),
                  pl.BlockSpec(w_m.shape, c2),
                  pl.BlockSpec((1, q_lora), c2),
                  pl.BlockSpec((1, kv_lora), c2),
                  pl.BlockSpec(wq_ext.shape, c2),
                  pl.BlockSpec(wkv.shape, c2),
                  pl.BlockSpec((tm, LANES), lambda i: (i % tab_tiles, 0)),
                  pl.BlockSpec((tm, LANES), lambda i: (i % tab_tiles, 0))],
        out_specs=[pl.BlockSpec((tm, 2 * n_heads * LANES), lambda i: (i, 0)),
                   pl.BlockSpec((tm, 2 * n_heads * LANES), lambda i: (i, 0)),
                   pl.BlockSpec((tm, 2 * n_heads * LANES), lambda i: (i, 0)),
                   pl.BlockSpec((tm, kv_lora), lambda i: (i, 0)),
                   pl.BlockSpec((tm, LANES), lambda i: (i, 0))],
        out_shape=[jax.ShapeDtypeStruct((m, 2 * n_heads * LANES), BF),
                   jax.ShapeDtypeStruct((m, 2 * n_heads * LANES), BF),
                   jax.ShapeDtypeStruct((m, 2 * n_heads * LANES), BF),
                   jax.ShapeDtypeStruct((m, kv_lora), F32),
                   jax.ShapeDtypeStruct((m, LANES), F32)],
        compiler_params=_cparams(("parallel",)),
        name="mla_proj",
    )(h, w_m, qg, kvg, wq_ext, wkv, cosp, sinp)


def _cache_kv_kernel(ckv_ref, kr_ref, wkv_ref, k_ref, v_ref, *, n_heads):
    kv = _dot(ckv_ref[...].astype(BF), wkv_ref[...])
    _store_kv(kv, kr_ref[...].astype(BF), k_ref, v_ref, n_heads)


def _cache_kv(ckv, kr_pad, wkv, n_heads):
    rows, kvl = ckv.shape
    return pl.pallas_call(
        functools.partial(_cache_kv_kernel, n_heads=n_heads),
        grid=(1,),
        in_specs=[pl.BlockSpec((rows, kvl), lambda i: (0, 0)),
                  pl.BlockSpec((rows, LANES), lambda i: (0, 0)),
                  pl.BlockSpec(wkv.shape, lambda i: (0, 0))],
        out_specs=[pl.BlockSpec((rows, 2 * n_heads * LANES), lambda i: (0, 0)),
                   pl.BlockSpec((rows, 2 * n_heads * LANES), lambda i: (0, 0))],
        out_shape=[jax.ShapeDtypeStruct((rows, 2 * n_heads * LANES), BF),
                   jax.ShapeDtypeStruct((rows, 2 * n_heads * LANES), BF)],
        compiler_params=_cparams(("arbitrary",)),
        name="mla_cache_kv",
    )(ckv, kr_pad, wkv)


ATTN_ROWS = 256


def _softmax_pv(q, k, v):
    s = _dot_nt(q, k)
    p = jnp.exp2(s - jnp.max(s, axis=-1, keepdims=True)).astype(BF)
    acc = _dot(p, v)
    return (acc[:, :LANES] / acc[:, LANES:LANES + 1]).astype(BF)


def _attn_kernel(q_ref, k_ref, v_ref, o_ref):
    k, v = k_ref[0], v_ref[0]
    for r0 in range(0, q_ref.shape[1], ATTN_ROWS):
        o_ref[0, r0:r0 + ATTN_ROWS] = _softmax_pv(q_ref[0, r0:r0 + ATTN_ROWS], k, v)


def _attn_heads_kernel(q_ref, k_ref, v_ref, o_ref, *, n_heads):
    for hh in range(n_heads):
        hk = slice(hh * 2 * LANES, (hh + 1) * 2 * LANES)
        o_ref[0, :, hh * LANES:(hh + 1) * LANES] = _softmax_pv(q_ref[0, :, hk], k_ref[0, :, hk], v_ref[0, :, hk])


def _attention(q, k, v, n_heads):
    b, l, _ = q.shape
    s = k.shape[1]
    hw = 2 * LANES
    out_shape = jax.ShapeDtypeStruct((b, l, n_heads * LANES), BF)
    if l * n_heads <= 4096:
        return pl.pallas_call(
            functools.partial(_attn_heads_kernel, n_heads=n_heads),
            grid=(b,),
            in_specs=[pl.BlockSpec((1, l, n_heads * hw), lambda bi: (bi, 0, 0)),
                      pl.BlockSpec((1, s, n_heads * hw), lambda bi: (bi, 0, 0)),
                      pl.BlockSpec((1, s, n_heads * hw), lambda bi: (bi, 0, 0))],
            out_specs=pl.BlockSpec((1, l, n_heads * LANES), lambda bi: (bi, 0, 0)),
            out_shape=out_shape,
            compiler_params=_cparams(("parallel",)),
            name="mla_attention_ctx",
        )(q, k, v)
    tq = min(2048, l)
    return pl.pallas_call(
        _attn_kernel,
        grid=(b, n_heads, l // tq),
        in_specs=[pl.BlockSpec((1, tq, hw), lambda bi, h, i: (bi, i, h)),
                  pl.BlockSpec((1, s, hw), lambda bi, h, i: (bi, 0, h)),
                  pl.BlockSpec((1, s, hw), lambda bi, h, i: (bi, 0, h))],
        out_specs=pl.BlockSpec((1, tq, LANES), lambda bi, h, i: (bi, i, h)),
        out_shape=out_shape,
        compiler_params=_cparams(("parallel", "parallel", "arbitrary")),
        name="mla_attention",
    )(q, k, v)


def _ret_kernel(*refs, with_state, n_chunks, c_len, n_hp, k_scale):
    if with_state:
        (q_ref, k_ref, v_ref, g_ref, dec_ref, ng_ref, s0f_ref, s0b_ref,
         y_ref, dm_ref, kvf_ref, kvb_ref, sf_ref, sb_ref) = refs
    else:
        (q_ref, k_ref, v_ref, g_ref, dec_ref, ng_ref,
         y_ref, ff_ref, fb_ref, dm_ref, kvf_ref, kvb_ref, sf_ref, sb_ref) = refs
    reps = c_len // LANES

    @pl.when(pl.program_id(1) == 0)
    def _():
        di = (lax.broadcasted_iota(jnp.int32, (c_len, c_len), 0)
              - lax.broadcasted_iota(jnp.int32, (c_len, c_len), 1)).astype(F32)
        for hp in range(n_hp):
            lgf = jnp.tile(-jnp.exp(dec_ref[0, hp]), (1, reps))
            lgb = jnp.tile(-jnp.exp(dec_ref[1, hp]), (1, reps))
            dm_ref[hp] = jnp.where(di > 0, jnp.exp(lgf * jnp.maximum(di, 0.0)),
                                   jnp.where(di < 0, jnp.exp(lgb * jnp.maximum(-di, 0.0)), 2.0))

    ii = lax.broadcasted_iota(jnp.int32, (c_len, LANES), 0).astype(F32)
    need_state = with_state or n_chunks > 1
    for hp in range(n_hp):
        cs = slice(hp * LANES, (hp + 1) * LANES)
        lgf = -jnp.exp(dec_ref[0, hp])
        lgb = -jnp.exp(dec_ref[1, hp])
        dk_f = jnp.exp(lgf * (c_len - 1.0 - ii))
        dk_b = jnp.exp(lgb * ii)
        ds_f = jnp.exp(lgf * c_len)
        ds_b = jnp.exp(lgb * c_len)

        for c in range(n_chunks):
            rows = slice(c * c_len, (c + 1) * c_len)
            kc = k_ref[0, rows, cs] * k_scale
            vc = v_ref[0, rows, cs].astype(BF)
            kvf_ref[c] = _dot((kc * dk_f).T.astype(BF), vc)
            kvb_ref[c] = _dot((kc * dk_b).T.astype(BF), vc)

        if with_state:
            s_f, s_b = s0f_ref[0, hp], s0b_ref[0, hp]
        else:
            s_f = s_b = jnp.zeros((LANES, LANES), F32)
        for c in range(n_chunks):
            sf_ref[c] = s_f
            s_f = ds_f * s_f + kvf_ref[c]
        for c in reversed(range(n_chunks)):
            sb_ref[c] = s_b
            s_b = ds_b * s_b + kvb_ref[c]
        if not with_state:
            ff_ref[0, hp] = s_f
            fb_ref[0, hp] = s_b

        ng = ng_ref[:, cs]
        dq_f = jnp.exp(lgf * (ii + 1.0))
        dq_b = jnp.exp(lgb * (c_len - ii))
        for c in range(n_chunks):
            rows = slice(c * c_len, (c + 1) * c_len)
            qc = q_ref[0, rows, cs]
            kc = (k_ref[0, rows, cs] * k_scale).astype(BF)
            vc = v_ref[0, rows, cs].astype(BF)
            att = _dot_nt(qc.astype(BF), kc) * dm_ref[hp]
            o = _dot(att.astype(BF), vc)
            if need_state:
                o = (o + _dot((qc * dq_f).astype(BF), sf_ref[c].astype(BF))
                     + _dot((qc * dq_b).astype(BF), sb_ref[c].astype(BF)))
            mu = jnp.mean(o, axis=-1, keepdims=True)
            oc = o - mu
            var = jnp.mean(oc * oc, axis=-1, keepdims=True)
            on = oc * lax.rsqrt(var + NORM_EPS)
            gg = g_ref[0, rows, cs]
            y_ref[0, rows, cs] = (gg * _sigmoid(gg) * (on * ng)).astype(BF)


def _retention(zs, col0, dec, ng, s0f, s0b, n_heads):
    b, l, _ = zs.shape
    c_len = min(RET_CHUNK_MAX, l)
    nc = l // c_len
    with_state = s0f is not None
    n_hp = 1 if nc > 1 else n_heads
    hw = n_hp * LANES
    cb = col0 // hw

    def zspec(seg):
        return pl.BlockSpec((1, l, hw), lambda hg, bi: (bi, 0, cb + seg * (n_heads // n_hp) + hg))

    st_spec = pl.BlockSpec((1, n_hp, LANES, LANES), lambda hg, bi: (bi, hg, 0, 0))
    in_specs = [zspec(0), zspec(1), zspec(2), zspec(3),
                pl.BlockSpec((2, n_hp, 1, LANES), lambda hg, bi: (0, hg, 0, 0)),
                pl.BlockSpec((1, hw), lambda hg, bi: (0, hg))]
    args = [zs, zs, zs, zs, dec, ng]
    y_spec = pl.BlockSpec((1, l, hw), lambda hg, bi: (bi, 0, hg))
    y_shape = jax.ShapeDtypeStruct((b, l, n_heads * LANES), BF)
    if with_state:
        in_specs += [st_spec, st_spec]
        args += [s0f, s0b]
        out_specs, out_shape = y_spec, y_shape
    else:
        st_shape = jax.ShapeDtypeStruct((b, n_heads, LANES, LANES), F32)
        out_specs, out_shape = [y_spec, st_spec, st_spec], [y_shape, st_shape, st_shape]
    scratch = ([pltpu.VMEM((n_hp, c_len, c_len), F32)]
               + [pltpu.VMEM((nc, LANES, LANES), F32) for _ in range(4)])
    return pl.pallas_call(
        functools.partial(_ret_kernel, with_state=with_state, n_chunks=nc, c_len=c_len, n_hp=n_hp,
                          k_scale=float(LANES) ** -0.5),
        grid=(n_heads // n_hp, b),
        in_specs=in_specs, out_specs=out_specs, out_shape=out_shape,
        scratch_shapes=scratch,
        compiler_params=_cparams(("arbitrary", "arbitrary")),
        name="retention",
    )(*args)


def _s5_prep_kernel(lre_ref, lim_ref, ldt_ref, btre_ref, btim_ref, cre_ref, cim_ref,
                    toep_ref, wd_ref, wo_ref, av_ref, *, ch, p, seg_chunks):
    t = S5_T
    gw = (LANES // ch) * p
    btre, btim, cre, cim = btre_ref[...], btim_ref[...], cre_ref[...], cim_ref[...]
    r1 = lax.broadcasted_iota(jnp.int32, (LANES, LANES), 0)
    c1 = lax.broadcasted_iota(jnp.int32, (LANES, LANES), 1)
    same_group = (r1 // ch) == (c1 // ch)
    lane_valid = c1 < p
    r5 = lax.broadcasted_iota(jnp.int32, (LANES, gw), 0)
    c5 = lax.broadcasted_iota(jnp.int32, (LANES, gw), 1)
    own_state = (r5 // ch) == (c5 // p)

    def spread(x):
        x2 = x + pltpu.roll(x, LANES // 2, 1)
        return jnp.where(own_state, jnp.concatenate([x2] * (gw // LANES), axis=1), 0.0)

    xs, ys, lags = [], [], []
    for d in range(2):
        lr, li, dt = lre_ref[d], lim_ref[d], jnp.exp(ldt_ref[d])
        ar, ai = lr * dt, li * dt
        mag = jnp.exp(ar)
        abr, abi = mag * jnp.cos(ai), mag * jnp.sin(ai)
        nr, ni = abr - 1.0, abi
        den = lr * lr + li * li
        cr = (nr * lr + ni * li) / den
        ci = (ni * lr - nr * li) / den
        pr, pi = jnp.ones_like(lr), jnp.zeros_like(lr)
        xd, yd, gd = [], [], []
        for kk in range(t + 1):
            if kk < t:
                wr, wi = cr * pr - ci * pi, cr * pi + ci * pr
                xr, xi = wr * btre - wi * btim, wr * btim + wi * btre
                xd.append((xr, xi))
                hi = lax.Precision.HIGHEST
                gd.append(jnp.where(same_group, _dot_nt(xr, cre, hi) - _dot_nt(xi, cim, hi), 0.0))
            if kk >= 1:
                yd.append((cre * pr - cim * pi, -(cre * pi + cim * pr)))
            if kk == t:
                qr, qi = jnp.where(lane_valid, pr, 0.0), jnp.where(lane_valid, pi, 0.0)
            pr, pi = pr * abr - pi * abi, pr * abi + pi * abr
        xs.append(xd)
        ys.append(yd)
        lags.append(gd)
        inv = 1.0 / ch
        av_ref[0, 2 * d:2 * d + 1] = jnp.sum(spread(qr), axis=0, keepdims=True) * inv
        av_ref[0, 2 * d + 1:2 * d + 2] = jnp.sum(spread(qi), axis=0, keepdims=True) * inv
        for _ in range(seg_chunks.bit_length() - 1):
            qr, qi = qr * qr - qi * qi, 2.0 * qr * qi
        av_ref[0, 4 + 2 * d:5 + 2 * d] = jnp.sum(spread(qr), axis=0, keepdims=True) * inv
        av_ref[0, 5 + 2 * d:6 + 2 * d] = jnp.sum(spread(qi), axis=0, keepdims=True) * inv

    for ti in range(t):
        for to in range(t):
            lag = to - ti
            tile = lags[0][lag] if lag > 0 else (lags[1][-lag] if lag < 0 else lags[0][0] + lags[1][0])
            toep_ref[0, ti * LANES:(ti + 1) * LANES, to * LANES:(to + 1) * LANES] = tile.astype(BF)
    for tt in range(t):
        rows = slice(tt * LANES, (tt + 1) * LANES)
        xf, xb = xs[0][t - 1 - tt], xs[1][tt]
        yf, yb = ys[0][tt], ys[1][t - 1 - tt]
        for m, (xv, yv) in enumerate(((xf[0], yf[0]), (xf[1], yf[1]), (xb[0], yb[0]), (xb[1], yb[1]))):
            wd_ref[0, rows, m * gw:(m + 1) * gw] = spread(xv).astype(BF)
            wo_ref[0, m * gw:(m + 1) * gw, rows] = spread(yv).T.astype(BF)


def _s5_prep(lam_re, lam_im, log_dt, b_re, b_im, c_re, c_im):
    _, g, p = lam_re.shape
    ch = b_re.shape[-1]
    assert p == LANES // 2 and LANES % ch == 0
    nb = g * ch // LANES
    gw = (LANES // ch) * p
    t = S5_T
    seg_chunks = S5_SEG // t
    assert seg_chunks & (seg_chunks - 1) == 0

    def prow(a, padval):
        return jnp.pad(jnp.repeat(a, ch, axis=1), ((0, 0), (0, 0), (0, LANES - p)), constant_values=padval)

    def wrow(a):
        return jnp.pad(a.reshape(g * ch, p), ((0, 0), (0, LANES - p)))

    lre = prow(lam_re, -0.5)
    lim = prow(lam_im, 0.0)
    ldt = prow(jnp.broadcast_to(log_dt[..., None], lam_re.shape), 0.0)
    lspec = pl.BlockSpec((2, LANES, LANES), lambda i: (0, i, 0))
    wspec = pl.BlockSpec((LANES, LANES), lambda i: (i, 0))
    return pl.pallas_call(
        functools.partial(_s5_prep_kernel, ch=ch, p=p, seg_chunks=seg_chunks),
        grid=(nb,),
        in_specs=[lspec, lspec, lspec, wspec, wspec, wspec, wspec],
        out_specs=[pl.BlockSpec((1, t * LANES, t * LANES), lambda i: (i, 0, 0)),
                   pl.BlockSpec((1, t * LANES, 4 * gw), lambda i: (i, 0, 0)),
                   pl.BlockSpec((1, 4 * gw, t * LANES), lambda i: (i, 0, 0)),
                   pl.BlockSpec((1, 8, gw), lambda i: (i, 0, 0))],
        out_shape=[jax.ShapeDtypeStruct((nb, t * LANES, t * LANES), BF),
                   jax.ShapeDtypeStruct((nb, t * LANES, 4 * gw), BF),
                   jax.ShapeDtypeStruct((nb, 4 * gw, t * LANES), BF),
                   jax.ShapeDtypeStruct((nb, 8, gw), F32)],
        compiler_params=_cparams(("parallel",)),
        name="s5_prep",
    )(lre, lim, ldt, wrow(jnp.swapaxes(b_re, 1, 2)), wrow(jnp.swapaxes(b_im, 1, 2)), wrow(c_re), wrow(c_im))


def _gelu(x):
    return 0.5 * x * (1.0 + jnp.tanh(0.7978845608028654 * (x + 0.044715 * (x * x * x))))


def _s5_kernel(*refs, chain):
    if chain:
        (u_ref, toep_ref, wd_ref, wo_ref, a_ref, dsk_ref, s0_ref,
         y_ref, lhs_ref, d_ref, s_ref, xs_ref) = refs
    else:
        (u_ref, toep_ref, wd_ref, wo_ref, a_ref, dsk_ref,
         y_ref, fin_ref, lhs_ref, d_ref, s_ref) = refs
    nc, t, pb, _ = u_ref.shape
    rows = nc * pb
    w = a_ref.shape[-1]
    for tt in range(t):
        lhs_ref[:, tt * LANES:(tt + 1) * LANES] = u_ref[:, tt].reshape(rows, LANES).astype(BF)
    d_ref[...] = _dot(lhs_ref[...], wd_ref[0])
    av = a_ref[0]
    afr, afi, abr, abi = av[0:1], av[1:2], av[2:3], av[3:4]

    def scan(init, store):
        sfr, sfi, sbr, sbi = init
        for c in range(nc):
            rf = slice(c * pb, (c + 1) * pb)
            rb = slice((nc - 1 - c) * pb, (nc - c) * pb)
            if store:
                s_ref[rf, 0:w] = sfr.astype(BF)
                s_ref[rf, w:2 * w] = sfi.astype(BF)
                s_ref[rb, 2 * w:3 * w] = sbr.astype(BF)
                s_ref[rb, 3 * w:4 * w] = sbi.astype(BF)
            sfr, sfi = (afr * sfr - afi * sfi + d_ref[rf, 0:w],
                        afr * sfi + afi * sfr + d_ref[rf, w:2 * w])
            sbr, sbi = (abr * sbr - abi * sbi + d_ref[rb, 2 * w:3 * w],
                        abr * sbi + abi * sbr + d_ref[rb, 3 * w:4 * w])
        return sfr, sfi, sbr, sbi

    zero = jnp.zeros((pb, w), F32)
    if not chain:
        fin = scan((zero, zero, zero, zero), True)
        for m in range(4):
            fin_ref[0, :, m * w:(m + 1) * w] = fin[m]
    else:
        efr, efi, ebr, ebi = scan((zero, zero, zero, zero), False)
        s0 = s0_ref[0, 0]
        xr, xi = s0[0:1], s0[1:2]
        for j in range(pb):
            xs_ref[j:j + 1, 0:w] = xr
            xs_ref[j:j + 1, w:2 * w] = xi
            xr, xi = (av[4:5] * xr - av[5:6] * xi + efr[j:j + 1],
                      av[4:5] * xi + av[5:6] * xr + efi[j:j + 1])
        xr, xi = s0[2:3], s0[3:4]
        for j in reversed(range(pb)):
            xs_ref[j:j + 1, 2 * w:3 * w] = xr
            xs_ref[j:j + 1, 3 * w:4 * w] = xi
            xr, xi = (av[6:7] * xr - av[7:8] * xi + ebr[j:j + 1],
                      av[6:7] * xi + av[7:8] * xr + ebi[j:j + 1])
        scan((xs_ref[:, 0:w], xs_ref[:, w:2 * w], xs_ref[:, 2 * w:3 * w], xs_ref[:, 3 * w:4 * w]), True)

    yv = _dot(lhs_ref[...], toep_ref[0]) + _dot(s_ref[...], wo_ref[0])
    dsk = dsk_ref[...]
    for tt in range(t):
        yt = yv[:, tt * LANES:(tt + 1) * LANES] + dsk * u_ref[:, tt].reshape(rows, LANES)
        y_ref[:, tt] = _gelu(yt).reshape(nc, pb, LANES)


def _s5(u4, toep, wd, wo, av, dskip, s0):
    nc, t, npb, nch = u4.shape
    nb = toep.shape[0]
    chain = s0 is not None
    pb = npb // s0.shape[1] if chain else min(16, npb)
    sw = wd.shape[-1]
    rows = nc * pb
    ublk = pl.BlockSpec((nc, t, pb, LANES), lambda kb, hh: (0, 0, hh, kb))

    def wspec(arr):
        return pl.BlockSpec((1,) + arr.shape[1:], lambda kb, hh: (kb, 0, 0))

    in_specs = [ublk, wspec(toep), wspec(wd), wspec(wo), wspec(av),
                pl.BlockSpec((1, LANES), lambda kb, hh: (0, kb))]
    args = [u4, toep, wd, wo, av, dskip]
    scratch = [pltpu.VMEM((rows, t * LANES), BF), pltpu.VMEM((rows, sw), F32), pltpu.VMEM((rows, sw), BF)]
    if chain:
        in_specs += [pl.BlockSpec((1, 1, 4, sw // 4), lambda kb, hh: (kb, hh, 0, 0))]
        args += [s0]
        out_specs = ublk
        out_shape = jax.ShapeDtypeStruct(u4.shape, F32)
        scratch.append(pltpu.VMEM((pb, sw), F32))
    else:
        out_specs = [ublk, pl.BlockSpec((1, pb, sw), lambda kb, hh: (kb, hh, 0))]
        out_shape = [jax.ShapeDtypeStruct(u4.shape, F32), jax.ShapeDtypeStruct((nb, npb, sw), F32)]
    return pl.pallas_call(
        functools.partial(_s5_kernel, chain=chain),
        grid=(nb, npb // pb),
        in_specs=in_specs, out_specs=out_specs, out_shape=out_shape,
        scratch_shapes=scratch,
        compiler_params=_cparams(("parallel", "parallel")),
        name="s5_chunked",
    )(*args)


def _glu_kernel(y_ref, w_ref, b_ref, o_ref):
    y = y_ref[...]
    o_ref[...] = (y * _sigmoid(_dot(y.astype(BF), w_ref[...]) + b_ref[...])).astype(BF)


def _glu(y, w, layer, b):
    m, n = y.shape
    tm = min(1024, m)
    return pl.pallas_call(
        _glu_kernel,
        grid=(m // tm,),
        in_specs=[pl.BlockSpec((tm, n), lambda i: (i, 0)),
                  pl.BlockSpec((None, n, n), lambda i: (layer, 0, 0)),
                  pl.BlockSpec((1, n), lambda i: (0, 0))],
        out_specs=pl.BlockSpec((tm, n), lambda i: (i, 0)),
        out_shape=jax.ShapeDtypeStruct((m, n), BF),
        compiler_params=_cparams(("parallel",)),
        name="s5_glu",
    )(y, w, b)


def _merge_kernel(ya_ref, yb_ref, yc_ref, ga_ref, gb_ref, gc_ref, wb_ref, o_ref):
    acc = (_sigmoid(ga_ref[...].astype(F32)) * _dot(ya_ref[...], wb_ref[0])
           + _sigmoid(gb_ref[...].astype(F32)) * _dot(yb_ref[...], wb_ref[1])
           + _sigmoid(gc_ref[...].astype(F32)) * _dot(yc_ref[...], wb_ref[2]))
    o_ref[...] = acc.astype(BF)


def _merge(ya, yb, yc, zg, wb, layer):
    m, dm = ya.shape
    d = wb.shape[-1]
    tm, tn = min(1024, m), 512
    nt = d // tn
    ysp = pl.BlockSpec((tm, dm), lambda i, j: (i, 0))

    def gsp(br):
        return pl.BlockSpec((tm, tn), lambda i, j: (i, j + br * nt))

    return pl.pallas_call(
        _merge_kernel,
        grid=(m // tm, nt),
        in_specs=[ysp, ysp, ysp, gsp(0), gsp(1), gsp(2),
                  pl.BlockSpec((None, 3, dm, tn), lambda i, j: (layer, 0, 0, j))],
        out_specs=pl.BlockSpec((tm, tn), lambda i, j: (i, j)),
        out_shape=jax.ShapeDtypeStruct((m, d), BF),
        compiler_params=_cparams(("parallel", "parallel")),
        name="branch_merge",
    )(ya, yb, yc, zg, zg, zg, wb)


def _mm_res_kernel(*refs, with_next):
    if with_next:
        a_ref, w_ref, x_ref, gate_ref, ng_ref, ng2_ref, sh2_ref, sc2_ref, o_ref, h_ref, acc_ref = refs
    else:
        a_ref, w_ref, x_ref, gate_ref, ng_ref, o_ref, acc_ref = refs
    k = pl.program_id(1)

    @pl.when(k == 0)
    def _():
        acc_ref[...] = jnp.zeros_like(acc_ref)

    acc_ref[...] += _dot(a_ref[...], w_ref[...])

    @pl.when(k == pl.num_programs(1) - 1)
    def _():
        gn = gate_ref[0] * ng_ref[...]
        if with_next:
            g2 = ng2_ref[...] * (1.0 + sc2_ref[0])
            sh2 = sh2_ref[0]
        rows = 128
        for r0 in range(0, o_ref.shape[0], rows):
            acc = acc_ref[r0:r0 + rows]
            inv = lax.rsqrt(jnp.mean(acc * acc, axis=-1, keepdims=True) + NORM_EPS)
            xn = x_ref[r0:r0 + rows] + (acc * inv) * gn
            o_ref[r0:r0 + rows] = xn
            if with_next:
                inv2 = lax.rsqrt(jnp.mean(xn * xn, axis=-1, keepdims=True) + NORM_EPS)
                h_ref[r0:r0 + rows] = ((xn * inv2) * g2 + sh2).astype(BF)


def _mm_res(a, w, layer, x, mods, gate_chunk, ng, rows_per_mod, nxt=None):
    m, kdim = a.shape
    d = w.shape[-1]
    tm = min(512, m)
    tk = kdim // 4 if kdim > 2048 else kdim // 2
    assert tk % LANES == 0
    in_specs = [pl.BlockSpec((tm, tk), lambda i, k: (i, k)),
                pl.BlockSpec((None, tk, d), lambda i, k: (layer, k, 0)),
                pl.BlockSpec((tm, d), lambda i, k: (i, 0)),
                _mod_spec(gate_chunk, d, rows_per_mod, tm),
                pl.BlockSpec((1, d), lambda i, k: (0, 0))]
    args = [a, w, x, mods, ng]
    x_spec = pl.BlockSpec((tm, d), lambda i, k: (i, 0))
    x_shape = jax.ShapeDtypeStruct((m, d), F32)
    if nxt is None:
        out_specs, out_shape = x_spec, x_shape
    else:
        ng2, mods2, sh_chunk, sc_chunk = nxt
        in_specs += [pl.BlockSpec((1, d), lambda i, k: (0, 0)),
                     _mod_spec(sh_chunk, d, rows_per_mod, tm),
                     _mod_spec(sc_chunk, d, rows_per_mod, tm)]
        args += [ng2, mods2, mods2]
        out_specs = [x_spec, pl.BlockSpec((tm, d), lambda i, k: (i, 0))]
        out_shape = [x_shape, jax.ShapeDtypeStruct((m, d), BF)]
    return pl.pallas_call(
        functools.partial(_mm_res_kernel, with_next=nxt is not None),
        grid=(m // tm, kdim // tk),
        in_specs=in_specs, out_specs=out_specs, out_shape=out_shape,
        scratch_shapes=[pltpu.VMEM((tm, d), F32)],
        compiler_params=_cparams(("parallel", "arbitrary")),
        name="matmul_norm_residual",
    )(*args)


FFN_HALO = 16


def _ffn_up_kernel(hp_ref, h_ref, hn_ref, wv_ref, wg_ref, cwv_ref, cwg_ref, cbv_ref, cbg_ref,
                   o_ref, hb_ref, uv_ref, ug_ref, *, tm, piece, seq_len):
    i, j = pl.program_id(0), pl.program_id(1)
    hl = FFN_HALO
    n_pieces = tm // piece
    stride = piece + hl

    @pl.when(j == 0)
    def _():
        zero = jnp.zeros(hp_ref.shape, BF)
        if n_pieces == 1 and seq_len > tm:
            hb_ref[0:hl] = jnp.where(lax.rem(i * tm, seq_len) != 0, hp_ref[...], zero)
            hb_ref[stride:stride + hl] = jnp.where(lax.rem((i + 1) * tm, seq_len) != 0, hn_ref[...], zero)
        else:
            for pc in range(n_pieces + 1):
                hb_ref[pc * stride:pc * stride + hl] = zero
        for pc in range(n_pieces):
            hb_ref[hl + pc * stride:hl + pc * stride + piece] = h_ref[pc * piece:(pc + 1) * piece]

    hb = hb_ref[...]
    uv_ref[...] = _dot(hb, wv_ref[...])
    ug_ref[...] = _dot(hb, wg_ref[...])
    cwv, cwg, cbv, cbg = cwv_ref[...], cwg_ref[...], cbv_ref[...], cbg_ref[...]
    for pc in range(n_pieces):
        b0 = hl + pc * stride

        def conv(u_ref, cw, cb):
            return (u_ref[b0 - 1:b0 - 1 + piece] * cw[0:1] + u_ref[b0:b0 + piece] * cw[1:2]
                    + u_ref[b0 + 1:b0 + 1 + piece] * cw[2:3] + cb)

        val = conv(uv_ref, cwv, cbv)
        gate = conv(ug_ref, cwg, cbg)
        o_ref[pc * piece:(pc + 1) * piece] = (gate * _sigmoid(gate) * val).astype(BF)


def _ffn_up(h, w_up, layer, conv_w, conv_b, seq_len):
    m, d = h.shape
    f = w_up.shape[-1] // 2
    tm, tf = min(1024, m), 512
    piece = min(seq_len, tm)
    assert tm % piece == 0 and seq_len % piece == 0
    nf = f // tf
    hl = FFN_HALO
    nblk = m // hl
    rows = (tm // piece) * (piece + hl) + hl
    return pl.pallas_call(
        functools.partial(_ffn_up_kernel, tm=tm, piece=piece, seq_len=seq_len),
        grid=(m // tm, nf),
        in_specs=[pl.BlockSpec((hl, d), lambda i, j: (jnp.maximum(i * (tm // hl) - 1, 0), 0)),
                  pl.BlockSpec((tm, d), lambda i, j: (i, 0)),
                  pl.BlockSpec((hl, d), lambda i, j: (jnp.minimum((i + 1) * (tm // hl), nblk - 1), 0)),
                  pl.BlockSpec((None, d, tf), lambda i, j: (layer, 0, j)),
                  pl.BlockSpec((None, d, tf), lambda i, j: (layer, 0, j + nf)),
                  pl.BlockSpec((3, tf), lambda i, j: (0, j)),
                  pl.BlockSpec((3, tf), lambda i, j: (0, j + nf)),
                  pl.BlockSpec((1, tf), lambda i, j: (0, j)),
                  pl.BlockSpec((1, tf), lambda i, j: (0, j + nf))],
        out_specs=pl.BlockSpec((tm, tf), lambda i, j: (i, j)),
        out_shape=jax.ShapeDtypeStruct((m, f), BF),
        scratch_shapes=[pltpu.VMEM((rows, d), BF),
                        pltpu.VMEM((rows, tf), F32),
                        pltpu.VMEM((rows, tf), F32)],
        compiler_params=_cparams(("parallel", "arbitrary")),
        name="ffn_up_conv",
    )(h, h, h, w_up, w_up, conv_w, conv_w, conv_b, conv_b)


def _pair_swap(w):
    r = w.reshape(w.shape[:-1] + (w.shape[-1] // 2, 2))
    return jnp.stack([-r[..., 1], r[..., 0]], axis=-1).reshape(w.shape)


def _rope_tables(n_tokens, dr):
    rows = n_tokens // GRID_W
    row = jnp.broadcast_to(jnp.arange(rows, dtype=F32)[:, None], (rows, GRID_W)).reshape(-1)
    col = jnp.broadcast_to(jnp.arange(GRID_W, dtype=F32)[None, :], (rows, GRID_W)).reshape(-1)
    nf = dr // 4
    inv = ROPE_BASE ** (-jnp.arange(nf, dtype=F32) / nf)
    ang = jnp.concatenate([row[:, None] * inv, col[:, None] * inv], axis=-1)
    ang = jnp.repeat(ang, 2, axis=-1)
    pad = jnp.zeros((n_tokens, LANES - dr), F32)
    return jnp.concatenate([jnp.cos(ang), pad], axis=-1), jnp.concatenate([jnp.sin(ang), pad], axis=-1)


def kernel(x_prompt, x_sample, cache_mla_ckv, cache_mla_krope, state_ret_fwd, state_ret_bwd, state_s5_fwd, state_s5_bwd, c, c_ctx, ada_w, ada_b, norm_g, w_in, s5_lam_re, s5_lam_im, s5_log_dt, s5_b_re, s5_b_im, s5_c_re, s5_c_im, s5_d, s5_glu_w, s5_glu_b, ret_decay, ret_norm_g, mla_q_norm, mla_kv_norm, mla_w_uq, mla_w_uk, mla_w_uv, w_branch, w_out, ffn_w_up, ffn_conv_w, ffn_conv_b, ffn_w_down):
    bsz, seq, d = x_prompt.shape
    dbsz, dseq, _ = x_sample.shape
    depth = ada_w.shape[0]
    dm = d // 2
    n_heads = ret_decay.shape[-1]
    dr = cache_mla_krope.shape[-1]
    q_lora, kv_lora = mla_w_uq.shape[1], mla_w_uk.shape[1]
    g5, p5, ch5 = s5_b_re.shape[1:]
    assert dm == n_heads * LANES and dr == LANES // 2 and mla_w_uk.shape[-1] == dm
    assert seq == S5_SEG and dseq % S5_SEG == 0
    qscale = float(LANES + dr) ** -0.5 * 1.4426950408889634

    cvecs = jnp.concatenate([c_ctx[None], c, jnp.zeros((8 - 1 - dbsz, d), F32)], axis=0)
    mods_all = _ada_mods(cvecs, ada_w, ada_b)

    cos_l, sin_l = _rope_tables(dseq, dr)
    ones_tab = jnp.concatenate([jnp.ones((256, dr), F32), jnp.zeros((256, LANES - dr), F32)], axis=-1)
    zeros_tab = jnp.zeros((256, LANES), F32)

    cw = np.cumsum([0, dm, dm, dm, dm, dm, q_lora, kv_lora, dr, d, d, d])
    w_a = jnp.concatenate([w_in[:, :, cw[8]:cw[11]], w_in[:, :, cw[0]:cw[5]]], axis=2).astype(BF)
    w_kr = w_in[:, :, cw[7]:cw[8]]
    w_m = jnp.concatenate([w_in[:, :, cw[5]:cw[7]], w_kr, _pair_swap(w_kr)], axis=2).astype(BF)
    uq = mla_w_uq.reshape(depth, q_lora, n_heads, LANES + dr)
    wq_ext = jnp.concatenate([uq[..., :LANES], uq[..., LANES:], _pair_swap(uq[..., LANES:])],
                             axis=-1).reshape(depth, q_lora, n_heads * 2 * LANES).astype(BF)
    wkv = jnp.concatenate([mla_w_uk, mla_w_uv], axis=2).astype(BF)
    glu_w = s5_glu_w.astype(BF)
    wb = w_branch.astype(BF)
    wo_ = w_out.astype(BF)
    wup = ffn_w_up.astype(BF)
    wdn = ffn_w_down.astype(BF)

    outs = {k: [] for k in ("ckv", "krope", "retf", "retb", "s5f", "s5b")}
    nb5 = g5 * ch5 // LANES
    gb5 = LANES // ch5

    def group_mods(l, ctx):
        if ctx:
            return mods_all[l, 0:1].reshape(1, 1, -1), bsz * seq
        return mods_all[l, 1:1 + dbsz].reshape(dbsz, 1, -1), dseq

    xres = [x_prompt.reshape(bsz * seq, d), x_sample.reshape(dbsz * dseq, d)]
    hres = []
    for grp in range(2):
        mods, rpm = group_mods(0, grp == 0)
        hres.append(_norm_mod(xres[grp], mods, norm_g[0, 0].reshape(1, d), rpm))

    for l in range(depth):
        qg = mla_q_norm[l].reshape(1, q_lora)
        kvg = mla_kv_norm[l].reshape(1, kv_lora)
        ng = norm_g[l].reshape(4, 1, d)
        glu_b = s5_glu_b[l].reshape(1, dm)
        conv_w = ffn_conv_w[l]
        conv_b = ffn_conv_b[l].reshape(1, -1)
        dec = jnp.broadcast_to(ret_decay[l].reshape(2, n_heads, 1, 1), (2, n_heads, 1, LANES))
        rng = ret_norm_g[l].reshape(1, dm)
        dskip = s5_d[l].reshape(1, dm)
        toep, wd, wo5, av = _s5_prep(s5_lam_re[l], s5_lam_im[l], s5_log_dt[l], s5_b_re[l], s5_b_im[l],
                                     s5_c_re[l], s5_c_im[l])

        for grp in range(2):
            ctx = grp == 0
            x, h = xres[grp], hres[grp]
            nbat, slen = (bsz, seq) if ctx else (dbsz, dseq)
            m = nbat * slen
            mods, rpm = group_mods(l, ctx)

            zg, zs = _inproj(h, w_a, l, 3 * d)

            nseg = slen // S5_SEG
            npb = nbat * nseg
            u = zs[:, :dm].reshape(npb, S5_SEG, dm).transpose(1, 0, 2)
            u4 = u.reshape(S5_SEG // S5_T, S5_T, npb, dm)
            if ctx:
                y4, fin = _s5(u4, toep, wd, wo5, av, dskip, None)
                fin = fin.reshape(nb5, npb, 4, gb5, p5).transpose(1, 2, 0, 3, 4).reshape(npb, 4, g5, p5)
                outs["s5f"].append(jnp.stack([fin[:, 0], fin[:, 1]], axis=-1))
                outs["s5b"].append(jnp.stack([fin[:, 2], fin[:, 3]], axis=-1))
            else:
                s0 = jnp.stack([state_s5_fwd[:, l, :, :, 0], state_s5_fwd[:, l, :, :, 1],
                                state_s5_bwd[:, l, :, :, 0], state_s5_bwd[:, l, :, :, 1]], axis=1)
                s0 = s0.reshape(dbsz, 4, nb5, gb5 * p5).transpose(2, 0, 1, 3)
                y4 = _s5(u4, toep, wd, wo5, av, dskip, s0)
            ya_tm = _glu(y4.reshape(S5_SEG * npb, dm), glu_w, l, glu_b)
            ya = ya_tm.reshape(S5_SEG, npb, dm).transpose(1, 0, 2).reshape(m, dm)

            zs3 = zs.reshape(nbat, slen, -1)
            if ctx:
                yb, rf, rb = _retention(zs3, dm, dec, rng, None, None, n_heads)
                outs["retf"].append(rf)
                outs["retb"].append(rb)
            else:
                yb = _retention(zs3, dm, dec, rng, state_ret_fwd[:, l], state_ret_bwd[:, l], n_heads)
            yb = yb.reshape(m, dm)

            if ctx:
                q, k, v, ckv, kr = _mla_proj(h, w_m[l], qg, kvg, wq_ext[l], wkv[l], ones_tab, zeros_tab,
                                             n_heads, qscale)
                outs["ckv"].append(ckv.reshape(nbat, slen, kv_lora))
                outs["krope"].append(kr[:, :dr].reshape(nbat, slen, dr))
                k, v = k.reshape(nbat, slen, -1), v.reshape(nbat, slen, -1)
            else:
                q, k, v, _, _ = _mla_proj(h, w_m[l], qg, kvg, wq_ext[l], wkv[l], cos_l, sin_l, n_heads, qscale)
                past = cache_mla_ckv.shape[2]
                kr_pad = jnp.pad(cache_mla_krope[:, l], ((0, 0), (0, 0), (0, LANES - dr)))
                kc, vc = _cache_kv(cache_mla_ckv[:, l].reshape(nbat * past, kv_lora),
                                   kr_pad.reshape(nbat * past, LANES), wkv[l], n_heads)
                k = jnp.concatenate([k.reshape(nbat, slen, -1), kc.reshape(nbat, past, -1)], axis=1)
                v = jnp.concatenate([v.reshape(nbat, slen, -1), vc.reshape(nbat, past, -1)], axis=1)
            yc = _attention(q.reshape(nbat, slen, -1), k, v, n_heads).reshape(m, dm)

            merged = _merge(ya, yb, yc, zg, wb, l)
            x, h = _mm_res(merged, wo_, l, x, mods, 2, ng[1], rpm, nxt=(ng[2], mods, 3, 4))
            act = _ffn_up(h, wup, l, conv_w, conv_b, slen)
            if l + 1 < depth:
                x, h = _mm_res(act, wdn, l, x, mods, 5, ng[3], rpm,
                               nxt=(norm_g[l + 1, 0].reshape(1, d), group_mods(l + 1, ctx)[0], 0, 1))
            else:
                x, h = _mm_res(act, wdn, l, x, mods, 5, ng[3], rpm), None
            xres[grp], hres[grp] = x, h

    return (xres[0].reshape(bsz, seq, d), xres[1].reshape(dbsz, dseq, d),
            jnp.stack(outs["ckv"], axis=1), jnp.stack(outs["krope"], axis=1),
            jnp.stack(outs["retf"], axis=1), jnp.stack(outs["retb"], axis=1),
            jnp.stack(outs["s5f"], axis=1), jnp.stack(outs["s5b"], axis=1))
```

```python
import functools

import numpy as np
import jax
import jax.numpy as jnp
from jax import lax
from jax.experimental import pallas as pl
from jax.experimental.pallas import tpu as pltpu

F32 = jnp.float32
BF = jnp.bfloat16

NORM_EPS = 1e-6
ROPE_BASE = 10000.0
GRID_W = 64
LANES = 128
S5_T = 8
S5_SEG = 256
RET_CHUNK_MAX = 512
VMEM_LIMIT = 56 << 20


def _cparams(sem):
    return pltpu.CompilerParams(dimension_semantics=sem, vmem_limit_bytes=VMEM_LIMIT)


def _dot(a, b):
    return jnp.dot(a, b, preferred_element_type=F32)


def _dot_nt(a, b, precision=None):
    return lax.dot_general(a, b, (((1,), (1,)), ((), ())), precision=precision, preferred_element_type=F32)


def _rms(x, g):
    return x * lax.rsqrt(jnp.mean(x * x, axis=-1, keepdims=True) + NORM_EPS) * g


def _rms_mod(x, g, sc, sh):
    return _rms(x, g) * (1.0 + sc) + sh


def _sigmoid(x):
    return 1.0 / (1.0 + jnp.exp(-x))


def _mod_spec(chunk, d, rows_per_mod, tm):
    return pl.BlockSpec((1, 1, d), lambda i, *_: ((i * tm) // rows_per_mod, 0, chunk))


def _ada_kernel(c_ref, w_ref, b_ref, o_ref):
    c = c_ref[...]
    s = (c * _sigmoid(c)).astype(BF)
    o_ref[0] = _dot(s, w_ref[0].astype(BF)) + b_ref[0]


def _ada_mods(cvecs, ada_w, ada_b):
    depth, d, n = ada_w.shape
    tn = 1024
    return pl.pallas_call(
        _ada_kernel,
        grid=(depth, n // tn),
        in_specs=[pl.BlockSpec((8, d), lambda l, j: (0, 0)),
                  pl.BlockSpec((1, d, tn), lambda l, j: (l, 0, j)),
                  pl.BlockSpec((1, 1, tn), lambda l, j: (l, 0, j))],
        out_specs=pl.BlockSpec((1, 8, tn), lambda l, j: (l, 0, j)),
        out_shape=jax.ShapeDtypeStruct((depth, 8, n), F32),
        compiler_params=_cparams(("parallel", "parallel")),
        name="ada_mods",
    )(cvecs, ada_w, ada_b.reshape(depth, 1, n))


def _norm_mod_kernel(x_ref, g_ref, sh_ref, sc_ref, h_ref):
    h_ref[...] = _rms_mod(x_ref[...], g_ref[...], sc_ref[0], sh_ref[0]).astype(BF)


def _norm_mod(x, mods, ng0, rows_per_mod):
    m, d = x.shape
    tm = min(1024, m)
    return pl.pallas_call(
        _norm_mod_kernel,
        grid=(m // tm,),
        in_specs=[pl.BlockSpec((tm, d), lambda i: (i, 0)),
                  pl.BlockSpec((1, d), lambda i: (0, 0)),
                  _mod_spec(0, d, rows_per_mod, tm),
                  _mod_spec(1, d, rows_per_mod, tm)],
        out_specs=pl.BlockSpec((tm, d), lambda i: (i, 0)),
        out_shape=jax.ShapeDtypeStruct((m, d), BF),
        compiler_params=_cparams(("parallel",)),
        name="norm_mod",
    )(x, ng0, mods, mods)


def _inproj_kernel(h_ref, w_ref, og_ref, ou_ref, os_ref, *, n_gate_tiles, n_u_tiles):
    j = pl.program_id(1)
    r = _dot(h_ref[...], w_ref[...])

    @pl.when(j < n_gate_tiles)
    def _():
        og_ref[...] = r.astype(BF)

    @pl.when((j >= n_gate_tiles) & (j < n_gate_tiles + n_u_tiles))
    def _():
        ou_ref[...] = r

    @pl.when(j >= n_gate_tiles + n_u_tiles)
    def _():
        os_ref[...] = r


def _inproj(h, w_a, layer, n_gate, n_u):
    m, d = h.shape
    n = w_a.shape[-1]
    tm, tn = min(1024, m), 1024
    ngt, nut = n_gate // tn, n_u // tn
    nst = (n - n_gate - n_u) // tn
    return pl.pallas_call(
        functools.partial(_inproj_kernel, n_gate_tiles=ngt, n_u_tiles=nut),
        grid=(m // tm, ngt + nut + nst),
        in_specs=[pl.BlockSpec((tm, d), lambda i, j: (i, 0)),
                  pl.BlockSpec((None, d, tn), lambda i, j: (layer, 0, j))],
        out_specs=[pl.BlockSpec((tm, tn), lambda i, j: (i, jnp.minimum(j, ngt - 1))),
                   pl.BlockSpec((tm, tn), lambda i, j: (i, jnp.clip(j - ngt, 0, nut - 1))),
                   pl.BlockSpec((tm, tn), lambda i, j: (i, jnp.maximum(j - ngt - nut, 0)))],
        out_shape=[jax.ShapeDtypeStruct((m, n_gate), BF),
                   jax.ShapeDtypeStruct((m, n_u), F32),
                   jax.ShapeDtypeStruct((m, n - n_gate - n_u), F32)],
        compiler_params=_cparams(("parallel", "arbitrary")),
        name="in_proj",
    )(h, w_a)


def _rope(seg, cosp, sinp):
    return seg * cosp + pltpu.roll(seg, LANES // 2, 1) * sinp


def _ones_column(rows):
    lane = lax.broadcasted_iota(jnp.int32, (rows, LANES), 1)
    return jnp.where(lane == 0, 1.0, 0.0).astype(BF)


def _store_kv(kv, krot, k_ref, v_ref, n_heads):
    hv = n_heads * LANES
    ones = _ones_column(kv.shape[0])
    for hh in range(n_heads):
        b0 = hh * 2 * LANES
        k_ref[:, b0:b0 + LANES] = kv[:, hh * LANES:(hh + 1) * LANES].astype(BF)
        k_ref[:, b0 + LANES:b0 + 2 * LANES] = krot
        v_ref[:, b0:b0 + LANES] = kv[:, hv + hh * LANES:hv + (hh + 1) * LANES].astype(BF)
        v_ref[:, b0 + LANES:b0 + 2 * LANES] = ones


def _mla_proj_kernel(h_ref, wm_ref, qg_ref, kvg_ref, wq_ref, wkv_ref, cos_ref, sin_ref,
                     q_ref, k_ref, v_ref, ckv_ref, kr_ref, *, q_lora, kv_lora, n_heads, scale):
    z = _dot(h_ref[...], wm_ref[...])
    cosp, sinp = cos_ref[...], sin_ref[...]

    cqn = _rms(z[:, :q_lora], qg_ref[...]).astype(BF)
    qraw = _dot(cqn, wq_ref[...])
    for hh in range(n_heads):
        b0 = hh * 2 * LANES
        q_ref[:, b0:b0 + LANES] = (qraw[:, b0:b0 + LANES] * scale).astype(BF)
        q_ref[:, b0 + LANES:b0 + 2 * LANES] = (_rope(qraw[:, b0 + LANES:b0 + 2 * LANES], cosp, sinp) * scale).astype(BF)

    ckv = _rms(z[:, q_lora:q_lora + kv_lora], kvg_ref[...])
    ckv_ref[...] = ckv
    kv = _dot(ckv.astype(BF), wkv_ref[...])
    kr = z[:, q_lora + kv_lora:]
    kr_ref[...] = kr
    _store_kv(kv, _rope(kr, cosp, sinp).astype(BF), k_ref, v_ref, n_heads)


def _mla_proj(h, w_m, qg, kvg, wq_ext, wkv, cosp, sinp, n_heads, scale):
    m, d = h.shape
    q_lora, kv_lora = wq_ext.shape[0], wkv.shape[0]
    tm = min(256, m)
    tab_tiles = cosp.shape[0] // tm
    c2 = lambda i: (0, 0)
    return pl.pallas_call(
        functools.partial(_mla_proj_kernel, q_lora=q_lora, kv_lora=kv_lora, n_heads=n_heads, scale=scale),
        grid=(m // tm,),
        in_specs=[pl.BlockSpec((tm, d), lambda i: (i, 0)),
                  pl.BlockSpec(w_m.shape, c2),
                  pl.BlockSpec((1, q_lora), c2),
                  pl.BlockSpec((1, kv_lora), c2),
                  pl.BlockSpec(wq_ext.shape, c2),
                  pl.BlockSpec(wkv.shape, c2),
                  pl.BlockSpec((tm, LANES), lambda i: (i % tab_tiles, 0)),
                  pl.BlockSpec((tm, LANES), lambda i: (i % tab_tiles, 0))],
        out_specs=[pl.BlockSpec((tm, 2 * n_heads * LANES), lambda i: (i, 0)),
                   pl.BlockSpec((tm, 2 * n_heads * LANES), lambda i: (i, 0)),
                   pl.BlockSpec((tm, 2 * n_heads * LANES), lambda i: (i, 0)),
                   pl.BlockSpec((tm, kv_lora), lambda i: (i, 0)),
                   pl.BlockSpec((tm, LANES), lambda i: (i, 0))],
        out_shape=[jax.ShapeDtypeStruct((m, 2 * n_heads * LANES), BF),
                   jax.ShapeDtypeStruct((m, 2 * n_heads * LANES), BF),
                   jax.ShapeDtypeStruct((m, 2 * n_heads * LANES), BF),
                   jax.ShapeDtypeStruct((m, kv_lora), F32),
                   jax.ShapeDtypeStruct((m, LANES), F32)],
        compiler_params=_cparams(("parallel",)),
        name="mla_proj",
    )(h, w_m, qg, kvg, wq_ext, wkv, cosp, sinp)


def _cache_kv_kernel(ckv_ref, kr_ref, wkv_ref, k_ref, v_ref, *, n_heads):
    kv = _dot(ckv_ref[...].astype(BF), wkv_ref[...])
    _store_kv(kv, kr_ref[...].astype(BF), k_ref, v_ref, n_heads)


def _cache_kv(ckv, kr_pad, wkv, n_heads):
    rows, kvl = ckv.shape
    return pl.pallas_call(
        functools.partial(_cache_kv_kernel, n_heads=n_heads),
        grid=(1,),
        in_specs=[pl.BlockSpec((rows, kvl), lambda i: (0, 0)),
                  pl.BlockSpec((rows, LANES), lambda i: (0, 0)),
                  pl.BlockSpec(wkv.shape, lambda i: (0, 0))],
        out_specs=[pl.BlockSpec((rows, 2 * n_heads * LANES), lambda i: (0, 0)),
                   pl.BlockSpec((rows, 2 * n_heads * LANES), lambda i: (0, 0))],
        out_shape=[jax.ShapeDtypeStruct((rows, 2 * n_heads * LANES), BF),
                   jax.ShapeDtypeStruct((rows, 2 * n_heads * LANES), BF)],
        compiler_params=_cparams(("arbitrary",)),
        name="mla_cache_kv",
    )(ckv, kr_pad, wkv)


ATTN_ROWS = 256


def _softmax_pv(q, k, v):
    s = _dot_nt(q, k)
    p = jnp.exp2(s - jnp.max(s, axis=-1, keepdims=True)).astype(BF)
    acc = _dot(p, v)
    return (acc[:, :LANES] / acc[:, LANES:LANES + 1]).astype(BF)


def _attn_kernel(q_ref, k_ref, v_ref, o_ref):
    k, v = k_ref[0], v_ref[0]
    for r0 in range(0, q_ref.shape[1], ATTN_ROWS):
        o_ref[0, r0:r0 + ATTN_ROWS] = _softmax_pv(q_ref[0, r0:r0 + ATTN_ROWS], k, v)


def _attn_heads_kernel(q_ref, k_ref, v_ref, o_ref, *, n_heads):
    for hh in range(n_heads):
        hk = slice(hh * 2 * LANES, (hh + 1) * 2 * LANES)
        o_ref[0, :, hh * LANES:(hh + 1) * LANES] = _softmax_pv(q_ref[0, :, hk], k_ref[0, :, hk], v_ref[0, :, hk])


def _attention(q, k, v, n_heads):
    b, l, _ = q.shape
    s = k.shape[1]
    hw = 2 * LANES
    out_shape = jax.ShapeDtypeStruct((b, l, n_heads * LANES), BF)
    if l * n_heads <= 4096:
        return pl.pallas_call(
            functools.partial(_attn_heads_kernel, n_heads=n_heads),
            grid=(b,),
            in_specs=[pl.BlockSpec((1, l, n_heads * hw), lambda bi: (bi, 0, 0)),
                      pl.BlockSpec((1, s, n_heads * hw), lambda bi: (bi, 0, 0)),
                      pl.BlockSpec((1, s, n_heads * hw), lambda bi: (bi, 0, 0))],
            out_specs=pl.BlockSpec((1, l, n_heads * LANES), lambda bi: (bi, 0, 0)),
            out_shape=out_shape,
            compiler_params=_cparams(("parallel",)),
            name="mla_attention_ctx",
        )(q, k, v)
    tq = min(2048, l)
    return pl.pallas_call(
        _attn_kernel,
        grid=(b, n_heads, l // tq),
        in_specs=[pl.BlockSpec((1, tq, hw), lambda bi, h, i: (bi, i, h)),
                  pl.BlockSpec((1, s, hw), lambda bi, h, i: (bi, 0, h)),
                  pl.BlockSpec((1, s, hw), lambda bi, h, i: (bi, 0, h))],
        out_specs=pl.BlockSpec((1, tq, LANES), lambda bi, h, i: (bi, i, h)),
        out_shape=out_shape,
        compiler_params=_cparams(("parallel", "parallel", "arbitrary")),
        name="mla_attention",
    )(q, k, v)


def _ret_kernel(*refs, with_state, n_chunks, c_len, n_hp, k_scale):
    if with_state:
        (q_ref, k_ref, v_ref, g_ref, dec_ref, ng_ref, s0f_ref, s0b_ref,
         y_ref, dm_ref, kvf_ref, kvb_ref, sf_ref, sb_ref) = refs
    else:
        (q_ref, k_ref, v_ref, g_ref, dec_ref, ng_ref,
         y_ref, ff_ref, fb_ref, dm_ref, kvf_ref, kvb_ref, sf_ref, sb_ref) = refs
    reps = c_len // LANES

    @pl.when(pl.program_id(1) == 0)
    def _():
        di = (lax.broadcasted_iota(jnp.int32, (c_len, c_len), 0)
              - lax.broadcasted_iota(jnp.int32, (c_len, c_len), 1)).astype(F32)
        for hp in range(n_hp):
            lgf = jnp.tile(-jnp.exp(dec_ref[0, hp]), (1, reps))
            lgb = jnp.tile(-jnp.exp(dec_ref[1, hp]), (1, reps))
            dm_ref[hp] = jnp.where(di > 0, jnp.exp(lgf * jnp.maximum(di, 0.0)),
                                   jnp.where(di < 0, jnp.exp(lgb * jnp.maximum(-di, 0.0)), 2.0))

    ii = lax.broadcasted_iota(jnp.int32, (c_len, LANES), 0).astype(F32)
    need_state = with_state or n_chunks > 1
    for hp in range(n_hp):
        cs = slice(hp * LANES, (hp + 1) * LANES)
        lgf = -jnp.exp(dec_ref[0, hp])
        lgb = -jnp.exp(dec_ref[1, hp])
        dk_f = jnp.exp(lgf * (c_len - 1.0 - ii))
        dk_b = jnp.exp(lgb * ii)
        ds_f = jnp.exp(lgf * c_len)
        ds_b = jnp.exp(lgb * c_len)

        for c in range(n_chunks):
            rows = slice(c * c_len, (c + 1) * c_len)
            kc = k_ref[0, rows, cs] * k_scale
            vc = v_ref[0, rows, cs].astype(BF)
            kvf_ref[c] = _dot((kc * dk_f).T.astype(BF), vc)
            kvb_ref[c] = _dot((kc * dk_b).T.astype(BF), vc)

        if with_state:
            s_f, s_b = s0f_ref[0, hp], s0b_ref[0, hp]
        else:
            s_f = s_b = jnp.zeros((LANES, LANES), F32)
        for c in range(n_chunks):
            sf_ref[c] = s_f
            s_f = ds_f * s_f + kvf_ref[c]
        for c in reversed(range(n_chunks)):
            sb_ref[c] = s_b
            s_b = ds_b * s_b + kvb_ref[c]
        if not with_state:
            ff_ref[0, hp] = s_f
            fb_ref[0, hp] = s_b

        ng = ng_ref[:, cs]
        dq_f = jnp.exp(lgf * (ii + 1.0))
        dq_b = jnp.exp(lgb * (c_len - ii))
        for c in range(n_chunks):
            rows = slice(c * c_len, (c + 1) * c_len)
            qc = q_ref[0, rows, cs]
            kc = (k_ref[0, rows, cs] * k_scale).astype(BF)
            vc = v_ref[0, rows, cs].astype(BF)
            att = _dot_nt(qc.astype(BF), kc) * dm_ref[hp]
            o = _dot(att.astype(BF), vc)
            if need_state:
                o = (o + _dot((qc * dq_f).astype(BF), sf_ref[c].astype(BF))
                     + _dot((qc * dq_b).astype(BF), sb_ref[c].astype(BF)))
            mu = jnp.mean(o, axis=-1, keepdims=True)
            oc = o - mu
            var = jnp.mean(oc * oc, axis=-1, keepdims=True)
            on = oc * lax.rsqrt(var + NORM_EPS)
            gg = g_ref[0, rows, cs]
            y_ref[0, rows, cs] = (gg * _sigmoid(gg) * (on * ng)).astype(BF)


def _retention(zs, col0, dec, ng, s0f, s0b, n_heads):
    b, l, _ = zs.shape
    c_len = min(RET_CHUNK_MAX, l)
    nc = l // c_len
    with_state = s0f is not None
    n_hp = 1 if nc > 1 else n_heads
    hw = n_hp * LANES
    cb = col0 // hw

    def zspec(seg):
        return pl.BlockSpec((1, l, hw), lambda hg, bi: (bi, 0, cb + seg * (n_heads // n_hp) + hg))

    st_spec = pl.BlockSpec((1, n_hp, LANES, LANES), lambda hg, bi: (bi, hg, 0, 0))
    in_specs = [zspec(0), zspec(1), zspec(2), zspec(3),
                pl.BlockSpec((2, n_hp, 1, LANES), lambda hg, bi: (0, hg, 0, 0)),
                pl.BlockSpec((1, hw), lambda hg, bi: (0, hg))]
    args = [zs, zs, zs, zs, dec, ng]
    y_spec = pl.BlockSpec((1, l, hw), lambda hg, bi: (bi, 0, hg))
    y_shape = jax.ShapeDtypeStruct((b, l, n_heads * LANES), BF)
    if with_state:
        in_specs += [st_spec, st_spec]
        args += [s0f, s0b]
        out_specs, out_shape = y_spec, y_shape
    else:
        st_shape = jax.ShapeDtypeStruct((b, n_heads, LANES, LANES), F32)
        out_specs, out_shape = [y_spec, st_spec, st_spec], [y_shape, st_shape, st_shape]
    scratch = ([pltpu.VMEM((n_hp, c_len, c_len), F32)]
               + [pltpu.VMEM((nc, LANES, LANES), F32) for _ in range(4)])
    return pl.pallas_call(
        functools.partial(_ret_kernel, with_state=with_state, n_chunks=nc, c_len=c_len, n_hp=n_hp,
                          k_scale=float(LANES) ** -0.5),
        grid=(n_heads // n_hp, b),
        in_specs=in_specs, out_specs=out_specs, out_shape=out_shape,
        scratch_shapes=scratch,
        compiler_params=_cparams(("arbitrary", "arbitrary")),
        name="retention",
    )(*args)


def _s5_prep_kernel(lre_ref, lim_ref, ldt_ref, btre_ref, btim_ref, cre_ref, cim_ref,
                    toep_ref, wd_ref, wo_ref, av_ref, *, ch, p, seg_chunks):
    t = S5_T
    gw = (LANES // ch) * p
    btre, btim, cre, cim = btre_ref[...], btim_ref[...], cre_ref[...], cim_ref[...]
    r1 = lax.broadcasted_iota(jnp.int32, (LANES, LANES), 0)
    c1 = lax.broadcasted_iota(jnp.int32, (LANES, LANES), 1)
    same_group = (r1 // ch) == (c1 // ch)
    lane_valid = c1 < p
    r5 = lax.broadcasted_iota(jnp.int32, (LANES, gw), 0)
    c5 = lax.broadcasted_iota(jnp.int32, (LANES, gw), 1)
    own_state = (r5 // ch) == (c5 // p)

    def spread(x):
        x2 = x + pltpu.roll(x, LANES // 2, 1)
        return jnp.where(own_state, jnp.concatenate([x2] * (gw // LANES), axis=1), 0.0)

    xs, ys, lags = [], [], []
    for d in range(2):
        lr, li, dt = lre_ref[d], lim_ref[d], jnp.exp(ldt_ref[d])
        ar, ai = lr * dt, li * dt
        mag = jnp.exp(ar)
        abr, abi = mag * jnp.cos(ai), mag * jnp.sin(ai)
        nr, ni = abr - 1.0, abi
        den = lr * lr + li * li
        cr = (nr * lr + ni * li) / den
        ci = (ni * lr - nr * li) / den
        pr, pi = jnp.ones_like(lr), jnp.zeros_like(lr)
        xd, yd, gd = [], [], []
        for kk in range(t + 1):
            if kk < t:
                wr, wi = cr * pr - ci * pi, cr * pi + ci * pr
                xr, xi = wr * btre - wi * btim, wr * btim + wi * btre
                xd.append((xr, xi))
                hi = lax.Precision.HIGHEST
                gd.append(jnp.where(same_group, _dot_nt(xr, cre, hi) - _dot_nt(xi, cim, hi), 0.0))
            if kk >= 1:
                yd.append((cre * pr - cim * pi, -(cre * pi + cim * pr)))
            if kk == t:
                qr, qi = jnp.where(lane_valid, pr, 0.0), jnp.where(lane_valid, pi, 0.0)
            pr, pi = pr * abr - pi * abi, pr * abi + pi * abr
        xs.append(xd)
        ys.append(yd)
        lags.append(gd)
        inv = 1.0 / ch
        av_ref[0, 2 * d:2 * d + 1] = jnp.sum(spread(qr), axis=0, keepdims=True) * inv
        av_ref[0, 2 * d + 1:2 * d + 2] = jnp.sum(spread(qi), axis=0, keepdims=True) * inv
        for _ in range(seg_chunks.bit_length() - 1):
            qr, qi = qr * qr - qi * qi, 2.0 * qr * qi
        av_ref[0, 4 + 2 * d:5 + 2 * d] = jnp.sum(spread(qr), axis=0, keepdims=True) * inv
        av_ref[0, 5 + 2 * d:6 + 2 * d] = jnp.sum(spread(qi), axis=0, keepdims=True) * inv

    for ti in range(t):
        for to in range(t):
            lag = to - ti
            tile = lags[0][lag] if lag > 0 else (lags[1][-lag] if lag < 0 else lags[0][0] + lags[1][0])
            toep_ref[0, ti * LANES:(ti + 1) * LANES, to * LANES:(to + 1) * LANES] = tile.astype(BF)
    for tt in range(t):
        rows = slice(tt * LANES, (tt + 1) * LANES)
        xf, xb = xs[0][t - 1 - tt], xs[1][tt]
        yf, yb = ys[0][tt], ys[1][t - 1 - tt]
        for m, (xv, yv) in enumerate(((xf[0], yf[0]), (xf[1], yf[1]), (xb[0], yb[0]), (xb[1], yb[1]))):
            wd_ref[0, rows, m * gw:(m + 1) * gw] = spread(xv).astype(BF)
            wo_ref[0, m * gw:(m + 1) * gw, rows] = spread(yv).T.astype(BF)


def _s5_prep(lam_re, lam_im, log_dt, b_re, b_im, c_re, c_im):
    _, g, p = lam_re.shape
    ch = b_re.shape[-1]
    assert p == LANES // 2 and LANES % ch == 0
    nb = g * ch // LANES
    gw = (LANES // ch) * p
    t = S5_T
    seg_chunks = S5_SEG // t
    assert seg_chunks & (seg_chunks - 1) == 0

    def prow(a, padval):
        return jnp.pad(jnp.repeat(a, ch, axis=1), ((0, 0), (0, 0), (0, LANES - p)), constant_values=padval)

    def wrow(a):
        return jnp.pad(a.reshape(g * ch, p), ((0, 0), (0, LANES - p)))

    lre = prow(lam_re, -0.5)
    lim = prow(lam_im, 0.0)
    ldt = prow(jnp.broadcast_to(log_dt[..., None], lam_re.shape), 0.0)
    lspec = pl.BlockSpec((2, LANES, LANES), lambda i: (0, i, 0))
    wspec = pl.BlockSpec((LANES, LANES), lambda i: (i, 0))
    return pl.pallas_call(
        functools.partial(_s5_prep_kernel, ch=ch, p=p, seg_chunks=seg_chunks),
        grid=(nb,),
        in_specs=[lspec, lspec, lspec, wspec, wspec, wspec, wspec],
        out_specs=[pl.BlockSpec((1, t * LANES, t * LANES), lambda i: (i, 0, 0)),
                   pl.BlockSpec((1, t * LANES, 4 * gw), lambda i: (i, 0, 0)),
                   pl.BlockSpec((1, 4 * gw, t * LANES), lambda i: (i, 0, 0)),
                   pl.BlockSpec((1, 8, gw), lambda i: (i, 0, 0))],
        out_shape=[jax.ShapeDtypeStruct((nb, t * LANES, t * LANES), BF),
                   jax.ShapeDtypeStruct((nb, t * LANES, 4 * gw), BF),
                   jax.ShapeDtypeStruct((nb, 4 * gw, t * LANES), BF),
                   jax.ShapeDtypeStruct((nb, 8, gw), F32)],
        compiler_params=_cparams(("parallel",)),
        name="s5_prep",
    )(lre, lim, ldt, wrow(jnp.swapaxes(b_re, 1, 2)), wrow(jnp.swapaxes(b_im, 1, 2)), wrow(c_re), wrow(c_im))


def _gelu(x):
    return 0.5 * x * (1.0 + jnp.tanh(0.7978845608028654 * (x + 0.044715 * (x * x * x))))


def _s5_kernel(*refs, chain):
    if chain:
        (u_ref, toep_ref, wd_ref, wo_ref, a_ref, dsk_ref, s0_ref,
         y_ref, lhs_ref, d_ref, s_ref, xs_ref) = refs
    else:
        (u_ref, toep_ref, wd_ref, wo_ref, a_ref, dsk_ref,
         y_ref, fin_ref, lhs_ref, d_ref, s_ref) = refs
    nc, t, pb, _ = u_ref.shape
    rows = nc * pb
    w = a_ref.shape[-1]
    for tt in range(t):
        lhs_ref[:, tt * LANES:(tt + 1) * LANES] = u_ref[:, tt].reshape(rows, LANES).astype(BF)
    d_ref[...] = _dot(lhs_ref[...], wd_ref[0])
    av = a_ref[0]
    afr, afi, abr, abi = av[0:1], av[1:2], av[2:3], av[3:4]

    def scan(init, store):
        sfr, sfi, sbr, sbi = init
        for c in range(nc):
            rf = slice(c * pb, (c + 1) * pb)
            rb = slice((nc - 1 - c) * pb, (nc - c) * pb)
            if store:
                s_ref[rf, 0:w] = sfr.astype(BF)
                s_ref[rf, w:2 * w] = sfi.astype(BF)
                s_ref[rb, 2 * w:3 * w] = sbr.astype(BF)
                s_ref[rb, 3 * w:4 * w] = sbi.astype(BF)
            sfr, sfi = (afr * sfr - afi * sfi + d_ref[rf, 0:w],
                        afr * sfi + afi * sfr + d_ref[rf, w:2 * w])
            sbr, sbi = (abr * sbr - abi * sbi + d_ref[rb, 2 * w:3 * w],
                        abr * sbi + abi * sbr + d_ref[rb, 3 * w:4 * w])
        return sfr, sfi, sbr, sbi

    zero = jnp.zeros((pb, w), F32)
    if not chain:
        fin = scan((zero, zero, zero, zero), True)
        for m in range(4):
            fin_ref[0, :, m * w:(m + 1) * w] = fin[m]
    else:
        efr, efi, ebr, ebi = scan((zero, zero, zero, zero), False)
        s0 = s0_ref[0, 0]
        xr, xi = s0[0:1], s0[1:2]
        for j in range(pb):
            xs_ref[j:j + 1, 0:w] = xr
            xs_ref[j:j + 1, w:2 * w] = xi
            xr, xi = (av[4:5] * xr - av[5:6] * xi + efr[j:j + 1],
                      av[4:5] * xi + av[5:6] * xr + efi[j:j + 1])
        xr, xi = s0[2:3], s0[3:4]
        for j in reversed(range(pb)):
            xs_ref[j:j + 1, 2 * w:3 * w] = xr
            xs_ref[j:j + 1, 3 * w:4 * w] = xi
            xr, xi = (av[6:7] * xr - av[7:8] * xi + ebr[j:j + 1],
                      av[6:7] * xi + av[7:8] * xr + ebi[j:j + 1])
        scan((xs_ref[:, 0:w], xs_ref[:, w:2 * w], xs_ref[:, 2 * w:3 * w], xs_ref[:, 3 * w:4 * w]), True)

    yv = _dot(lhs_ref[...], toep_ref[0]) + _dot(s_ref[...], wo_ref[0])
    dsk = dsk_ref[...]
    for tt in range(t):
        yt = yv[:, tt * LANES:(tt + 1) * LANES] + dsk * u_ref[:, tt].reshape(rows, LANES)
        y_ref[:, tt] = _gelu(yt).reshape(nc, pb, LANES)


def _s5(u4, toep, wd, wo, av, dskip, s0):
    nc, t, npb, nch = u4.shape
    nb = toep.shape[0]
    chain = s0 is not None
    pb = npb // s0.shape[1] if chain else min(16, npb)
    sw = wd.shape[-1]
    rows = nc * pb
    ublk = pl.BlockSpec((nc, t, pb, LANES), lambda kb, hh: (0, 0, hh, kb))

    def wspec(arr):
        return pl.BlockSpec((1,) + arr.shape[1:], lambda kb, hh: (kb, 0, 0))

    in_specs = [ublk, wspec(toep), wspec(wd), wspec(wo), wspec(av),
                pl.BlockSpec((1, LANES), lambda kb, hh: (0, kb))]
    args = [u4, toep, wd, wo, av, dskip]
    scratch = [pltpu.VMEM((rows, t * LANES), BF), pltpu.VMEM((rows, sw), F32), pltpu.VMEM((rows, sw), BF)]
    if chain:
        in_specs += [pl.BlockSpec((1, 1, 4, sw // 4), lambda kb, hh: (kb, hh, 0, 0))]
        args += [s0]
        out_specs = ublk
        out_shape = jax.ShapeDtypeStruct(u4.shape, F32)
        scratch.append(pltpu.VMEM((pb, sw), F32))
    else:
        out_specs = [ublk, pl.BlockSpec((1, pb, sw), lambda kb, hh: (kb, hh, 0))]
        out_shape = [jax.ShapeDtypeStruct(u4.shape, F32), jax.ShapeDtypeStruct((nb, npb, sw), F32)]
    return pl.pallas_call(
        functools.partial(_s5_kernel, chain=chain),
        grid=(nb, npb // pb),
        in_specs=in_specs, out_specs=out_specs, out_shape=out_shape,
        scratch_shapes=scratch,
        compiler_params=_cparams(("parallel", "parallel")),
        name="s5_chunked",
    )(*args)


def _glu_kernel(y_ref, w_ref, b_ref, o_ref):
    y = y_ref[...]
    o_ref[...] = (y * _sigmoid(_dot(y.astype(BF), w_ref[...]) + b_ref[...])).astype(BF)


def _glu(y, w, layer, b):
    m, n = y.shape
    tm = min(1024, m)
    return pl.pallas_call(
        _glu_kernel,
        grid=(m // tm,),
        in_specs=[pl.BlockSpec((tm, n), lambda i: (i, 0)),
                  pl.BlockSpec((None, n, n), lambda i: (layer, 0, 0)),
                  pl.BlockSpec((1, n), lambda i: (0, 0))],
        out_specs=pl.BlockSpec((tm, n), lambda i: (i, 0)),
        out_shape=jax.ShapeDtypeStruct((m, n), BF),
        compiler_params=_cparams(("parallel",)),
        name="s5_glu",
    )(y, w, b)


def _accumulate(acc_ref, part, step):
    @pl.when(step == 0)
    def _():
        acc_ref[...] = part

    @pl.when(step > 0)
    def _():
        acc_ref[...] += part


def _res_epilogue(acc_ref, x_ref, gate_ref, ng_ref, o_ref, nxt_refs):
    gn = gate_ref[0] * ng_ref[...]
    if nxt_refs is not None:
        ng2_ref, sh2_ref, sc2_ref, h_ref = nxt_refs
        g2 = ng2_ref[...] * (1.0 + sc2_ref[0])
        sh2 = sh2_ref[0]
    rows = 128
    for r0 in range(0, o_ref.shape[0], rows):
        acc = acc_ref[r0:r0 + rows]
        inv = lax.rsqrt(jnp.mean(acc * acc, axis=-1, keepdims=True) + NORM_EPS)
        xn = x_ref[r0:r0 + rows] + (acc * inv) * gn
        o_ref[r0:r0 + rows] = xn
        if nxt_refs is not None:
            inv2 = lax.rsqrt(jnp.mean(xn * xn, axis=-1, keepdims=True) + NORM_EPS)
            h_ref[r0:r0 + rows] = ((xn * inv2) * g2 + sh2).astype(BF)


def _merge_out_kernel(ya_ref, yb_ref, yc_ref, ga_ref, gb_ref, gc_ref, wb_ref, wo_ref,
                      x_ref, gate_ref, ng_ref, ng2_ref, sh2_ref, sc2_ref, o_ref, h_ref, acc_ref):
    j = pl.program_id(1)
    merged = (_sigmoid(ga_ref[...].astype(F32)) * _dot(ya_ref[...], wb_ref[0])
              + _sigmoid(gb_ref[...].astype(F32)) * _dot(yb_ref[...], wb_ref[1])
              + _sigmoid(gc_ref[...].astype(F32)) * _dot(yc_ref[...], wb_ref[2]))
    _accumulate(acc_ref, _dot(merged.astype(BF), wo_ref[...]), j)

    @pl.when(j == pl.num_programs(1) - 1)
    def _():
        _res_epilogue(acc_ref, x_ref, gate_ref, ng_ref, o_ref, (ng2_ref, sh2_ref, sc2_ref, h_ref))


def _merge_out(ya, yb, yc, zg, wb, wo, layer, x, mods, ng, ng2, rows_per_mod):
    m, dm = ya.shape
    d = wb.shape[-1]
    tm, tn = min(512, m), 512
    nt = d // tn
    ysp = pl.BlockSpec((tm, dm), lambda i, j: (i, 0))
    rsp = pl.BlockSpec((tm, d), lambda i, j: (i, 0))
    vsp = pl.BlockSpec((1, d), lambda i, j: (0, 0))

    def gsp(br):
        return pl.BlockSpec((tm, tn), lambda i, j: (i, j + br * nt))

    return pl.pallas_call(
        _merge_out_kernel,
        grid=(m // tm, nt),
        in_specs=[ysp, ysp, ysp, gsp(0), gsp(1), gsp(2),
                  pl.BlockSpec((None, 3, dm, tn), lambda i, j: (layer, 0, 0, j)),
                  pl.BlockSpec((None, tn, d), lambda i, j: (layer, j, 0)),
                  rsp, _mod_spec(2, d, rows_per_mod, tm), vsp,
                  vsp, _mod_spec(3, d, rows_per_mod, tm), _mod_spec(4, d, rows_per_mod, tm)],
        out_specs=[rsp, rsp],
        out_shape=[jax.ShapeDtypeStruct((m, d), F32), jax.ShapeDtypeStruct((m, d), BF)],
        scratch_shapes=[pltpu.VMEM((tm, d), F32)],
        compiler_params=_cparams(("parallel", "arbitrary")),
        name="merge_out_proj",
    )(ya, yb, yc, zg, zg, zg, wb, wo, x, mods, ng, ng2, mods, mods)


def _mm_res_kernel(*refs, with_next):
    if with_next:
        a_ref, w_ref, x_ref, gate_ref, ng_ref, ng2_ref, sh2_ref, sc2_ref, o_ref, h_ref, acc_ref = refs
        nxt_refs = (ng2_ref, sh2_ref, sc2_ref, h_ref)
    else:
        a_ref, w_ref, x_ref, gate_ref, ng_ref, o_ref, acc_ref = refs
        nxt_refs = None
    k = pl.program_id(1)
    _accumulate(acc_ref, _dot(a_ref[...], w_ref[...]), k)

    @pl.when(k == pl.num_programs(1) - 1)
    def _():
        _res_epilogue(acc_ref, x_ref, gate_ref, ng_ref, o_ref, nxt_refs)


def _mm_res(a, w, layer, x, mods, gate_chunk, ng, rows_per_mod, nxt=None):
    m, kdim = a.shape
    d = w.shape[-1]
    tm = min(512, m)
    tk = kdim // 4 if kdim > 2048 else kdim // 2
    assert tk % LANES == 0
    in_specs = [pl.BlockSpec((tm, tk), lambda i, k: (i, k)),
                pl.BlockSpec((None, tk, d), lambda i, k: (layer, k, 0)),
                pl.BlockSpec((tm, d), lambda i, k: (i, 0)),
                _mod_spec(gate_chunk, d, rows_per_mod, tm),
                pl.BlockSpec((1, d), lambda i, k: (0, 0))]
    args = [a, w, x, mods, ng]
    x_spec = pl.BlockSpec((tm, d), lambda i, k: (i, 0))
    x_shape = jax.ShapeDtypeStruct((m, d), F32)
    if nxt is None:
        out_specs, out_shape = x_spec, x_shape
    else:
        ng2, mods2, sh_chunk, sc_chunk = nxt
        in_specs += [pl.BlockSpec((1, d), lambda i, k: (0, 0)),
                     _mod_spec(sh_chunk, d, rows_per_mod, tm),
                     _mod_spec(sc_chunk, d, rows_per_mod, tm)]
        args += [ng2, mods2, mods2]
        out_specs = [x_spec, pl.BlockSpec((tm, d), lambda i, k: (i, 0))]
        out_shape = [x_shape, jax.ShapeDtypeStruct((m, d), BF)]
    return pl.pallas_call(
        functools.partial(_mm_res_kernel, with_next=nxt is not None),
        grid=(m // tm, kdim // tk),
        in_specs=in_specs, out_specs=out_specs, out_shape=out_shape,
        scratch_shapes=[pltpu.VMEM((tm, d), F32)],
        compiler_params=_cparams(("parallel", "arbitrary")),
        name="matmul_norm_residual",
    )(*args)


FFN_HALO = 16


def _ffn_up_kernel(hp_ref, h_ref, hn_ref, wv_ref, wg_ref, cwv_ref, cwg_ref, cbv_ref, cbg_ref,
                   o_ref, hb_ref, uv_ref, ug_ref, *, tm, piece, seq_len):
    i, j = pl.program_id(0), pl.program_id(1)
    hl = FFN_HALO
    n_pieces = tm // piece
    stride = piece + hl

    @pl.when(j == 0)
    def _():
        zero = jnp.zeros(hp_ref.shape, BF)
        if n_pieces == 1 and seq_len > tm:
            hb_ref[0:hl] = jnp.where(lax.rem(i * tm, seq_len) != 0, hp_ref[...], zero)
            hb_ref[stride:stride + hl] = jnp.where(lax.rem((i + 1) * tm, seq_len) != 0, hn_ref[...], zero)
        else:
            for pc in range(n_pieces + 1):
                hb_ref[pc * stride:pc * stride + hl] = zero
        for pc in range(n_pieces):
            hb_ref[hl + pc * stride:hl + pc * stride + piece] = h_ref[pc * piece:(pc + 1) * piece]

    hb = hb_ref[...]
    uv_ref[...] = _dot(hb, wv_ref[...])
    ug_ref[...] = _dot(hb, wg_ref[...])
    cwv, cwg, cbv, cbg = cwv_ref[...], cwg_ref[...], cbv_ref[...], cbg_ref[...]
    for pc in range(n_pieces):
        b0 = hl + pc * stride

        def conv(u_ref, cw, cb):
            return (u_ref[b0 - 1:b0 - 1 + piece] * cw[0:1] + u_ref[b0:b0 + piece] * cw[1:2]
                    + u_ref[b0 + 1:b0 + 1 + piece] * cw[2:3] + cb)

        val = conv(uv_ref, cwv, cbv)
        gate = conv(ug_ref, cwg, cbg)
        o_ref[pc * piece:(pc + 1) * piece] = (gate * _sigmoid(gate) * val).astype(BF)


def _ffn_up(h, w_up, layer, conv_w, conv_b, seq_len):
    m, d = h.shape
    f = w_up.shape[-1] // 2
    tm, tf = min(1024, m), 512
    piece = min(seq_len, tm)
    assert tm % piece == 0 and seq_len % piece == 0
    nf = f // tf
    hl = FFN_HALO
    nblk = m // hl
    rows = (tm // piece) * (piece + hl) + hl

    return pl.pallas_call(
        functools.partial(_ffn_up_kernel, tm=tm, piece=piece, seq_len=seq_len),
        grid=(m // tm, nf),
        in_specs=[pl.BlockSpec((hl, d), lambda i, j: (jnp.maximum(i * (tm // hl) - 1, 0), 0)),
                  pl.BlockSpec((tm, d), lambda i, j: (i, 0)),
                  pl.BlockSpec((hl, d), lambda i, j: (jnp.minimum((i + 1) * (tm // hl), nblk - 1), 0)),
                  pl.BlockSpec((None, d, tf), lambda i, j: (layer, 0, j)),
                  pl.BlockSpec((None, d, tf), lambda i, j: (layer, 0, j + nf)),
                  pl.BlockSpec((3, tf), lambda i, j: (0, j)),
                  pl.BlockSpec((3, tf), lambda i, j: (0, j + nf)),
                  pl.BlockSpec((1, tf), lambda i, j: (0, j)),
                  pl.BlockSpec((1, tf), lambda i, j: (0, j + nf))],
        out_specs=pl.BlockSpec((tm, tf), lambda i, j: (i, j)),
        out_shape=jax.ShapeDtypeStruct((m, f), BF),
        scratch_shapes=[pltpu.VMEM((rows, d), BF), pltpu.VMEM((rows, tf), F32), pltpu.VMEM((rows, tf), F32)],
        compiler_params=_cparams(("parallel", "arbitrary")),
        name="ffn_up_conv",
    )(h, h, h, w_up, w_up, conv_w, conv_w, conv_b, conv_b)


def _pair_swap(w):
    r = w.reshape(w.shape[:-1] + (w.shape[-1] // 2, 2))
    return jnp.stack([-r[..., 1], r[..., 0]], axis=-1).reshape(w.shape)


def _rope_tables(n_tokens, dr):
    rows = n_tokens // GRID_W
    row = jnp.broadcast_to(jnp.arange(rows, dtype=F32)[:, None], (rows, GRID_W)).reshape(-1)
    col = jnp.broadcast_to(jnp.arange(GRID_W, dtype=F32)[None, :], (rows, GRID_W)).reshape(-1)
    nf = dr // 4
    inv = ROPE_BASE ** (-jnp.arange(nf, dtype=F32) / nf)
    ang = jnp.concatenate([row[:, None] * inv, col[:, None] * inv], axis=-1)
    ang = jnp.repeat(ang, 2, axis=-1)
    pad = jnp.zeros((n_tokens, LANES - dr), F32)
    return jnp.concatenate([jnp.cos(ang), pad], axis=-1), jnp.concatenate([jnp.sin(ang), pad], axis=-1)


def kernel(x_prompt, x_sample, cache_mla_ckv, cache_mla_krope, state_ret_fwd, state_ret_bwd, state_s5_fwd, state_s5_bwd, c, c_ctx, ada_w, ada_b, norm_g, w_in, s5_lam_re, s5_lam_im, s5_log_dt, s5_b_re, s5_b_im, s5_c_re, s5_c_im, s5_d, s5_glu_w, s5_glu_b, ret_decay, ret_norm_g, mla_q_norm, mla_kv_norm, mla_w_uq, mla_w_uk, mla_w_uv, w_branch, w_out, ffn_w_up, ffn_conv_w, ffn_conv_b, ffn_w_down):
    bsz, seq, d = x_prompt.shape
    dbsz, dseq, _ = x_sample.shape
    depth = ada_w.shape[0]
    dm = d // 2
    n_heads = ret_decay.shape[-1]
    dr = cache_mla_krope.shape[-1]
    q_lora, kv_lora = mla_w_uq.shape[1], mla_w_uk.shape[1]
    g5, p5, ch5 = s5_b_re.shape[1:]
    assert dm == n_heads * LANES and dr == LANES // 2 and mla_w_uk.shape[-1] == dm
    assert seq == S5_SEG and dseq % S5_SEG == 0
    qscale = float(LANES + dr) ** -0.5 * 1.4426950408889634

    cvecs = jnp.concatenate([c_ctx[None], c, jnp.zeros((8 - 1 - dbsz, d), F32)], axis=0)
    mods_all = _ada_mods(cvecs, ada_w, ada_b)

    cos_l, sin_l = _rope_tables(dseq, dr)
    ones_tab = jnp.concatenate([jnp.ones((256, dr), F32), jnp.zeros((256, LANES - dr), F32)], axis=-1)
    zeros_tab = jnp.zeros((256, LANES), F32)

    cw = np.cumsum([0, dm, dm, dm, dm, dm, q_lora, kv_lora, dr, d, d, d])
    w_a = jnp.concatenate([w_in[:, :, cw[8]:cw[11]], w_in[:, :, cw[0]:cw[5]]], axis=2).astype(BF)
    w_kr = w_in[:, :, cw[7]:cw[8]]
    w_m = jnp.concatenate([w_in[:, :, cw[5]:cw[7]], w_kr, _pair_swap(w_kr)], axis=2).astype(BF)
    uq = mla_w_uq.reshape(depth, q_lora, n_heads, LANES + dr)
    wq_ext = jnp.concatenate([uq[..., :LANES], uq[..., LANES:], _pair_swap(uq[..., LANES:])],
                             axis=-1).reshape(depth, q_lora, n_heads * 2 * LANES).astype(BF)
    wkv = jnp.concatenate([mla_w_uk, mla_w_uv], axis=2).astype(BF)
    glu_w = s5_glu_w.astype(BF)
    wb = w_branch.astype(BF)
    wo_ = w_out.astype(BF)
    wup = ffn_w_up.astype(BF)
    wdn = ffn_w_down.astype(BF)

    outs = {k: [] for k in ("ckv", "krope", "retf", "retb", "s5f", "s5b")}
    nb5 = g5 * ch5 // LANES
    gb5 = LANES // ch5

    def group_mods(l, ctx):
        if ctx:
            return mods_all[l, 0:1].reshape(1, 1, -1), bsz * seq
        return mods_all[l, 1:1 + dbsz].reshape(dbsz, 1, -1), dseq

    xres = [x_prompt.reshape(bsz * seq, d), x_sample.reshape(dbsz * dseq, d)]
    hres = []
    for grp in range(2):
        mods, rpm = group_mods(0, grp == 0)
        hres.append(_norm_mod(xres[grp], mods, norm_g[0, 0].reshape(1, d), rpm))

    for l in range(depth):
        qg = mla_q_norm[l].reshape(1, q_lora)
        kvg = mla_kv_norm[l].reshape(1, kv_lora)
        ng = norm_g[l].reshape(4, 1, d)
        glu_b = s5_glu_b[l].reshape(1, dm)
        conv_w = ffn_conv_w[l]
        conv_b = ffn_conv_b[l].reshape(1, -1)
        dec = jnp.broadcast_to(ret_decay[l].reshape(2, n_heads, 1, 1), (2, n_heads, 1, LANES))
        rng = ret_norm_g[l].reshape(1, dm)
        dskip = s5_d[l].reshape(1, dm)
        toep, wd, wo5, av = _s5_prep(s5_lam_re[l], s5_lam_im[l], s5_log_dt[l], s5_b_re[l], s5_b_im[l],
                                     s5_c_re[l], s5_c_im[l])

        for grp in range(2):
            ctx = grp == 0
            x, h = xres[grp], hres[grp]
            nbat, slen = (bsz, seq) if ctx else (dbsz, dseq)
            m = nbat * slen
            mods, rpm = group_mods(l, ctx)

            zg, zu, zs = _inproj(h, w_a, l, 3 * d, dm)

            nseg = slen // S5_SEG
            npb = nbat * nseg
            u = zu.reshape(npb, S5_SEG, dm).transpose(1, 0, 2)
            u4 = u.reshape(S5_SEG // S5_T, S5_T, npb, dm)
            if ctx:
                y4, fin = _s5(u4, toep, wd, wo5, av, dskip, None)
                fin = fin.reshape(nb5, npb, 4, gb5, p5).transpose(1, 2, 0, 3, 4).reshape(npb, 4, g5, p5)
                outs["s5f"].append(jnp.stack([fin[:, 0], fin[:, 1]], axis=-1))
                outs["s5b"].append(jnp.stack([fin[:, 2], fin[:, 3]], axis=-1))
            else:
                s0 = jnp.stack([state_s5_fwd[:, l, :, :, 0], state_s5_fwd[:, l, :, :, 1],
                                state_s5_bwd[:, l, :, :, 0], state_s5_bwd[:, l, :, :, 1]], axis=1)
                s0 = s0.reshape(dbsz, 4, nb5, gb5 * p5).transpose(2, 0, 1, 3)
                y4 = _s5(u4, toep, wd, wo5, av, dskip, s0)
            ya_tm = _glu(y4.reshape(S5_SEG * npb, dm), glu_w, l, glu_b)
            ya = ya_tm.reshape(S5_SEG, npb, dm).transpose(1, 0, 2).reshape(m, dm)

            zs3 = zs.reshape(nbat, slen, -1)
            if ctx:
                yb, rf, rb = _retention(zs3, 0, dec, rng, None, None, n_heads)
                outs["retf"].append(rf)
                outs["retb"].append(rb)
            else:
                yb = _retention(zs3, 0, dec, rng, state_ret_fwd[:, l], state_ret_bwd[:, l], n_heads)
            yb = yb.reshape(m, dm)

            if ctx:
                q, k, v, ckv, kr = _mla_proj(h, w_m[l], qg, kvg, wq_ext[l], wkv[l], ones_tab, zeros_tab,
                                             n_heads, qscale)
                outs["ckv"].append(ckv.reshape(nbat, slen, kv_lora))
                outs["krope"].append(kr[:, :dr].reshape(nbat, slen, dr))
                k, v = k.reshape(nbat, slen, -1), v.reshape(nbat, slen, -1)
            else:
                q, k, v, _, _ = _mla_proj(h, w_m[l], qg, kvg, wq_ext[l], wkv[l], cos_l, sin_l, n_heads, qscale)
                past = cache_mla_ckv.shape[2]
                kr_pad = jnp.pad(cache_mla_krope[:, l], ((0, 0), (0, 0), (0, LANES - dr)))
                kc, vc = _cache_kv(cache_mla_ckv[:, l].reshape(nbat * past, kv_lora),
                                   kr_pad.reshape(nbat * past, LANES), wkv[l], n_heads)
                k = jnp.concatenate([k.reshape(nbat, slen, -1), kc.reshape(nbat, past, -1)], axis=1)
                v = jnp.concatenate([v.reshape(nbat, slen, -1), vc.reshape(nbat, past, -1)], axis=1)
            yc = _attention(q.reshape(nbat, slen, -1), k, v, n_heads).reshape(m, dm)

            x, h = _merge_out(ya, yb, yc, zg, wb, wo_, l, x, mods, ng[1], ng[2], rpm)
            act = _ffn_up(h, wup, l, conv_w, conv_b, slen)
            if l + 1 < depth:
                x, h = _mm_res(act, wdn, l, x, mods, 5, ng[3], rpm,
                               nxt=(norm_g[l + 1, 0].reshape(1, d), group_mods(l + 1, ctx)[0], 0, 1))
            else:
                x, h = _mm_res(act, wdn, l, x, mods, 5, ng[3], rpm), None
            xres[grp], hres[grp] = x, h

    return (xres[0].reshape(bsz, seq, d), xres[1].reshape(dbsz, dseq, d),
            jnp.stack(outs["ckv"], axis=1), jnp.stack(outs["krope"], axis=1),
            jnp.stack(outs["retf"], axis=1), jnp.stack(outs["retb"], axis=1),
            jnp.stack(outs["s5f"], axis=1), jnp.stack(outs["s5b"], axis=1))
```

```python
import functools

import numpy as np
import jax
import jax.numpy as jnp
from jax import lax
from jax.experimental import pallas as pl
from jax.experimental.pallas import tpu as pltpu

F32 = jnp.float32
BF = jnp.bfloat16

NORM_EPS = 1e-6
ROPE_BASE = 10000.0
GRID_W = 64
LANES = 128
S5_T = 8
S5_SEG = 256
RET_CHUNK_MAX = 512
VMEM_LIMIT = 56 << 20


def _cparams(sem):
    return pltpu.CompilerParams(dimension_semantics=sem, vmem_limit_bytes=VMEM_LIMIT)


def _dot(a, b):
    return jnp.dot(a, b, preferred_element_type=F32)


def _dot_nt(a, b, precision=None):
    return lax.dot_general(a, b, (((1,), (1,)), ((), ())), precision=precision, preferred_element_type=F32)


def _rms(x, g):
    return x * lax.rsqrt(jnp.mean(x * x, axis=-1, keepdims=True) + NORM_EPS) * g


def _rms_mod(x, g, sc, sh):
    return _rms(x, g) * (1.0 + sc) + sh


def _sigmoid(x):
    return 1.0 / (1.0 + jnp.exp(-x))


def _mod_spec(chunk, d, rows_per_mod, tm):
    return pl.BlockSpec((1, 1, d), lambda i, *_: ((i * tm) // rows_per_mod, 0, chunk))


def _ada_kernel(c_ref, w_ref, b_ref, o_ref):
    c = c_ref[...]
    s = (c * _sigmoid(c)).astype(BF)
    o_ref[0] = _dot(s, w_ref[0].astype(BF)) + b_ref[0]


def _ada_mods(cvecs, ada_w, ada_b):
    depth, d, n = ada_w.shape
    tn = 1024
    return pl.pallas_call(
        _ada_kernel,
        grid=(depth, n // tn),
        in_specs=[pl.BlockSpec((8, d), lambda l, j: (0, 0)),
                  pl.BlockSpec((1, d, tn), lambda l, j: (l, 0, j)),
                  pl.BlockSpec((1, 1, tn), lambda l, j: (l, 0, j))],
        out_specs=pl.BlockSpec((1, 8, tn), lambda l, j: (l, 0, j)),
        out_shape=jax.ShapeDtypeStruct((depth, 8, n), F32),
        compiler_params=_cparams(("parallel", "parallel")),
        name="ada_mods",
    )(cvecs, ada_w, ada_b.reshape(depth, 1, n))


def _norm_mod_kernel(x_ref, g_ref, sh_ref, sc_ref, h_ref):
    h_ref[...] = _rms_mod(x_ref[...], g_ref[...], sc_ref[0], sh_ref[0]).astype(BF)


def _norm_mod(x, mods, ng0, rows_per_mod):
    m, d = x.shape
    tm = min(1024, m)
    return pl.pallas_call(
        _norm_mod_kernel,
        grid=(m // tm,),
        in_specs=[pl.BlockSpec((tm, d), lambda i: (i, 0)),
                  pl.BlockSpec((1, d), lambda i: (0, 0)),
                  _mod_spec(0, d, rows_per_mod, tm),
                  _mod_spec(1, d, rows_per_mod, tm)],
        out_specs=pl.BlockSpec((tm, d), lambda i: (i, 0)),
        out_shape=jax.ShapeDtypeStruct((m, d), BF),
        compiler_params=_cparams(("parallel",)),
        name="norm_mod",
    )(x, ng0, mods, mods)


def _inproj_kernel(h_ref, w_ref, og_ref, ou_ref, os_ref, *, n_gate_tiles, n_u_tiles):
    j = pl.program_id(1)
    r = _dot(h_ref[...], w_ref[...])

    @pl.when(j < n_gate_tiles)
    def _():
        og_ref[...] = r.astype(BF)

    @pl.when((j >= n_gate_tiles) & (j < n_gate_tiles + n_u_tiles))
    def _():
        ou_ref[...] = r

    @pl.when(j >= n_gate_tiles + n_u_tiles)
    def _():
        os_ref[...] = r


def _inproj(h, w_a, layer, n_gate, n_u):
    m, d = h.shape
    n = w_a.shape[-1]
    tm, tn = min(1024, m), 1024
    ngt, nut = n_gate // tn, n_u // tn
    nst = (n - n_gate - n_u) // tn
    return pl.pallas_call(
        functools.partial(_inproj_kernel, n_gate_tiles=ngt, n_u_tiles=nut),
        grid=(m // tm, ngt + nut + nst),
        in_specs=[pl.BlockSpec((tm, d), lambda i, j: (i, 0)),
                  pl.BlockSpec((None, d, tn), lambda i, j: (layer, 0, j))],
        out_specs=[pl.BlockSpec((tm, tn), lambda i, j: (i, jnp.minimum(j, ngt - 1))),
                   pl.BlockSpec((tm, tn), lambda i, j: (i, jnp.clip(j - ngt, 0, nut - 1))),
                   pl.BlockSpec((tm, tn), lambda i, j: (i, jnp.maximum(j - ngt - nut, 0)))],
        out_shape=[jax.ShapeDtypeStruct((m, n_gate), BF),
                   jax.ShapeDtypeStruct((m, n_u), F32),
                   jax.ShapeDtypeStruct((m, n - n_gate - n_u), F32)],
        compiler_params=_cparams(("parallel", "arbitrary")),
        name="in_proj",
    )(h, w_a)


def _rope(seg, cosp, sinp):
    return seg * cosp + pltpu.roll(seg, LANES // 2, 1) * sinp


def _ones_column(rows):
    lane = lax.broadcasted_iota(jnp.int32, (rows, LANES), 1)
    return jnp.where(lane == 0, 1.0, 0.0).astype(BF)


def _store_kv(kv, krot, k_ref, v_ref, n_heads):
    hv = n_heads * LANES
    ones = _ones_column(kv.shape[0])
    for hh in range(n_heads):
        b0 = hh * 2 * LANES
        k_ref[:, b0:b0 + LANES] = kv[:, hh * LANES:(hh + 1) * LANES].astype(BF)
        k_ref[:, b0 + LANES:b0 + 2 * LANES] = krot
        v_ref[:, b0:b0 + LANES] = kv[:, hv + hh * LANES:hv + (hh + 1) * LANES].astype(BF)
        v_ref[:, b0 + LANES:b0 + 2 * LANES] = ones


def _mla_proj_kernel(h_ref, wm_ref, qg_ref, kvg_ref, wq_ref, wkv_ref, cos_ref, sin_ref,
                     q_ref, k_ref, v_ref, ckv_ref, kr_ref, *, q_lora, kv_lora, n_heads, scale):
    z = _dot(h_ref[...], wm_ref[...])
    cosp, sinp = cos_ref[...], sin_ref[...]

    cqn = _rms(z[:, :q_lora], qg_ref[...]).astype(BF)
    qraw = _dot(cqn, wq_ref[...])
    for hh in range(n_heads):
        b0 = hh * 2 * LANES
        q_ref[:, b0:b0 + LANES] = (qraw[:, b0:b0 + LANES] * scale).astype(BF)
        q_ref[:, b0 + LANES:b0 + 2 * LANES] = (_rope(qraw[:, b0 + LANES:b0 + 2 * LANES], cosp, sinp) * scale).astype(BF)

    ckv = _rms(z[:, q_lora:q_lora + kv_lora], kvg_ref[...])
    ckv_ref[...] = ckv
    kv = _dot(ckv.astype(BF), wkv_ref[...])
    kr = z[:, q_lora + kv_lora:]
    kr_ref[...] = kr
    _store_kv(kv, _rope(kr, cosp, sinp).astype(BF), k_ref, v_ref, n_heads)


def _mla_proj(h, w_m, qg, kvg, wq_ext, wkv, cosp, sinp, n_heads, scale):
    m, d = h.shape
    q_lora, kv_lora = wq_ext.shape[0], wkv.shape[0]
    tm = min(256, m)
    tab_tiles = cosp.shape[0] // tm
    c2 = lambda i: (0, 0)
    return pl.pallas_call(
        functools.partial(_mla_proj_kernel, q_lora=q_lora, kv_lora=kv_lora, n_heads=n_heads, scale=scale),
        grid=(m // tm,),
        in_specs=[pl.BlockSpec((tm, d), lambda i: (i, 0)),
                  pl.BlockSpec(w_m.shape, c2),
                  pl.BlockSpec((1, q_lora), c2),
                  pl.BlockSpec((1, kv_lora), c2),
                  pl.BlockSpec(wq_ext.shape, c2),
                  pl.BlockSpec(wkv.shape, c2),
                  pl.BlockSpec((tm, LANES), lambda i: (i % tab_tiles, 0)),
                  pl.BlockSpec((tm, LANES), lambda i: (i % tab_tiles, 0))],
        out_specs=[pl.BlockSpec((tm, 2 * n_heads * LANES), lambda i: (i, 0)),
                   pl.BlockSpec((tm, 2 * n_heads * LANES), lambda i: (i, 0)),
                   pl.BlockSpec((tm, 2 * n_heads * LANES), lambda i: (i, 0)),
                   pl.BlockSpec((tm, kv_lora), lambda i: (i, 0)),
                   pl.BlockSpec((tm, LANES), lambda i: (i, 0))],
        out_shape=[jax.ShapeDtypeStruct((m, 2 * n_heads * LANES), BF),
                   jax.ShapeDtypeStruct((m, 2 * n_heads * LANES), BF),
                   jax.ShapeDtypeStruct((m, 2 * n_heads * LANES), BF),
                   jax.ShapeDtypeStruct((m, kv_lora), F32),
                   jax.ShapeDtypeStruct((m, LANES), F32)],
        compiler_params=_cparams(("parallel",)),
        name="mla_proj",
    )(h, w_m, qg, kvg, wq_ext, wkv, cosp, sinp)


def _cache_kv_kernel(ckv_ref, kr_ref, wkv_ref, k_ref, v_ref, *, n_heads):
    kv = _dot(ckv_ref[...].astype(BF), wkv_ref[...])
    _store_kv(kv, kr_ref[...].astype(BF), k_ref, v_ref, n_heads)


def _cache_kv(ckv, kr_pad, wkv, n_heads):
    rows, kvl = ckv.shape
    return pl.pallas_call(
        functools.partial(_cache_kv_kernel, n_heads=n_heads),
        grid=(1,),
        in_specs=[pl.BlockSpec((rows, kvl), lambda i: (0, 0)),
                  pl.BlockSpec((rows, LANES), lambda i: (0, 0)),
                  pl.BlockSpec(wkv.shape, lambda i: (0, 0))],
        out_specs=[pl.BlockSpec((rows, 2 * n_heads * LANES), lambda i: (0, 0)),
                   pl.BlockSpec((rows, 2 * n_heads * LANES), lambda i: (0, 0))],
        out_shape=[jax.ShapeDtypeStruct((rows, 2 * n_heads * LANES), BF),
                   jax.ShapeDtypeStruct((rows, 2 * n_heads * LANES), BF)],
        compiler_params=_cparams(("arbitrary",)),
        name="mla_cache_kv",
    )(ckv, kr_pad, wkv)


ATTN_ROWS = 256


def _softmax_pv(q, k, v):
    s = _dot_nt(q, k)
    p = jnp.exp2(s - jnp.max(s, axis=-1, keepdims=True)).astype(BF)
    acc = _dot(p, v)
    return (acc[:, :LANES] / acc[:, LANES:LANES + 1]).astype(BF)


def _attn_kernel(q_ref, k_ref, v_ref, o_ref):
    k, v = k_ref[0], v_ref[0]
    for r0 in range(0, q_ref.shape[1], ATTN_ROWS):
        o_ref[0, r0:r0 + ATTN_ROWS] = _softmax_pv(q_ref[0, r0:r0 + ATTN_ROWS], k, v)


def _attn_heads_kernel(q_ref, k_ref, v_ref, o_ref, *, n_heads):
    for hh in range(n_heads):
        hk = slice(hh * 2 * LANES, (hh + 1) * 2 * LANES)
        o_ref[0, :, hh * LANES:(hh + 1) * LANES] = _softmax_pv(q_ref[0, :, hk], k_ref[0, :, hk], v_ref[0, :, hk])


def _attention(q, k, v, n_heads):
    b, l, _ = q.shape
    s = k.shape[1]
    hw = 2 * LANES
    out_shape = jax.ShapeDtypeStruct((b, l, n_heads * LANES), BF)
    if l * n_heads <= 4096:
        return pl.pallas_call(
            functools.partial(_attn_heads_kernel, n_heads=n_heads),
            grid=(b,),
            in_specs=[pl.BlockSpec((1, l, n_heads * hw), lambda bi: (bi, 0, 0)),
                      pl.BlockSpec((1, s, n_heads * hw), lambda bi: (bi, 0, 0)),
                      pl.BlockSpec((1, s, n_heads * hw), lambda bi: (bi, 0, 0))],
            out_specs=pl.BlockSpec((1, l, n_heads * LANES), lambda bi: (bi, 0, 0)),
            out_shape=out_shape,
            compiler_params=_cparams(("parallel",)),
            name="mla_attention_ctx",
        )(q, k, v)
    tq = min(2048, l)
    return pl.pallas_call(
        _attn_kernel,
        grid=(b, n_heads, l // tq),
        in_specs=[pl.BlockSpec((1, tq, hw), lambda bi, h, i: (bi, i, h)),
                  pl.BlockSpec((1, s, hw), lambda bi, h, i: (bi, 0, h)),
                  pl.BlockSpec((1, s, hw), lambda bi, h, i: (bi, 0, h))],
        out_specs=pl.BlockSpec((1, tq, LANES), lambda bi, h, i: (bi, i, h)),
        out_shape=out_shape,
        compiler_params=_cparams(("parallel", "parallel", "arbitrary")),
        name="mla_attention",
    )(q, k, v)


def _ret_kernel(*refs, with_state, n_chunks, c_len, n_hp, k_scale):
    if with_state:
        (q_ref, k_ref, v_ref, g_ref, dec_ref, ng_ref, s0f_ref, s0b_ref,
         y_ref, dm_ref, kvf_ref, kvb_ref, sf_ref, sb_ref) = refs
    else:
        (q_ref, k_ref, v_ref, g_ref, dec_ref, ng_ref,
         y_ref, ff_ref, fb_ref, dm_ref, kvf_ref, kvb_ref, sf_ref, sb_ref) = refs
    reps = c_len // LANES

    @pl.when(pl.program_id(1) == 0)
    def _():
        di = (lax.broadcasted_iota(jnp.int32, (c_len, c_len), 0)
              - lax.broadcasted_iota(jnp.int32, (c_len, c_len), 1)).astype(F32)
        for hp in range(n_hp):
            lgf = jnp.tile(-jnp.exp(dec_ref[0, hp]), (1, reps))
            lgb = jnp.tile(-jnp.exp(dec_ref[1, hp]), (1, reps))
            dm_ref[hp] = jnp.where(di > 0, jnp.exp(lgf * jnp.maximum(di, 0.0)),
                                   jnp.where(di < 0, jnp.exp(lgb * jnp.maximum(-di, 0.0)), 2.0))

    ii = lax.broadcasted_iota(jnp.int32, (c_len, LANES), 0).astype(F32)
    need_state = with_state or n_chunks > 1
    for hp in range(n_hp):
        cs = slice(hp * LANES, (hp + 1) * LANES)
        lgf = -jnp.exp(dec_ref[0, hp])
        lgb = -jnp.exp(dec_ref[1, hp])
        dk_f = jnp.exp(lgf * (c_len - 1.0 - ii))
        dk_b = jnp.exp(lgb * ii)
        ds_f = jnp.exp(lgf * c_len)
        ds_b = jnp.exp(lgb * c_len)

        for c in range(n_chunks):
            rows = slice(c * c_len, (c + 1) * c_len)
            kc = k_ref[0, rows, cs] * k_scale
            vc = v_ref[0, rows, cs].astype(BF)
            kvf_ref[c] = _dot((kc * dk_f).T.astype(BF), vc)
            kvb_ref[c] = _dot((kc * dk_b).T.astype(BF), vc)

        if with_state:
            s_f, s_b = s0f_ref[0, hp], s0b_ref[0, hp]
        else:
            s_f = s_b = jnp.zeros((LANES, LANES), F32)
        for c in range(n_chunks):
            sf_ref[c] = s_f
            s_f = ds_f * s_f + kvf_ref[c]
        for c in reversed(range(n_chunks)):
            sb_ref[c] = s_b
            s_b = ds_b * s_b + kvb_ref[c]
        if not with_state:
            ff_ref[0, hp] = s_f
            fb_ref[0, hp] = s_b

        ng = ng_ref[:, cs]
        dq_f = jnp.exp(lgf * (ii + 1.0))
        dq_b = jnp.exp(lgb * (c_len - ii))
        for c in range(n_chunks):
            rows = slice(c * c_len, (c + 1) * c_len)
            qc = q_ref[0, rows, cs]
            kc = (k_ref[0, rows, cs] * k_scale).astype(BF)
            vc = v_ref[0, rows, cs].astype(BF)
            att = _dot_nt(qc.astype(BF), kc) * dm_ref[hp]
            o = _dot(att.astype(BF), vc)
            if need_state:
                o = (o + _dot((qc * dq_f).astype(BF), sf_ref[c].astype(BF))
                     + _dot((qc * dq_b).astype(BF), sb_ref[c].astype(BF)))
            mu = jnp.mean(o, axis=-1, keepdims=True)
            oc = o - mu
            var = jnp.mean(oc * oc, axis=-1, keepdims=True)
            on = oc * lax.rsqrt(var + NORM_EPS)
            gg = g_ref[0, rows, cs]
            y_ref[0, rows, cs] = (gg * _sigmoid(gg) * (on * ng)).astype(BF)


def _retention(zs, col0, dec, ng, s0f, s0b, n_heads):
    b, l, _ = zs.shape
    c_len = min(RET_CHUNK_MAX, l)
    nc = l // c_len
    with_state = s0f is not None
    n_hp = 1 if nc > 1 else n_heads
    hw = n_hp * LANES
    cb = col0 // hw

    def zspec(seg):
        return pl.BlockSpec((1, l, hw), lambda hg, bi: (bi, 0, cb + seg * (n_heads // n_hp) + hg))

    st_spec = pl.BlockSpec((1, n_hp, LANES, LANES), lambda hg, bi: (bi, hg, 0, 0))
    in_specs = [zspec(0), zspec(1), zspec(2), zspec(3),
                pl.BlockSpec((2, n_hp, 1, LANES), lambda hg, bi: (0, hg, 0, 0)),
                pl.BlockSpec((1, hw), lambda hg, bi: (0, hg))]
    args = [zs, zs, zs, zs, dec, ng]
    y_spec = pl.BlockSpec((1, l, hw), lambda hg, bi: (bi, 0, hg))
    y_shape = jax.ShapeDtypeStruct((b, l, n_heads * LANES), BF)
    if with_state:
        in_specs += [st_spec, st_spec]
        args += [s0f, s0b]
        out_specs, out_shape = y_spec, y_shape
    else:
        st_shape = jax.ShapeDtypeStruct((b, n_heads, LANES, LANES), F32)
        out_specs, out_shape = [y_spec, st_spec, st_spec], [y_shape, st_shape, st_shape]
    scratch = ([pltpu.VMEM((n_hp, c_len, c_len), F32)]
               + [pltpu.VMEM((nc, LANES, LANES), F32) for _ in range(4)])
    return pl.pallas_call(
        functools.partial(_ret_kernel, with_state=with_state, n_chunks=nc, c_len=c_len, n_hp=n_hp,
                          k_scale=float(LANES) ** -0.5),
        grid=(n_heads // n_hp, b),
        in_specs=in_specs, out_specs=out_specs, out_shape=out_shape,
        scratch_shapes=scratch,
        compiler_params=_cparams(("arbitrary", "arbitrary")),
        name="retention",
    )(*args)


def _s5_prep_kernel(lre_ref, lim_ref, ldt_ref, btre_ref, btim_ref, cre_ref, cim_ref,
                    toep_ref, wd_ref, wo_ref, av_ref, *, ch, p, seg_chunks):
    t = S5_T
    gw = (LANES // ch) * p
    btre, btim, cre, cim = btre_ref[...], btim_ref[...], cre_ref[...], cim_ref[...]
    r1 = lax.broadcasted_iota(jnp.int32, (LANES, LANES), 0)
    c1 = lax.broadcasted_iota(jnp.int32, (LANES, LANES), 1)
    same_group = (r1 // ch) == (c1 // ch)
    lane_valid = c1 < p
    r5 = lax.broadcasted_iota(jnp.int32, (LANES, gw), 0)
    c5 = lax.broadcasted_iota(jnp.int32, (LANES, gw), 1)
    own_state = (r5 // ch) == (c5 // p)

    def spread(x):
        x2 = x + pltpu.roll(x, LANES // 2, 1)
        return jnp.where(own_state, jnp.concatenate([x2] * (gw // LANES), axis=1), 0.0)

    xs, ys, lags = [], [], []
    for d in range(2):
        lr, li, dt = lre_ref[d], lim_ref[d], jnp.exp(ldt_ref[d])
        ar, ai = lr * dt, li * dt
        mag = jnp.exp(ar)
        abr, abi = mag * jnp.cos(ai), mag * jnp.sin(ai)
        nr, ni = abr - 1.0, abi
        den = lr * lr + li * li
        cr = (nr * lr + ni * li) / den
        ci = (ni * lr - nr * li) / den
        pr, pi = jnp.ones_like(lr), jnp.zeros_like(lr)
        xd, yd, gd = [], [], []
        for kk in range(t + 1):
            if kk < t:
                wr, wi = cr * pr - ci * pi, cr * pi + ci * pr
                xr, xi = wr * btre - wi * btim, wr * btim + wi * btre
                xd.append((xr, xi))
                hi = lax.Precision.HIGHEST
                gd.append(jnp.where(same_group, _dot_nt(xr, cre, hi) - _dot_nt(xi, cim, hi), 0.0))
            if kk >= 1:
                yd.append((cre * pr - cim * pi, -(cre * pi + cim * pr)))
            if kk == t:
                qr, qi = jnp.where(lane_valid, pr, 0.0), jnp.where(lane_valid, pi, 0.0)
            pr, pi = pr * abr - pi * abi, pr * abi + pi * abr
        xs.append(xd)
        ys.append(yd)
        lags.append(gd)
        inv = 1.0 / ch
        av_ref[0, 2 * d:2 * d + 1] = jnp.sum(spread(qr), axis=0, keepdims=True) * inv
        av_ref[0, 2 * d + 1:2 * d + 2] = jnp.sum(spread(qi), axis=0, keepdims=True) * inv
        for _ in range(seg_chunks.bit_length() - 1):
            qr, qi = qr * qr - qi * qi, 2.0 * qr * qi
        av_ref[0, 4 + 2 * d:5 + 2 * d] = jnp.sum(spread(qr), axis=0, keepdims=True) * inv
        av_ref[0, 5 + 2 * d:6 + 2 * d] = jnp.sum(spread(qi), axis=0, keepdims=True) * inv

    for ti in range(t):
        for to in range(t):
            lag = to - ti
            tile = lags[0][lag] if lag > 0 else (lags[1][-lag] if lag < 0 else lags[0][0] + lags[1][0])
            toep_ref[0, ti * LANES:(ti + 1) * LANES, to * LANES:(to + 1) * LANES] = tile.astype(BF)
    for tt in range(t):
        rows = slice(tt * LANES, (tt + 1) * LANES)
        xf, xb = xs[0][t - 1 - tt], xs[1][tt]
        yf, yb = ys[0][tt], ys[1][t - 1 - tt]
        for m, (xv, yv) in enumerate(((xf[0], yf[0]), (xf[1], yf[1]), (xb[0], yb[0]), (xb[1], yb[1]))):
            wd_ref[0, rows, m * gw:(m + 1) * gw] = spread(xv).astype(BF)
            wo_ref[0, m * gw:(m + 1) * gw, rows] = spread(yv).T.astype(BF)


def _s5_prep(lam_re, lam_im, log_dt, b_re, b_im, c_re, c_im):
    _, g, p = lam_re.shape
    ch = b_re.shape[-1]
    assert p == LANES // 2 and LANES % ch == 0
    nb = g * ch // LANES
    gw = (LANES // ch) * p
    t = S5_T
    seg_chunks = S5_SEG // t
    assert seg_chunks & (seg_chunks - 1) == 0

    def prow(a, padval):
        return jnp.pad(jnp.repeat(a, ch, axis=1), ((0, 0), (0, 0), (0, LANES - p)), constant_values=padval)

    def wrow(a):
        return jnp.pad(a.reshape(g * ch, p), ((0, 0), (0, LANES - p)))

    lre = prow(lam_re, -0.5)
    lim = prow(lam_im, 0.0)
    ldt = prow(jnp.broadcast_to(log_dt[..., None], lam_re.shape), 0.0)
    lspec = pl.BlockSpec((2, LANES, LANES), lambda i: (0, i, 0))
    wspec = pl.BlockSpec((LANES, LANES), lambda i: (i, 0))
    return pl.pallas_call(
        functools.partial(_s5_prep_kernel, ch=ch, p=p, seg_chunks=seg_chunks),
        grid=(nb,),
        in_specs=[lspec, lspec, lspec, wspec, wspec, wspec, wspec],
        out_specs=[pl.BlockSpec((1, t * LANES, t * LANES), lambda i: (i, 0, 0)),
                   pl.BlockSpec((1, t * LANES, 4 * gw), lambda i: (i, 0, 0)),
                   pl.BlockSpec((1, 4 * gw, t * LANES), lambda i: (i, 0, 0)),
                   pl.BlockSpec((1, 8, gw), lambda i: (i, 0, 0))],
        out_shape=[jax.ShapeDtypeStruct((nb, t * LANES, t * LANES), BF),
                   jax.ShapeDtypeStruct((nb, t * LANES, 4 * gw), BF),
                   jax.ShapeDtypeStruct((nb, 4 * gw, t * LANES), BF),
                   jax.ShapeDtypeStruct((nb, 8, gw), F32)],
        compiler_params=_cparams(("parallel",)),
        name="s5_prep",
    )(lre, lim, ldt, wrow(jnp.swapaxes(b_re, 1, 2)), wrow(jnp.swapaxes(b_im, 1, 2)), wrow(c_re), wrow(c_im))


def _gelu(x):
    return 0.5 * x * (1.0 + jnp.tanh(0.7978845608028654 * (x + 0.044715 * (x * x * x))))


def _s5_kernel(*refs, chain):
    if chain:
        (u_ref, toep_ref, wd_ref, wo_ref, a_ref, dsk_ref, s0_ref,
         y_ref, lhs_ref, d_ref, s_ref, xs_ref) = refs
    else:
        (u_ref, toep_ref, wd_ref, wo_ref, a_ref, dsk_ref,
         y_ref, fin_ref, lhs_ref, d_ref, s_ref) = refs
    nc, t, pb, _ = u_ref.shape
    rows = nc * pb
    w = a_ref.shape[-1]
    for tt in range(t):
        lhs_ref[:, tt * LANES:(tt + 1) * LANES] = u_ref[:, tt].reshape(rows, LANES).astype(BF)
    d_ref[...] = _dot(lhs_ref[...], wd_ref[0])
    av = a_ref[0]
    afr, afi, abr, abi = av[0:1], av[1:2], av[2:3], av[3:4]

    def scan(init, store):
        sfr, sfi, sbr, sbi = init
        for c in range(nc):
            rf = slice(c * pb, (c + 1) * pb)
            rb = slice((nc - 1 - c) * pb, (nc - c) * pb)
            if store:
                s_ref[rf, 0:w] = sfr.astype(BF)
                s_ref[rf, w:2 * w] = sfi.astype(BF)
                s_ref[rb, 2 * w:3 * w] = sbr.astype(BF)
                s_ref[rb, 3 * w:4 * w] = sbi.astype(BF)
            sfr, sfi = (afr * sfr - afi * sfi + d_ref[rf, 0:w],
                        afr * sfi + afi * sfr + d_ref[rf, w:2 * w])
            sbr, sbi = (abr * sbr - abi * sbi + d_ref[rb, 2 * w:3 * w],
                        abr * sbi + abi * sbr + d_ref[rb, 3 * w:4 * w])
        return sfr, sfi, sbr, sbi

    zero = jnp.zeros((pb, w), F32)
    if not chain:
        fin = scan((zero, zero, zero, zero), True)
        for m in range(4):
            fin_ref[0, :, m * w:(m + 1) * w] = fin[m]
    else:
        efr, efi, ebr, ebi = scan((zero, zero, zero, zero), False)
        s0 = s0_ref[0, 0]
        xr, xi = s0[0:1], s0[1:2]
        for j in range(pb):
            xs_ref[j:j + 1, 0:w] = xr
            xs_ref[j:j + 1, w:2 * w] = xi
            xr, xi = (av[4:5] * xr - av[5:6] * xi + efr[j:j + 1],
                      av[4:5] * xi + av[5:6] * xr + efi[j:j + 1])
        xr, xi = s0[2:3], s0[3:4]
        for j in reversed(range(pb)):
            xs_ref[j:j + 1, 2 * w:3 * w] = xr
            xs_ref[j:j + 1, 3 * w:4 * w] = xi
            xr, xi = (av[6:7] * xr - av[7:8] * xi + ebr[j:j + 1],
                      av[6:7] * xi + av[7:8] * xr + ebi[j:j + 1])
        scan((xs_ref[:, 0:w], xs_ref[:, w:2 * w], xs_ref[:, 2 * w:3 * w], xs_ref[:, 3 * w:4 * w]), True)

    yv = _dot(lhs_ref[...], toep_ref[0]) + _dot(s_ref[...], wo_ref[0])
    dsk = dsk_ref[...]
    for tt in range(t):
        yt = yv[:, tt * LANES:(tt + 1) * LANES] + dsk * u_ref[:, tt].reshape(rows, LANES)
        y_ref[:, tt] = _gelu(yt).reshape(nc, pb, LANES)


def _s5(u4, toep, wd, wo, av, dskip, s0):
    nc, t, npb, nch = u4.shape
    nb = toep.shape[0]
    chain = s0 is not None
    pb = npb // s0.shape[1] if chain else min(16, npb)
    sw = wd.shape[-1]
    rows = nc * pb
    ublk = pl.BlockSpec((nc, t, pb, LANES), lambda kb, hh: (0, 0, hh, kb))

    def wspec(arr):
        return pl.BlockSpec((1,) + arr.shape[1:], lambda kb, hh: (kb, 0, 0))

    in_specs = [ublk, wspec(toep), wspec(wd), wspec(wo), wspec(av),
                pl.BlockSpec((1, LANES), lambda kb, hh: (0, kb))]
    args = [u4, toep, wd, wo, av, dskip]
    scratch = [pltpu.VMEM((rows, t * LANES), BF), pltpu.VMEM((rows, sw), F32), pltpu.VMEM((rows, sw), BF)]
    if chain:
        in_specs += [pl.BlockSpec((1, 1, 4, sw // 4), lambda kb, hh: (kb, hh, 0, 0))]
        args += [s0]
        out_specs = ublk
        out_shape = jax.ShapeDtypeStruct(u4.shape, F32)
        scratch.append(pltpu.VMEM((pb, sw), F32))
    else:
        out_specs = [ublk, pl.BlockSpec((1, pb, sw), lambda kb, hh: (kb, hh, 0))]
        out_shape = [jax.ShapeDtypeStruct(u4.shape, F32), jax.ShapeDtypeStruct((nb, npb, sw), F32)]
    return pl.pallas_call(
        functools.partial(_s5_kernel, chain=chain),
        grid=(nb, npb // pb),
        in_specs=in_specs, out_specs=out_specs, out_shape=out_shape,
        scratch_shapes=scratch,
        compiler_params=_cparams(("parallel", "parallel")),
        name="s5_chunked",
    )(*args)


def _glu_kernel(y_ref, w_ref, b_ref, o_ref):
    y = y_ref[...]
    o_ref[...] = (y * _sigmoid(_dot(y.astype(BF), w_ref[...]) + b_ref[...])).astype(BF)


def _glu(y, w, layer, b):
    m, n = y.shape
    tm = min(1024, m)
    return pl.pallas_call(
        _glu_kernel,
        grid=(m // tm,),
        in_specs=[pl.BlockSpec((tm, n), lambda i: (i, 0)),
                  pl.BlockSpec((None, n, n), lambda i: (layer, 0, 0)),
                  pl.BlockSpec((1, n), lambda i: (0, 0))],
        out_specs=pl.BlockSpec((tm, n), lambda i: (i, 0)),
        out_shape=jax.ShapeDtypeStruct((m, n), BF),
        compiler_params=_cparams(("parallel",)),
        name="s5_glu",
    )(y, w, b)


def _accumulate(acc_ref, part, step):
    @pl.when(step == 0)
    def _():
        acc_ref[...] = part

    @pl.when(step > 0)
    def _():
        acc_ref[...] += part


def _res_epilogue(acc_ref, x_ref, gate_ref, ng_ref, o_ref, nxt_refs):
    gn = gate_ref[0] * ng_ref[...]
    if nxt_refs is not None:
        ng2_ref, sh2_ref, sc2_ref, h_ref = nxt_refs
        g2 = ng2_ref[...] * (1.0 + sc2_ref[0])
        sh2 = sh2_ref[0]
    rows = 128
    for r0 in range(0, o_ref.shape[0], rows):
        acc = acc_ref[r0:r0 + rows]
        inv = lax.rsqrt(jnp.mean(acc * acc, axis=-1, keepdims=True) + NORM_EPS)
        xn = x_ref[r0:r0 + rows] + (acc * inv) * gn
        o_ref[r0:r0 + rows] = xn
        if nxt_refs is not None:
            inv2 = lax.rsqrt(jnp.mean(xn * xn, axis=-1, keepdims=True) + NORM_EPS)
            h_ref[r0:r0 + rows] = ((xn * inv2) * g2 + sh2).astype(BF)


def _merge_kernel(ya_ref, yb_ref, yc_ref, ga_ref, gb_ref, gc_ref, wb_ref, o_ref):
    acc = (_sigmoid(ga_ref[...].astype(F32)) * _dot(ya_ref[...], wb_ref[0])
           + _sigmoid(gb_ref[...].astype(F32)) * _dot(yb_ref[...], wb_ref[1])
           + _sigmoid(gc_ref[...].astype(F32)) * _dot(yc_ref[...], wb_ref[2]))
    o_ref[...] = acc.astype(BF)


def _merge(ya, yb, yc, zg, wb, layer):
    m, dm = ya.shape
    d = wb.shape[-1]
    tm, tn = min(1024, m), 512
    nt = d // tn
    ysp = pl.BlockSpec((tm, dm), lambda i, j: (i, 0))

    def gsp(br):
        return pl.BlockSpec((tm, tn), lambda i, j: (i, j + br * nt))

    return pl.pallas_call(
        _merge_kernel,
        grid=(m // tm, nt),
        in_specs=[ysp, ysp, ysp, gsp(0), gsp(1), gsp(2),
                  pl.BlockSpec((None, 3, dm, tn), lambda i, j: (layer, 0, 0, j))],
        out_specs=pl.BlockSpec((tm, tn), lambda i, j: (i, j)),
        out_shape=jax.ShapeDtypeStruct((m, d), BF),
        compiler_params=_cparams(("parallel", "parallel")),
        name="branch_merge",
    )(ya, yb, yc, zg, zg, zg, wb)


def _mm_res_kernel(*refs, with_next):
    if with_next:
        a_ref, w_ref, x_ref, gate_ref, ng_ref, ng2_ref, sh2_ref, sc2_ref, o_ref, h_ref = refs
        nxt_refs = (ng2_ref, sh2_ref, sc2_ref, h_ref)
    else:
        a_ref, w_ref, x_ref, gate_ref, ng_ref, o_ref = refs
        nxt_refs = None
    k = pl.program_id(1)
    _accumulate(o_ref, _dot(a_ref[...], w_ref[...]), k)

    @pl.when(k == pl.num_programs(1) - 1)
    def _():
        _res_epilogue(o_ref, x_ref, gate_ref, ng_ref, o_ref, nxt_refs)


def _mm_res(a, w, layer, x, mods, gate_chunk, ng, rows_per_mod, nxt=None):
    m, kdim = a.shape
    d = w.shape[-1]
    tall = kdim > 2048
    tm = min(1024 if tall else 512, m)
    tk = 512 if tall else kdim // 2
    assert kdim % tk == 0
    in_specs = [pl.BlockSpec((tm, tk), lambda i, k: (i, k)),
                pl.BlockSpec((None, tk, d), lambda i, k: (layer, k, 0)),
                pl.BlockSpec((tm, d), lambda i, k: (i, 0), pipeline_mode=pl.Buffered(1 if tall else 2)),
                _mod_spec(gate_chunk, d, rows_per_mod, tm),
                pl.BlockSpec((1, d), lambda i, k: (0, 0))]
    args = [a, w, x, mods, ng]
    x_spec = pl.BlockSpec((tm, d), lambda i, k: (i, 0))
    x_shape = jax.ShapeDtypeStruct((m, d), F32)
    if nxt is None:
        out_specs, out_shape = x_spec, x_shape
    else:
        ng2, mods2, sh_chunk, sc_chunk = nxt
        in_specs += [pl.BlockSpec((1, d), lambda i, k: (0, 0)),
                     _mod_spec(sh_chunk, d, rows_per_mod, tm),
                     _mod_spec(sc_chunk, d, rows_per_mod, tm)]
        args += [ng2, mods2, mods2]
        out_specs = [x_spec, pl.BlockSpec((tm, d), lambda i, k: (i, 0))]
        out_shape = [x_shape, jax.ShapeDtypeStruct((m, d), BF)]
    return pl.pallas_call(
        functools.partial(_mm_res_kernel, with_next=nxt is not None),
        grid=(m // tm, kdim // tk),
        in_specs=in_specs, out_specs=out_specs, out_shape=out_shape,
        compiler_params=_cparams(("parallel", "arbitrary")),
        name="matmul_norm_residual",
    )(*args)


FFN_HALO = 16


def _ffn_up_kernel(hp_ref, h_ref, hn_ref, wv_ref, wg_ref, cwv_ref, cwg_ref, cbv_ref, cbg_ref,
                   o_ref, hb_ref, uv_ref, ug_ref, *, tm, piece, seq_len):
    i, j = pl.program_id(0), pl.program_id(1)
    hl = FFN_HALO
    n_pieces = tm // piece
    stride = piece + hl

    @pl.when(j == 0)
    def _():
        zero = jnp.zeros(hp_ref.shape, BF)
        if n_pieces == 1 and seq_len > tm:
            hb_ref[0:hl] = jnp.where(lax.rem(i * tm, seq_len) != 0, hp_ref[...], zero)
            hb_ref[stride:stride + hl] = jnp.where(lax.rem((i + 1) * tm, seq_len) != 0, hn_ref[...], zero)
        else:
            for pc in range(n_pieces + 1):
                hb_ref[pc * stride:pc * stride + hl] = zero
        for pc in range(n_pieces):
            hb_ref[hl + pc * stride:hl + pc * stride + piece] = h_ref[pc * piece:(pc + 1) * piece]

    hb = hb_ref[...]
    ug_ref[...] = _dot(hb, wg_ref[...])
    uv_ref[...] = _dot(hb, wv_ref[...])
    cwv, cwg, cbv, cbg = cwv_ref[...], cwg_ref[...], cbv_ref[...], cbg_ref[...]
    for pc in range(n_pieces):
        b0 = hl + pc * stride

        def conv(u_ref, cw, cb):
            return (u_ref[b0 - 1:b0 - 1 + piece] * cw[0:1] + u_ref[b0:b0 + piece] * cw[1:2]
                    + u_ref[b0 + 1:b0 + 1 + piece] * cw[2:3] + cb)

        gate = conv(ug_ref, cwg, cbg)
        act = gate * _sigmoid(gate)
        o_ref[pc * piece:(pc + 1) * piece] = (act * conv(uv_ref, cwv, cbv)).astype(BF)


def _ffn_up(h, w_up, layer, conv_w, conv_b, seq_len):
    m, d = h.shape
    f = w_up.shape[-1] // 2
    tm, tf = min(1024, m), 512
    piece = min(seq_len, tm)
    assert tm % piece == 0 and seq_len % piece == 0
    nf = f // tf
    hl = FFN_HALO
    nblk = m // hl
    rows = (tm // piece) * (piece + hl) + hl

    return pl.pallas_call(
        functools.partial(_ffn_up_kernel, tm=tm, piece=piece, seq_len=seq_len),
        grid=(m // tm, nf),
        in_specs=[pl.BlockSpec((hl, d), lambda i, j: (jnp.maximum(i * (tm // hl) - 1, 0), 0)),
                  pl.BlockSpec((tm, d), lambda i, j: (i, 0)),
                  pl.BlockSpec((hl, d), lambda i, j: (jnp.minimum((i + 1) * (tm // hl), nblk - 1), 0)),
                  pl.BlockSpec((None, d, tf), lambda i, j: (layer, 0, j)),
                  pl.BlockSpec((None, d, tf), lambda i, j: (layer, 0, j + nf)),
                  pl.BlockSpec((3, tf), lambda i, j: (0, j)),
                  pl.BlockSpec((3, tf), lambda i, j: (0, j + nf)),
                  pl.BlockSpec((1, tf), lambda i, j: (0, j)),
                  pl.BlockSpec((1, tf), lambda i, j: (0, j + nf))],
        out_specs=pl.BlockSpec((tm, tf), lambda i, j: (i, j)),
        out_shape=jax.ShapeDtypeStruct((m, f), BF),
        scratch_shapes=[pltpu.VMEM((rows, d), BF), pltpu.VMEM((rows, tf), F32), pltpu.VMEM((rows, tf), F32)],
        compiler_params=_cparams(("parallel", "arbitrary")),
        name="ffn_up_conv",
    )(h, h, h, w_up, w_up, conv_w, conv_w, conv_b, conv_b)


def _pair_swap(w):
    r = w.reshape(w.shape[:-1] + (w.shape[-1] // 2, 2))
    return jnp.stack([-r[..., 1], r[..., 0]], axis=-1).reshape(w.shape)


def _rope_tables(n_tokens, dr):
    rows = n_tokens // GRID_W
    row = jnp.broadcast_to(jnp.arange(rows, dtype=F32)[:, None], (rows, GRID_W)).reshape(-1)
    col = jnp.broadcast_to(jnp.arange(GRID_W, dtype=F32)[None, :], (rows, GRID_W)).reshape(-1)
    nf = dr // 4
    inv = ROPE_BASE ** (-jnp.arange(nf, dtype=F32) / nf)
    ang = jnp.concatenate([row[:, None] * inv, col[:, None] * inv], axis=-1)
    ang = jnp.repeat(ang, 2, axis=-1)
    pad = jnp.zeros((n_tokens, LANES - dr), F32)
    return jnp.concatenate([jnp.cos(ang), pad], axis=-1), jnp.concatenate([jnp.sin(ang), pad], axis=-1)


def kernel(x_prompt, x_sample, cache_mla_ckv, cache_mla_krope, state_ret_fwd, state_ret_bwd, state_s5_fwd, state_s5_bwd, c, c_ctx, ada_w, ada_b, norm_g, w_in, s5_lam_re, s5_lam_im, s5_log_dt, s5_b_re, s5_b_im, s5_c_re, s5_c_im, s5_d, s5_glu_w, s5_glu_b, ret_decay, ret_norm_g, mla_q_norm, mla_kv_norm, mla_w_uq, mla_w_uk, mla_w_uv, w_branch, w_out, ffn_w_up, ffn_conv_w, ffn_conv_b, ffn_w_down):
    bsz, seq, d = x_prompt.shape
    dbsz, dseq, _ = x_sample.shape
    depth = ada_w.shape[0]
    dm = d // 2
    n_heads = ret_decay.shape[-1]
    dr = cache_mla_krope.shape[-1]
    q_lora, kv_lora = mla_w_uq.shape[1], mla_w_uk.shape[1]
    g5, p5, ch5 = s5_b_re.shape[1:]
    assert dm == n_heads * LANES and dr == LANES // 2 and mla_w_uk.shape[-1] == dm
    assert seq == S5_SEG and dseq % S5_SEG == 0
    qscale = float(LANES + dr) ** -0.5 * 1.4426950408889634

    cvecs = jnp.concatenate([c_ctx[None], c, jnp.zeros((8 - 1 - dbsz, d), F32)], axis=0)
    mods_all = _ada_mods(cvecs, ada_w, ada_b)

    cos_l, sin_l = _rope_tables(dseq, dr)
    ones_tab = jnp.concatenate([jnp.ones((256, dr), F32), jnp.zeros((256, LANES - dr), F32)], axis=-1)
    zeros_tab = jnp.zeros((256, LANES), F32)

    cw = np.cumsum([0, dm, dm, dm, dm, dm, q_lora, kv_lora, dr, d, d, d])
    w_a = jnp.concatenate([w_in[:, :, cw[8]:cw[11]], w_in[:, :, cw[0]:cw[5]]], axis=2).astype(BF)
    w_kr = w_in[:, :, cw[7]:cw[8]]
    w_m = jnp.concatenate([w_in[:, :, cw[5]:cw[7]], w_kr, _pair_swap(w_kr)], axis=2).astype(BF)
    uq = mla_w_uq.reshape(depth, q_lora, n_heads, LANES + dr)
    wq_ext = jnp.concatenate([uq[..., :LANES], uq[..., LANES:], _pair_swap(uq[..., LANES:])],
                             axis=-1).reshape(depth, q_lora, n_heads * 2 * LANES).astype(BF)
    wkv = jnp.concatenate([mla_w_uk, mla_w_uv], axis=2).astype(BF)
    glu_w = s5_glu_w.astype(BF)
    wb = w_branch.astype(BF)
    wo_ = w_out.astype(BF)
    wup = ffn_w_up.astype(BF)
    wdn = ffn_w_down.astype(BF)

    outs = {k: [] for k in ("ckv", "krope", "retf", "retb", "s5f", "s5b")}
    nb5 = g5 * ch5 // LANES
    gb5 = LANES // ch5

    def group_mods(l, ctx):
        if ctx:
            return mods_all[l, 0:1].reshape(1, 1, -1), bsz * seq
        return mods_all[l, 1:1 + dbsz].reshape(dbsz, 1, -1), dseq

    xres = [x_prompt.reshape(bsz * seq, d), x_sample.reshape(dbsz * dseq, d)]
    hres = []
    for grp in range(2):
        mods, rpm = group_mods(0, grp == 0)
        hres.append(_norm_mod(xres[grp], mods, norm_g[0, 0].reshape(1, d), rpm))

    for l in range(depth):
        qg = mla_q_norm[l].reshape(1, q_lora)
        kvg = mla_kv_norm[l].reshape(1, kv_lora)
        ng = norm_g[l].reshape(4, 1, d)
        glu_b = s5_glu_b[l].reshape(1, dm)
        conv_w = ffn_conv_w[l]
        conv_b = ffn_conv_b[l].reshape(1, -1)
        dec = jnp.broadcast_to(ret_decay[l].reshape(2, n_heads, 1, 1), (2, n_heads, 1, LANES))
        rng = ret_norm_g[l].reshape(1, dm)
        dskip = s5_d[l].reshape(1, dm)
        toep, wd, wo5, av = _s5_prep(s5_lam_re[l], s5_lam_im[l], s5_log_dt[l], s5_b_re[l], s5_b_im[l],
                                     s5_c_re[l], s5_c_im[l])

        for grp in range(2):
            ctx = grp == 0
            x, h = xres[grp], hres[grp]
            nbat, slen = (bsz, seq) if ctx else (dbsz, dseq)
            m = nbat * slen
            mods, rpm = group_mods(l, ctx)

            zg, zu, zs = _inproj(h, w_a, l, 3 * d, dm)

            nseg = slen // S5_SEG
            npb = nbat * nseg
            u = zu.reshape(npb, S5_SEG, dm).transpose(1, 0, 2)
            u4 = u.reshape(S5_SEG // S5_T, S5_T, npb, dm)
            if ctx:
                y4, fin = _s5(u4, toep, wd, wo5, av, dskip, None)
                fin = fin.reshape(nb5, npb, 4, gb5, p5).transpose(1, 2, 0, 3, 4).reshape(npb, 4, g5, p5)
                outs["s5f"].append(jnp.stack([fin[:, 0], fin[:, 1]], axis=-1))
                outs["s5b"].append(jnp.stack([fin[:, 2], fin[:, 3]], axis=-1))
            else:
                s0 = jnp.stack([state_s5_fwd[:, l, :, :, 0], state_s5_fwd[:, l, :, :, 1],
                                state_s5_bwd[:, l, :, :, 0], state_s5_bwd[:, l, :, :, 1]], axis=1)
                s0 = s0.reshape(dbsz, 4, nb5, gb5 * p5).transpose(2, 0, 1, 3)
                y4 = _s5(u4, toep, wd, wo5, av, dskip, s0)
            ya_tm = _glu(y4.reshape(S5_SEG * npb, dm), glu_w, l, glu_b)
            ya = ya_tm.reshape(S5_SEG, npb, dm).transpose(1, 0, 2).reshape(m, dm)

            zs3 = zs.reshape(nbat, slen, -1)
            if ctx:
                yb, rf, rb = _retention(zs3, 0, dec, rng, None, None, n_heads)
                outs["retf"].append(rf)
                outs["retb"].append(rb)
            else:
                yb = _retention(zs3, 0, dec, rng, state_ret_fwd[:, l], state_ret_bwd[:, l], n_heads)
            yb = yb.reshape(m, dm)

            if ctx:
                q, k, v, ckv, kr = _mla_proj(h, w_m[l], qg, kvg, wq_ext[l], wkv[l], ones_tab, zeros_tab,
                                             n_heads, qscale)
                outs["ckv"].append(ckv.reshape(nbat, slen, kv_lora))
                outs["krope"].append(kr[:, :dr].reshape(nbat, slen, dr))
                k, v = k.reshape(nbat, slen, -1), v.reshape(nbat, slen, -1)
            else:
                q, k, v, _, _ = _mla_proj(h, w_m[l], qg, kvg, wq_ext[l], wkv[l], cos_l, sin_l, n_heads, qscale)
                past = cache_mla_ckv.shape[2]
                kr_pad = jnp.pad(cache_mla_krope[:, l], ((0, 0), (0, 0), (0, LANES - dr)))
                kc, vc = _cache_kv(cache_mla_ckv[:, l].reshape(nbat * past, kv_lora),
                                   kr_pad.reshape(nbat * past, LANES), wkv[l], n_heads)
                k = jnp.concatenate([k.reshape(nbat, slen, -1), kc.reshape(nbat, past, -1)], axis=1)
                v = jnp.concatenate([v.reshape(nbat, slen, -1), vc.reshape(nbat, past, -1)], axis=1)
            yc = _attention(q.reshape(nbat, slen, -1), k, v, n_heads).reshape(m, dm)

            merged = _merge(ya, yb, yc, zg, wb, l)
            x, h = _mm_res(merged, wo_, l, x, mods, 2, ng[1], rpm, nxt=(ng[2], mods, 3, 4))
            act = _ffn_up(h, wup, l, conv_w, conv_b, slen)
            if l + 1 < depth:
                x, h = _mm_res(act, wdn, l, x, mods, 5, ng[3], rpm,
                               nxt=(norm_g[l + 1, 0].reshape(1, d), group_mods(l + 1, ctx)[0], 0, 1))
            else:
                x, h = _mm_res(act, wdn, l, x, mods, 5, ng[3], rpm), None
            xres[grp], hres[grp] = x, h

    return (xres[0].reshape(bsz, seq, d), xres[1].reshape(dbsz, dseq, d),
            jnp.stack(outs["ckv"], axis=1), jnp.stack(outs["krope"], axis=1),
            jnp.stack(outs["retf"], axis=1), jnp.stack(outs["retb"], axis=1),
            jnp.stack(outs["s5f"], axis=1), jnp.stack(outs["s5b"], axis=1))
```

```python
import functools

import numpy as np
import jax
import jax.numpy as jnp
from jax import lax
from jax.experimental import pallas as pl
from jax.experimental.pallas import tpu as pltpu

F32 = jnp.float32
BF = jnp.bfloat16

NORM_EPS = 1e-6
ROPE_BASE = 10000.0
GRID_W = 64
LANES = 128
S5_T = 8
S5_SEG = 256
RET_CHUNK_MAX = 512
VMEM_LIMIT = 56 << 20


def _cparams(sem):
    return pltpu.CompilerParams(dimension_semantics=sem, vmem_limit_bytes=VMEM_LIMIT)


def _dot(a, b):
    return jnp.dot(a, b, preferred_element_type=F32)


def _dot_nt(a, b, precision=None):
    return lax.dot_general(a, b, (((1,), (1,)), ((), ())), precision=precision, preferred_element_type=F32)


def _rms(x, g):
    return x * lax.rsqrt(jnp.mean(x * x, axis=-1, keepdims=True) + NORM_EPS) * g


def _rms_mod(x, g, sc, sh):
    return _rms(x, g) * (1.0 + sc) + sh


def _sigmoid(x):
    return 1.0 / (1.0 + jnp.exp(-x))


def _mod_spec(chunk, d, rows_per_mod, tm):
    return pl.BlockSpec((1, 1, d), lambda i, *_: ((i * tm) // rows_per_mod, 0, chunk))


def _ada_kernel(c_ref, w_ref, b_ref, o_ref):
    c = c_ref[...]
    s = (c * _sigmoid(c)).astype(BF)
    o_ref[0] = _dot(s, w_ref[0].astype(BF)) + b_ref[0]


def _ada_mods(cvecs, ada_w, ada_b):
    depth, d, n = ada_w.shape
    tn = 1024
    return pl.pallas_call(
        _ada_kernel,
        grid=(depth, n // tn),
        in_specs=[pl.BlockSpec((8, d), lambda l, j: (0, 0)),
                  pl.BlockSpec((1, d, tn), lambda l, j: (l, 0, j)),
                  pl.BlockSpec((1, 1, tn), lambda l, j: (l, 0, j))],
        out_specs=pl.BlockSpec((1, 8, tn), lambda l, j: (l, 0, j)),
        out_shape=jax.ShapeDtypeStruct((depth, 8, n), F32),
        compiler_params=_cparams(("parallel", "parallel")),
        name="ada_mods",
    )(cvecs, ada_w, ada_b.reshape(depth, 1, n))


def _norm_mod_kernel(x_ref, g_ref, sh_ref, sc_ref, h_ref):
    h_ref[...] = _rms_mod(x_ref[...], g_ref[...], sc_ref[0], sh_ref[0]).astype(BF)


def _norm_mod(x, mods, ng0, rows_per_mod):
    m, d = x.shape
    tm = min(1024, m)
    return pl.pallas_call(
        _norm_mod_kernel,
        grid=(m // tm,),
        in_specs=[pl.BlockSpec((tm, d), lambda i: (i, 0)),
                  pl.BlockSpec((1, d), lambda i: (0, 0)),
                  _mod_spec(0, d, rows_per_mod, tm),
                  _mod_spec(1, d, rows_per_mod, tm)],
        out_specs=pl.BlockSpec((tm, d), lambda i: (i, 0)),
        out_shape=jax.ShapeDtypeStruct((m, d), BF),
        compiler_params=_cparams(("parallel",)),
        name="norm_mod",
    )(x, ng0, mods, mods)


def _inproj_kernel(h_ref, w_ref, og_ref, ou_ref, os_ref, *, n_gate_tiles, n_u_tiles):
    j = pl.program_id(1)
    r = _dot(h_ref[...], w_ref[...])

    @pl.when(j < n_gate_tiles)
    def _():
        og_ref[...] = r.astype(BF)

    @pl.when((j >= n_gate_tiles) & (j < n_gate_tiles + n_u_tiles))
    def _():
        ou_ref[...] = r

    @pl.when(j >= n_gate_tiles + n_u_tiles)
    def _():
        os_ref[...] = r


def _inproj(h, w_a, layer, n_gate, n_u):
    m, d = h.shape
    n = w_a.shape[-1]
    tm, tn = min(1024, m), 1024
    ngt, nut = n_gate // tn, n_u // tn
    nst = (n - n_gate - n_u) // tn
    return pl.pallas_call(
        functools.partial(_inproj_kernel, n_gate_tiles=ngt, n_u_tiles=nut),
        grid=(m // tm, ngt + nut + nst),
        in_specs=[pl.BlockSpec((tm, d), lambda i, j: (i, 0)),
                  pl.BlockSpec((None, d, tn), lambda i, j: (layer, 0, j))],
        out_specs=[pl.BlockSpec((tm, tn), lambda i, j: (i, jnp.minimum(j, ngt - 1))),
                   pl.BlockSpec((tm, tn), lambda i, j: (i, jnp.clip(j - ngt, 0, nut - 1))),
                   pl.BlockSpec((tm, tn), lambda i, j: (i, jnp.maximum(j - ngt - nut, 0)))],
        out_shape=[jax.ShapeDtypeStruct((m, n_gate), BF),
                   jax.ShapeDtypeStruct((m, n_u), F32),
                   jax.ShapeDtypeStruct((m, n - n_gate - n_u), F32)],
        compiler_params=_cparams(("parallel", "arbitrary")),
        name="in_proj",
    )(h, w_a)


def _rope(seg, cosp, sinp):
    return seg * cosp + pltpu.roll(seg, LANES // 2, 1) * sinp


def _ones_column(rows):
    lane = lax.broadcasted_iota(jnp.int32, (rows, LANES), 1)
    return jnp.where(lane == 0, 1.0, 0.0).astype(BF)


def _store_kv(kv, krot, k_ref, v_ref, n_heads):
    hv = n_heads * LANES
    ones = _ones_column(kv.shape[0])
    for hh in range(n_heads):
        b0 = hh * 2 * LANES
        k_ref[:, b0:b0 + LANES] = kv[:, hh * LANES:(hh + 1) * LANES].astype(BF)
        k_ref[:, b0 + LANES:b0 + 2 * LANES] = krot
        v_ref[:, b0:b0 + LANES] = kv[:, hv + hh * LANES:hv + (hh + 1) * LANES].astype(BF)
        v_ref[:, b0 + LANES:b0 + 2 * LANES] = ones


def _mla_proj_kernel(h_ref, wm_ref, qg_ref, kvg_ref, wq_ref, wkv_ref, cos_ref, sin_ref,
                     q_ref, k_ref, v_ref, ckv_ref, kr_ref, *, q_lora, kv_lora, n_heads, scale):
    z = _dot(h_ref[...], wm_ref[...])
    cosp, sinp = cos_ref[...], sin_ref[...]

    cqn = _rms(z[:, :q_lora], qg_ref[...]).astype(BF)
    qraw = _dot(cqn, wq_ref[...])
    for hh in range(n_heads):
        b0 = hh * 2 * LANES
        q_ref[:, b0:b0 + LANES] = (qraw[:, b0:b0 + LANES] * scale).astype(BF)
        q_ref[:, b0 + LANES:b0 + 2 * LANES] = (_rope(qraw[:, b0 + LANES:b0 + 2 * LANES], cosp, sinp) * scale).astype(BF)

    ckv = _rms(z[:, q_lora:q_lora + kv_lora], kvg_ref[...])
    ckv_ref[...] = ckv
    kv = _dot(ckv.astype(BF), wkv_ref[...])
    kr = z[:, q_lora + kv_lora:]
    kr_ref[...] = kr
    _store_kv(kv, _rope(kr, cosp, sinp).astype(BF), k_ref, v_ref, n_heads)


def _mla_proj(h, w_m, qg, kvg, wq_ext, wkv, cosp, sinp, n_heads, scale):
    m, d = h.shape
    q_lora, kv_lora = wq_ext.shape[0], wkv.shape[0]
    tm = min(256, m)
    tab_tiles = cosp.shape[0] // tm
    c2 = lambda i: (0, 0)
    return pl.pallas_call(
        functools.partial(_mla_proj_kernel, q_lora=q_lora, kv_lora=kv_lora, n_heads=n_heads, scale=scale),
        grid=(m // tm,),
        in_specs=[pl.BlockSpec((tm, d), lambda i: (i, 0)),
                  pl.BlockSpec(w_m.shape, c2),
                  pl.BlockSpec((1, q_lora), c2),
                  pl.BlockSpec((1, kv_lora), c2),
                  pl.BlockSpec(wq_ext.shape, c2),
                  pl.BlockSpec(wkv.shape, c2),
                  pl.BlockSpec((tm, LANES), lambda i: (i % tab_tiles, 0)),
                  pl.BlockSpec((tm, LANES), lambda i: (i % tab_tiles, 0))],
        out_specs=[pl.BlockSpec((tm, 2 * n_heads * LANES), lambda i: (i, 0)),
                   pl.BlockSpec((tm, 2 * n_heads * LANES), lambda i: (i, 0)),
                   pl.BlockSpec((tm, 2 * n_heads * LANES), lambda i: (i, 0)),
                   pl.BlockSpec((tm, kv_lora), lambda i: (i, 0)),
                   pl.BlockSpec((tm, LANES), lambda i: (i, 0))],
        out_shape=[jax.ShapeDtypeStruct((m, 2 * n_heads * LANES), BF),
                   jax.ShapeDtypeStruct((m, 2 * n_heads * LANES), BF),
                   jax.ShapeDtypeStruct((m, 2 * n_heads * LANES), BF),
                   jax.ShapeDtypeStruct((m, kv_lora), F32),
                   jax.ShapeDtypeStruct((m, LANES), F32)],
        compiler_params=_cparams(("parallel",)),
        name="mla_proj",
    )(h, w_m, qg, kvg, wq_ext, wkv, cosp, sinp)


def _cache_kv_kernel(ckv_ref, kr_ref, wkv_ref, k_ref, v_ref, *, n_heads):
    kv = _dot(ckv_ref[...].astype(BF), wkv_ref[...])
    _store_kv(kv, kr_ref[...].astype(BF), k_ref, v_ref, n_heads)


def _cache_kv(ckv, kr_pad, wkv, n_heads):
    rows, kvl = ckv.shape
    return pl.pallas_call(
        functools.partial(_cache_kv_kernel, n_heads=n_heads),
        grid=(1,),
        in_specs=[pl.BlockSpec((rows, kvl), lambda i: (0, 0)),
                  pl.BlockSpec((rows, LANES), lambda i: (0, 0)),
                  pl.BlockSpec(wkv.shape, lambda i: (0, 0))],
        out_specs=[pl.BlockSpec((rows, 2 * n_heads * LANES), lambda i: (0, 0)),
                   pl.BlockSpec((rows, 2 * n_heads * LANES), lambda i: (0, 0))],
        out_shape=[jax.ShapeDtypeStruct((rows, 2 * n_heads * LANES), BF),
                   jax.ShapeDtypeStruct((rows, 2 * n_heads * LANES), BF)],
        compiler_params=_cparams(("arbitrary",)),
        name="mla_cache_kv",
    )(ckv, kr_pad, wkv)


ATTN_ROWS = 256


def _softmax_pv(q, k, v):
    s = _dot_nt(q, k)
    p = jnp.exp2(s - jnp.max(s, axis=-1, keepdims=True)).astype(BF)
    acc = _dot(p, v)
    return (acc[:, :LANES] / acc[:, LANES:LANES + 1]).astype(BF)


def _attn_kernel(q_ref, k_ref, v_ref, o_ref):
    k, v = k_ref[0], v_ref[0]
    for r0 in range(0, q_ref.shape[1], ATTN_ROWS):
        o_ref[0, r0:r0 + ATTN_ROWS] = _softmax_pv(q_ref[0, r0:r0 + ATTN_ROWS], k, v)


def _attn_heads_kernel(q_ref, k_ref, v_ref, o_ref, *, n_heads):
    for hh in range(n_heads):
        hk = slice(hh * 2 * LANES, (hh + 1) * 2 * LANES)
        o_ref[0, :, hh * LANES:(hh + 1) * LANES] = _softmax_pv(q_ref[0, :, hk], k_ref[0, :, hk], v_ref[0, :, hk])


def _attention(q, k, v, n_heads):
    b, l, _ = q.shape
    s = k.shape[1]
    hw = 2 * LANES
    out_shape = jax.ShapeDtypeStruct((b, l, n_heads * LANES), BF)
    if l * n_heads <= 4096:
        return pl.pallas_call(
            functools.partial(_attn_heads_kernel, n_heads=n_heads),
            grid=(b,),
            in_specs=[pl.BlockSpec((1, l, n_heads * hw), lambda bi: (bi, 0, 0)),
                      pl.BlockSpec((1, s, n_heads * hw), lambda bi: (bi, 0, 0)),
                      pl.BlockSpec((1, s, n_heads * hw), lambda bi: (bi, 0, 0))],
            out_specs=pl.BlockSpec((1, l, n_heads * LANES), lambda bi: (bi, 0, 0)),
            out_shape=out_shape,
            compiler_params=_cparams(("parallel",)),
            name="mla_attention_ctx",
        )(q, k, v)
    tq = min(2048, l)
    return pl.pallas_call(
        _attn_kernel,
        grid=(b, n_heads, l // tq),
        in_specs=[pl.BlockSpec((1, tq, hw), lambda bi, h, i: (bi, i, h)),
                  pl.BlockSpec((1, s, hw), lambda bi, h, i: (bi, 0, h)),
                  pl.BlockSpec((1, s, hw), lambda bi, h, i: (bi, 0, h))],
        out_specs=pl.BlockSpec((1, tq, LANES), lambda bi, h, i: (bi, i, h)),
        out_shape=out_shape,
        compiler_params=_cparams(("parallel", "parallel", "arbitrary")),
        name="mla_attention",
    )(q, k, v)


def _ret_kernel(*refs, with_state, n_chunks, c_len, n_hp, k_scale):
    if with_state:
        (q_ref, k_ref, v_ref, g_ref, dec_ref, ng_ref, s0f_ref, s0b_ref,
         y_ref, dm_ref, kvf_ref, kvb_ref, sf_ref, sb_ref) = refs
    else:
        (q_ref, k_ref, v_ref, g_ref, dec_ref, ng_ref,
         y_ref, ff_ref, fb_ref, dm_ref, kvf_ref, kvb_ref, sf_ref, sb_ref) = refs
    reps = c_len // LANES

    @pl.when(pl.program_id(1) == 0)
    def _():
        di = (lax.broadcasted_iota(jnp.int32, (c_len, c_len), 0)
              - lax.broadcasted_iota(jnp.int32, (c_len, c_len), 1)).astype(F32)
        for hp in range(n_hp):
            lgf = jnp.tile(-jnp.exp(dec_ref[0, hp]), (1, reps))
            lgb = jnp.tile(-jnp.exp(dec_ref[1, hp]), (1, reps))
            dm_ref[hp] = jnp.where(di > 0, jnp.exp(lgf * jnp.maximum(di, 0.0)),
                                   jnp.where(di < 0, jnp.exp(lgb * jnp.maximum(-di, 0.0)), 2.0))

    ii = lax.broadcasted_iota(jnp.int32, (c_len, LANES), 0).astype(F32)
    need_state = with_state or n_chunks > 1
    for hp in range(n_hp):
        cs = slice(hp * LANES, (hp + 1) * LANES)
        lgf = -jnp.exp(dec_ref[0, hp])
        lgb = -jnp.exp(dec_ref[1, hp])
        dk_f = jnp.exp(lgf * (c_len - 1.0 - ii))
        dk_b = jnp.exp(lgb * ii)
        ds_f = jnp.exp(lgf * c_len)
        ds_b = jnp.exp(lgb * c_len)

        for c in range(n_chunks):
            rows = slice(c * c_len, (c + 1) * c_len)
            kc = k_ref[0, rows, cs] * k_scale
            vc = v_ref[0, rows, cs].astype(BF)
            kvf_ref[c] = _dot((kc * dk_f).T.astype(BF), vc)
            kvb_ref[c] = _dot((kc * dk_b).T.astype(BF), vc)

        if with_state:
            s_f, s_b = s0f_ref[0, hp], s0b_ref[0, hp]
        else:
            s_f = s_b = jnp.zeros((LANES, LANES), F32)
        for c in range(n_chunks):
            sf_ref[c] = s_f
            s_f = ds_f * s_f + kvf_ref[c]
        for c in reversed(range(n_chunks)):
            sb_ref[c] = s_b
            s_b = ds_b * s_b + kvb_ref[c]
        if not with_state:
            ff_ref[0, hp] = s_f
            fb_ref[0, hp] = s_b

        ng = ng_ref[:, cs]
        dq_f = jnp.exp(lgf * (ii + 1.0))
        dq_b = jnp.exp(lgb * (c_len - ii))
        for c in range(n_chunks):
            rows = slice(c * c_len, (c + 1) * c_len)
            qc = q_ref[0, rows, cs]
            kc = (k_ref[0, rows, cs] * k_scale).astype(BF)
            vc = v_ref[0, rows, cs].astype(BF)
            att = _dot_nt(qc.astype(BF), kc) * dm_ref[hp]
            o = _dot(att.astype(BF), vc)
            if need_state:
                o = (o + _dot((qc * dq_f).astype(BF), sf_ref[c].astype(BF))
                     + _dot((qc * dq_b).astype(BF), sb_ref[c].astype(BF)))
            mu = jnp.mean(o, axis=-1, keepdims=True)
            oc = o - mu
            var = jnp.mean(oc * oc, axis=-1, keepdims=True)
            on = oc * lax.rsqrt(var + NORM_EPS)
            gg = g_ref[0, rows, cs]
            y_ref[0, rows, cs] = (gg * _sigmoid(gg) * (on * ng)).astype(BF)


def _retention(zs, col0, dec, ng, s0f, s0b, n_heads):
    b, l, _ = zs.shape
    c_len = min(RET_CHUNK_MAX, l)
    nc = l // c_len
    with_state = s0f is not None
    n_hp = 1 if nc > 1 else n_heads
    hw = n_hp * LANES
    cb = col0 // hw

    def zspec(seg):
        return pl.BlockSpec((1, l, hw), lambda hg, bi: (bi, 0, cb + seg * (n_heads // n_hp) + hg))

    st_spec = pl.BlockSpec((1, n_hp, LANES, LANES), lambda hg, bi: (bi, hg, 0, 0))
    in_specs = [zspec(0), zspec(1), zspec(2), zspec(3),
                pl.BlockSpec((2, n_hp, 1, LANES), lambda hg, bi: (0, hg, 0, 0)),
                pl.BlockSpec((1, hw), lambda hg, bi: (0, hg))]
    args = [zs, zs, zs, zs, dec, ng]
    y_spec = pl.BlockSpec((1, l, hw), lambda hg, bi: (bi, 0, hg))
    y_shape = jax.ShapeDtypeStruct((b, l, n_heads * LANES), BF)
    if with_state:
        in_specs += [st_spec, st_spec]
        args += [s0f, s0b]
        out_specs, out_shape = y_spec, y_shape
    else:
        st_shape = jax.ShapeDtypeStruct((b, n_heads, LANES, LANES), F32)
        out_specs, out_shape = [y_spec, st_spec, st_spec], [y_shape, st_shape, st_shape]
    scratch = ([pltpu.VMEM((n_hp, c_len, c_len), F32)]
               + [pltpu.VMEM((nc, LANES, LANES), F32) for _ in range(4)])
    return pl.pallas_call(
        functools.partial(_ret_kernel, with_state=with_state, n_chunks=nc, c_len=c_len, n_hp=n_hp,
                          k_scale=float(LANES) ** -0.5),
        grid=(n_heads // n_hp, b),
        in_specs=in_specs, out_specs=out_specs, out_shape=out_shape,
        scratch_shapes=scratch,
        compiler_params=_cparams(("arbitrary", "arbitrary")),
        name="retention",
    )(*args)


def _s5_prep_kernel(lre_ref, lim_ref, ldt_ref, btre_ref, btim_ref, cre_ref, cim_ref,
                    toep_ref, wd_ref, wo_ref, av_ref, *, ch, p, seg_chunks):
    t = S5_T
    gw = (LANES // ch) * p
    btre, btim, cre, cim = btre_ref[...], btim_ref[...], cre_ref[...], cim_ref[...]
    r1 = lax.broadcasted_iota(jnp.int32, (LANES, LANES), 0)
    c1 = lax.broadcasted_iota(jnp.int32, (LANES, LANES), 1)
    same_group = (r1 // ch) == (c1 // ch)
    lane_valid = c1 < p
    r5 = lax.broadcasted_iota(jnp.int32, (LANES, gw), 0)
    c5 = lax.broadcasted_iota(jnp.int32, (LANES, gw), 1)
    own_state = (r5 // ch) == (c5 // p)

    def spread(x):
        x2 = x + pltpu.roll(x, LANES // 2, 1)
        return jnp.where(own_state, jnp.concatenate([x2] * (gw // LANES), axis=1), 0.0)

    xs, ys, lags = [], [], []
    for d in range(2):
        lr, li, dt = lre_ref[d], lim_ref[d], jnp.exp(ldt_ref[d])
        ar, ai = lr * dt, li * dt
        mag = jnp.exp(ar)
        abr, abi = mag * jnp.cos(ai), mag * jnp.sin(ai)
        nr, ni = abr - 1.0, abi
        den = lr * lr + li * li
        cr = (nr * lr + ni * li) / den
        ci = (ni * lr - nr * li) / den
        pr, pi = jnp.ones_like(lr), jnp.zeros_like(lr)
        xd, yd, gd = [], [], []
        for kk in range(t + 1):
            if kk < t:
                wr, wi = cr * pr - ci * pi, cr * pi + ci * pr
                xr, xi = wr * btre - wi * btim, wr * btim + wi * btre
                xd.append((xr, xi))
                hi = lax.Precision.HIGHEST
                gd.append(jnp.where(same_group, _dot_nt(xr, cre, hi) - _dot_nt(xi, cim, hi), 0.0))
            if kk >= 1:
                yd.append((cre * pr - cim * pi, -(cre * pi + cim * pr)))
            if kk == t:
                qr, qi = jnp.where(lane_valid, pr, 0.0), jnp.where(lane_valid, pi, 0.0)
            pr, pi = pr * abr - pi * abi, pr * abi + pi * abr
        xs.append(xd)
        ys.append(yd)
        lags.append(gd)
        inv = 1.0 / ch
        av_ref[0, 2 * d:2 * d + 1] = jnp.sum(spread(qr), axis=0, keepdims=True) * inv
        av_ref[0, 2 * d + 1:2 * d + 2] = jnp.sum(spread(qi), axis=0, keepdims=True) * inv
        for _ in range(seg_chunks.bit_length() - 1):
            qr, qi = qr * qr - qi * qi, 2.0 * qr * qi
        av_ref[0, 4 + 2 * d:5 + 2 * d] = jnp.sum(spread(qr), axis=0, keepdims=True) * inv
        av_ref[0, 5 + 2 * d:6 + 2 * d] = jnp.sum(spread(qi), axis=0, keepdims=True) * inv

    for ti in range(t):
        for to in range(t):
            lag = to - ti
            tile = lags[0][lag] if lag > 0 else (lags[1][-lag] if lag < 0 else lags[0][0] + lags[1][0])
            toep_ref[0, ti * LANES:(ti + 1) * LANES, to * LANES:(to + 1) * LANES] = tile.astype(BF)
    for tt in range(t):
        rows = slice(tt * LANES, (tt + 1) * LANES)
        xf, xb = xs[0][t - 1 - tt], xs[1][tt]
        yf, yb = ys[0][tt], ys[1][t - 1 - tt]
        for m, (xv, yv) in enumerate(((xf[0], yf[0]), (xf[1], yf[1]), (xb[0], yb[0]), (xb[1], yb[1]))):
            wd_ref[0, rows, m * gw:(m + 1) * gw] = spread(xv).astype(BF)
            wo_ref[0, m * gw:(m + 1) * gw, rows] = spread(yv).T.astype(BF)


def _s5_prep(lam_re, lam_im, log_dt, b_re, b_im, c_re, c_im):
    _, g, p = lam_re.shape
    ch = b_re.shape[-1]
    assert p == LANES // 2 and LANES % ch == 0
    nb = g * ch // LANES
    gw = (LANES // ch) * p
    t = S5_T
    seg_chunks = S5_SEG // t
    assert seg_chunks & (seg_chunks - 1) == 0

    def prow(a, padval):
        return jnp.pad(jnp.repeat(a, ch, axis=1), ((0, 0), (0, 0), (0, LANES - p)), constant_values=padval)

    def wrow(a):
        return jnp.pad(a.reshape(g * ch, p), ((0, 0), (0, LANES - p)))

    lre = prow(lam_re, -0.5)
    lim = prow(lam_im, 0.0)
    ldt = prow(jnp.broadcast_to(log_dt[..., None], lam_re.shape), 0.0)
    lspec = pl.BlockSpec((2, LANES, LANES), lambda i: (0, i, 0))
    wspec = pl.BlockSpec((LANES, LANES), lambda i: (i, 0))
    return pl.pallas_call(
        functools.partial(_s5_prep_kernel, ch=ch, p=p, seg_chunks=seg_chunks),
        grid=(nb,),
        in_specs=[lspec, lspec, lspec, wspec, wspec, wspec, wspec],
        out_specs=[pl.BlockSpec((1, t * LANES, t * LANES), lambda i: (i, 0, 0)),
                   pl.BlockSpec((1, t * LANES, 4 * gw), lambda i: (i, 0, 0)),
                   pl.BlockSpec((1, 4 * gw, t * LANES), lambda i: (i, 0, 0)),
                   pl.BlockSpec((1, 8, gw), lambda i: (i, 0, 0))],
        out_shape=[jax.ShapeDtypeStruct((nb, t * LANES, t * LANES), BF),
                   jax.ShapeDtypeStruct((nb, t * LANES, 4 * gw), BF),
                   jax.ShapeDtypeStruct((nb, 4 * gw, t * LANES), BF),
                   jax.ShapeDtypeStruct((nb, 8, gw), F32)],
        compiler_params=_cparams(("parallel",)),
        name="s5_prep",
    )(lre, lim, ldt, wrow(jnp.swapaxes(b_re, 1, 2)), wrow(jnp.swapaxes(b_im, 1, 2)), wrow(c_re), wrow(c_im))


def _gelu(x):
    return 0.5 * x * (1.0 + jnp.tanh(0.7978845608028654 * (x + 0.044715 * (x * x * x))))


def _s5_kernel(*refs, chain):
    if chain:
        (u_ref, toep_ref, wd_ref, wo_ref, a_ref, dsk_ref, s0_ref,
         y_ref, lhs_ref, d_ref, s_ref, xs_ref) = refs
    else:
        (u_ref, toep_ref, wd_ref, wo_ref, a_ref, dsk_ref,
         y_ref, fin_ref, lhs_ref, d_ref, s_ref) = refs
    nc, t, pb, _ = u_ref.shape
    rows = nc * pb
    w = a_ref.shape[-1]
    for tt in range(t):
        lhs_ref[:, tt * LANES:(tt + 1) * LANES] = u_ref[:, tt].reshape(rows, LANES).astype(BF)
    d_ref[...] = _dot(lhs_ref[...], wd_ref[0])
    av = a_ref[0]
    afr, afi, abr, abi = av[0:1], av[1:2], av[2:3], av[3:4]

    def scan(init, store):
        sfr, sfi, sbr, sbi = init
        for c in range(nc):
            rf = slice(c * pb, (c + 1) * pb)
            rb = slice((nc - 1 - c) * pb, (nc - c) * pb)
            if store:
                s_ref[rf, 0:w] = sfr.astype(BF)
                s_ref[rf, w:2 * w] = sfi.astype(BF)
                s_ref[rb, 2 * w:3 * w] = sbr.astype(BF)
                s_ref[rb, 3 * w:4 * w] = sbi.astype(BF)
            sfr, sfi = (afr * sfr - afi * sfi + d_ref[rf, 0:w],
                        afr * sfi + afi * sfr + d_ref[rf, w:2 * w])
            sbr, sbi = (abr * sbr - abi * sbi + d_ref[rb, 2 * w:3 * w],
                        abr * sbi + abi * sbr + d_ref[rb, 3 * w:4 * w])
        return sfr, sfi, sbr, sbi

    zero = jnp.zeros((pb, w), F32)
    if not chain:
        fin = scan((zero, zero, zero, zero), True)
        for m in range(4):
            fin_ref[0, :, m * w:(m + 1) * w] = fin[m]
    else:
        efr, efi, ebr, ebi = scan((zero, zero, zero, zero), False)
        s0 = s0_ref[0, 0]
        xr, xi = s0[0:1], s0[1:2]
        for j in range(pb):
            xs_ref[j:j + 1, 0:w] = xr
            xs_ref[j:j + 1, w:2 * w] = xi
            xr, xi = (av[4:5] * xr - av[5:6] * xi + efr[j:j + 1],
                      av[4:5] * xi + av[5:6] * xr + efi[j:j + 1])
        xr, xi = s0[2:3], s0[3:4]
        for j in reversed(range(pb)):
            xs_ref[j:j + 1, 2 * w:3 * w] = xr
            xs_ref[j:j + 1, 3 * w:4 * w] = xi
            xr, xi = (av[6:7] * xr - av[7:8] * xi + ebr[j:j + 1],
                      av[6:7] * xi + av[7:8] * xr + ebi[j:j + 1])
        scan((xs_ref[:, 0:w], xs_ref[:, w:2 * w], xs_ref[:, 2 * w:3 * w], xs_ref[:, 3 * w:4 * w]), True)

    yv = _dot(lhs_ref[...], toep_ref[0]) + _dot(s_ref[...], wo_ref[0])
    dsk = dsk_ref[...]
    for tt in range(t):
        yt = yv[:, tt * LANES:(tt + 1) * LANES] + dsk * u_ref[:, tt].reshape(rows, LANES)
        y_ref[:, tt] = _gelu(yt).reshape(nc, pb, LANES)


def _s5(u4, toep, wd, wo, av, dskip, s0):
    nc, t, npb, nch = u4.shape
    nb = toep.shape[0]
    chain = s0 is not None
    pb = npb // s0.shape[1] if chain else min(16, npb)
    sw = wd.shape[-1]
    rows = nc * pb
    ublk = pl.BlockSpec((nc, t, pb, LANES), lambda kb, hh: (0, 0, hh, kb))

    def wspec(arr):
        return pl.BlockSpec((1,) + arr.shape[1:], lambda kb, hh: (kb, 0, 0))

    in_specs = [ublk, wspec(toep), wspec(wd), wspec(wo), wspec(av),
                pl.BlockSpec((1, LANES), lambda kb, hh: (0, kb))]
    args = [u4, toep, wd, wo, av, dskip]
    scratch = [pltpu.VMEM((rows, t * LANES), BF), pltpu.VMEM((rows, sw), F32), pltpu.VMEM((rows, sw), BF)]
    if chain:
        in_specs += [pl.BlockSpec((1, 1, 4, sw // 4), lambda kb, hh: (kb, hh, 0, 0))]
        args += [s0]
        out_specs = ublk
        out_shape = jax.ShapeDtypeStruct(u4.shape, F32)
        scratch.append(pltpu.VMEM((pb, sw), F32))
    else:
        out_specs = [ublk, pl.BlockSpec((1, pb, sw), lambda kb, hh: (kb, hh, 0))]
        out_shape = [jax.ShapeDtypeStruct(u4.shape, F32), jax.ShapeDtypeStruct((nb, npb, sw), F32)]
    return pl.pallas_call(
        functools.partial(_s5_kernel, chain=chain),
        grid=(nb, npb // pb),
        in_specs=in_specs, out_specs=out_specs, out_shape=out_shape,
        scratch_shapes=scratch,
        compiler_params=_cparams(("parallel", "parallel")),
        name="s5_chunked",
    )(*args)


def _glu_kernel(y_ref, w_ref, b_ref, o_ref):
    y = y_ref[...]
    o_ref[...] = (y * _sigmoid(_dot(y.astype(BF), w_ref[...]) + b_ref[...])).astype(BF)


def _glu(y, w, layer, b):
    m, n = y.shape
    tm = min(1024, m)
    return pl.pallas_call(
        _glu_kernel,
        grid=(m // tm,),
        in_specs=[pl.BlockSpec((tm, n), lambda i: (i, 0)),
                  pl.BlockSpec((None, n, n), lambda i: (layer, 0, 0)),
                  pl.BlockSpec((1, n), lambda i: (0, 0))],
        out_specs=pl.BlockSpec((tm, n), lambda i: (i, 0)),
        out_shape=jax.ShapeDtypeStruct((m, n), BF),
        compiler_params=_cparams(("parallel",)),
        name="s5_glu",
    )(y, w, b)


def _res_epilogue(acc_ref, x_ref, gate_ref, ng_ref, o_ref, nxt_refs):
    gn = gate_ref[0] * ng_ref[...]
    if nxt_refs is not None:
        ng2_ref, sh2_ref, sc2_ref, h_ref = nxt_refs
        g2 = ng2_ref[...] * (1.0 + sc2_ref[0])
        sh2 = sh2_ref[0]
    rows = 128
    for r0 in range(0, o_ref.shape[0], rows):
        acc = acc_ref[r0:r0 + rows]
        inv = lax.rsqrt(jnp.mean(acc * acc, axis=-1, keepdims=True) + NORM_EPS)
        xn = x_ref[r0:r0 + rows] + (acc * inv) * gn
        o_ref[r0:r0 + rows] = xn
        if nxt_refs is not None:
            inv2 = lax.rsqrt(jnp.mean(xn * xn, axis=-1, keepdims=True) + NORM_EPS)
            h_ref[r0:r0 + rows] = ((xn * inv2) * g2 + sh2).astype(BF)


def _merge_kernel(ya_ref, yb_ref, yc_ref, ga_ref, gb_ref, gc_ref, wb_ref, o_ref):
    acc = (_sigmoid(ga_ref[...].astype(F32)) * _dot(ya_ref[...], wb_ref[0])
           + _sigmoid(gb_ref[...].astype(F32)) * _dot(yb_ref[...], wb_ref[1])
           + _sigmoid(gc_ref[...].astype(F32)) * _dot(yc_ref[...], wb_ref[2]))
    o_ref[...] = acc.astype(BF)


def _merge(ya, yb, yc, zg, wb, layer):
    m, dm = ya.shape
    d = wb.shape[-1]
    tm, tn = min(1024, m), 512
    nt = d // tn
    ysp = pl.BlockSpec((tm, dm), lambda i, j: (i, 0))

    def gsp(br):
        return pl.BlockSpec((tm, tn), lambda i, j: (i, j + br * nt))

    return pl.pallas_call(
        _merge_kernel,
        grid=(m // tm, nt),
        in_specs=[ysp, ysp, ysp, gsp(0), gsp(1), gsp(2),
                  pl.BlockSpec((None, 3, dm, tn), lambda i, j: (layer, 0, 0, j))],
        out_specs=pl.BlockSpec((tm, tn), lambda i, j: (i, j)),
        out_shape=jax.ShapeDtypeStruct((m, d), BF),
        compiler_params=_cparams(("parallel", "parallel")),
        name="branch_merge",
    )(ya, yb, yc, zg, zg, zg, wb)


def _mm_res_kernel(*refs, with_next):
    if with_next:
        a_ref, w_ref, x_ref, gate_ref, ng_ref, ng2_ref, sh2_ref, sc2_ref, o_ref, h_ref, acc_ref = refs
        nxt_refs = (ng2_ref, sh2_ref, sc2_ref, h_ref)
    else:
        a_ref, w_ref, x_ref, gate_ref, ng_ref, o_ref, acc_ref = refs
        nxt_refs = None
    k = pl.program_id(1)

    @pl.when(k == 0)
    def _():
        acc_ref[...] = jnp.zeros_like(acc_ref)

    acc_ref[...] += _dot(a_ref[...], w_ref[...])

    @pl.when(k == pl.num_programs(1) - 1)
    def _():
        _res_epilogue(acc_ref, x_ref, gate_ref, ng_ref, o_ref, nxt_refs)


def _mm_res(a, w, layer, x, mods, gate_chunk, ng, rows_per_mod, nxt=None):
    m, kdim = a.shape
    d = w.shape[-1]
    tm = min(512, m)
    tk = kdim // 4 if kdim > 2048 else kdim // 2
    assert tk % LANES == 0
    in_specs = [pl.BlockSpec((tm, tk), lambda i, k: (i, k)),
                pl.BlockSpec((None, tk, d), lambda i, k: (layer, k, 0)),
                pl.BlockSpec((tm, d), lambda i, k: (i, 0)),
                _mod_spec(gate_chunk, d, rows_per_mod, tm),
                pl.BlockSpec((1, d), lambda i, k: (0, 0))]
    args = [a, w, x, mods, ng]
    x_spec = pl.BlockSpec((tm, d), lambda i, k: (i, 0))
    x_shape = jax.ShapeDtypeStruct((m, d), F32)
    if nxt is None:
        out_specs, out_shape = x_spec, x_shape
    else:
        ng2, mods2, sh_chunk, sc_chunk = nxt
        in_specs += [pl.BlockSpec((1, d), lambda i, k: (0, 0)),
                     _mod_spec(sh_chunk, d, rows_per_mod, tm),
                     _mod_spec(sc_chunk, d, rows_per_mod, tm)]
        args += [ng2, mods2, mods2]
        out_specs = [x_spec, pl.BlockSpec((tm, d), lambda i, k: (i, 0))]
        out_shape = [x_shape, jax.ShapeDtypeStruct((m, d), BF)]
    return pl.pallas_call(
        functools.partial(_mm_res_kernel, with_next=nxt is not None),
        grid=(m // tm, kdim // tk),
        in_specs=in_specs, out_specs=out_specs, out_shape=out_shape,
        scratch_shapes=[pltpu.VMEM((tm, d), F32)],
        compiler_params=_cparams(("parallel", "arbitrary")),
        name="matmul_norm_residual",
    )(*args)


FFN_HALO = 16


def _ffn_up_kernel(hp_ref, h_ref, hn_ref, wv_ref, wg_ref, cwv_ref, cwg_ref, cbv_ref, cbg_ref,
                   o_ref, hb_ref, uv_ref, ug_ref, *, tm, piece, seq_len):
    i, j = pl.program_id(0), pl.program_id(1)
    hl = FFN_HALO
    n_pieces = tm // piece
    stride = piece + hl

    @pl.when(j == 0)
    def _():
        zero = jnp.zeros(hp_ref.shape, BF)
        if n_pieces == 1 and seq_len > tm:
            hb_ref[0:hl] = jnp.where(lax.rem(i * tm, seq_len) != 0, hp_ref[...], zero)
            hb_ref[stride:stride + hl] = jnp.where(lax.rem((i + 1) * tm, seq_len) != 0, hn_ref[...], zero)
        else:
            for pc in range(n_pieces + 1):
                hb_ref[pc * stride:pc * stride + hl] = zero
        for pc in range(n_pieces):
            hb_ref[hl + pc * stride:hl + pc * stride + piece] = h_ref[pc * piece:(pc + 1) * piece]

    hb = hb_ref[...]
    ug_ref[...] = _dot(hb, wg_ref[...])
    uv_ref[...] = _dot(hb, wv_ref[...])
    cwv, cwg, cbv, cbg = cwv_ref[...], cwg_ref[...], cbv_ref[...], cbg_ref[...]
    for pc in range(n_pieces):
        b0 = hl + pc * stride

        def conv(u_ref, cw, cb):
            return (u_ref[b0 - 1:b0 - 1 + piece] * cw[0:1] + u_ref[b0:b0 + piece] * cw[1:2]
                    + u_ref[b0 + 1:b0 + 1 + piece] * cw[2:3] + cb)

        gate = conv(ug_ref, cwg, cbg)
        act = gate * _sigmoid(gate)
        o_ref[pc * piece:(pc + 1) * piece] = (act * conv(uv_ref, cwv, cbv)).astype(BF)


def _ffn_up(h, w_up, layer, conv_w, conv_b, seq_len):
    m, d = h.shape
    f = w_up.shape[-1] // 2
    tm, tf = min(1024, m), 512
    piece = min(seq_len, tm)
    assert tm % piece == 0 and seq_len % piece == 0
    nf = f // tf
    hl = FFN_HALO
    nblk = m // hl
    rows = (tm // piece) * (piece + hl) + hl

    return pl.pallas_call(
        functools.partial(_ffn_up_kernel, tm=tm, piece=piece, seq_len=seq_len),
        grid=(m // tm, nf),
        in_specs=[pl.BlockSpec((hl, d), lambda i, j: (jnp.maximum(i * (tm // hl) - 1, 0), 0)),
                  pl.BlockSpec((tm, d), lambda i, j: (i, 0)),
                  pl.BlockSpec((hl, d), lambda i, j: (jnp.minimum((i + 1) * (tm // hl), nblk - 1), 0)),
                  pl.BlockSpec((None, d, tf), lambda i, j: (layer, 0, j)),
                  pl.BlockSpec((None, d, tf), lambda i, j: (layer, 0, j + nf)),
                  pl.BlockSpec((3, tf), lambda i, j: (0, j)),
                  pl.BlockSpec((3, tf), lambda i, j: (0, j + nf)),
                  pl.BlockSpec((1, tf), lambda i, j: (0, j)),
                  pl.BlockSpec((1, tf), lambda i, j: (0, j + nf))],
        out_specs=pl.BlockSpec((tm, tf), lambda i, j: (i, j)),
        out_shape=jax.ShapeDtypeStruct((m, f), BF),
        scratch_shapes=[pltpu.VMEM((rows, d), BF), pltpu.VMEM((rows, tf), F32), pltpu.VMEM((rows, tf), F32)],
        compiler_params=_cparams(("parallel", "arbitrary")),
        name="ffn_up_conv",
    )(h, h, h, w_up, w_up, conv_w, conv_w, conv_b, conv_b)


def _pair_swap(w):
    r = w.reshape(w.shape[:-1] + (w.shape[-1] // 2, 2))
    return jnp.stack([-r[..., 1], r[..., 0]], axis=-1).reshape(w.shape)


def _rope_tables(n_tokens, dr):
    rows = n_tokens // GRID_W
    row = jnp.broadcast_to(jnp.arange(rows, dtype=F32)[:, None], (rows, GRID_W)).reshape(-1)
    col = jnp.broadcast_to(jnp.arange(GRID_W, dtype=F32)[None, :], (rows, GRID_W)).reshape(-1)
    nf = dr // 4
    inv = ROPE_BASE ** (-jnp.arange(nf, dtype=F32) / nf)
    ang = jnp.concatenate([row[:, None] * inv, col[:, None] * inv], axis=-1)
    ang = jnp.repeat(ang, 2, axis=-1)
    pad = jnp.zeros((n_tokens, LANES - dr), F32)
    return jnp.concatenate([jnp.cos(ang), pad], axis=-1), jnp.concatenate([jnp.sin(ang), pad], axis=-1)


def kernel(x_prompt, x_sample, cache_mla_ckv, cache_mla_krope, state_ret_fwd, state_ret_bwd, state_s5_fwd, state_s5_bwd, c, c_ctx, ada_w, ada_b, norm_g, w_in, s5_lam_re, s5_lam_im, s5_log_dt, s5_b_re, s5_b_im, s5_c_re, s5_c_im, s5_d, s5_glu_w, s5_glu_b, ret_decay, ret_norm_g, mla_q_norm, mla_kv_norm, mla_w_uq, mla_w_uk, mla_w_uv, w_branch, w_out, ffn_w_up, ffn_conv_w, ffn_conv_b, ffn_w_down):
    bsz, seq, d = x_prompt.shape
    dbsz, dseq, _ = x_sample.shape
    depth = ada_w.shape[0]
    dm = d // 2
    n_heads = ret_decay.shape[-1]
    dr = cache_mla_krope.shape[-1]
    q_lora, kv_lora = mla_w_uq.shape[1], mla_w_uk.shape[1]
    g5, p5, ch5 = s5_b_re.shape[1:]
    assert dm == n_heads * LANES and dr == LANES // 2 and mla_w_uk.shape[-1] == dm
    assert seq == S5_SEG and dseq % S5_SEG == 0
    qscale = float(LANES + dr) ** -0.5 * 1.4426950408889634

    cvecs = jnp.concatenate([c_ctx[None], c, jnp.zeros((8 - 1 - dbsz, d), F32)], axis=0)
    mods_all = _ada_mods(cvecs, ada_w, ada_b)

    cos_l, sin_l = _rope_tables(dseq, dr)
    ones_tab = jnp.concatenate([jnp.ones((256, dr), F32), jnp.zeros((256, LANES - dr), F32)], axis=-1)
    zeros_tab = jnp.zeros((256, LANES), F32)

    cw = np.cumsum([0, dm, dm, dm, dm, dm, q_lora, kv_lora, dr, d, d, d])
    w_a = jnp.concatenate([w_in[:, :, cw[8]:cw[11]], w_in[:, :, cw[0]:cw[5]]], axis=2).astype(BF)
    w_kr = w_in[:, :, cw[7]:cw[8]]
    w_m = jnp.concatenate([w_in[:, :, cw[5]:cw[7]], w_kr, _pair_swap(w_kr)], axis=2).astype(BF)
    uq = mla_w_uq.reshape(depth, q_lora, n_heads, LANES + dr)
    wq_ext = jnp.concatenate([uq[..., :LANES], uq[..., LANES:], _pair_swap(uq[..., LANES:])],
                             axis=-1).reshape(depth, q_lora, n_heads * 2 * LANES).astype(BF)
    wkv = jnp.concatenate([mla_w_uk, mla_w_uv], axis=2).astype(BF)
    glu_w = s5_glu_w.astype(BF)
    wb = w_branch.astype(BF)
    wo_ = w_out.astype(BF)
    wup = ffn_w_up.astype(BF)
    wdn = ffn_w_down.astype(BF)

    outs = {k: [] for k in ("ckv", "krope", "retf", "retb", "s5f", "s5b")}
    nb5 = g5 * ch5 // LANES
    gb5 = LANES // ch5

    def group_mods(l, ctx):
        if ctx:
            return mods_all[l, 0:1].reshape(1, 1, -1), bsz * seq
        return mods_all[l, 1:1 + dbsz].reshape(dbsz, 1, -1), dseq

    xres = [x_prompt.reshape(bsz * seq, d), x_sample.reshape(dbsz * dseq, d)]
    hres = []
    for grp in range(2):
        mods, rpm = group_mods(0, grp == 0)
        hres.append(_norm_mod(xres[grp], mods, norm_g[0, 0].reshape(1, d), rpm))

    for l in range(depth):
        qg = mla_q_norm[l].reshape(1, q_lora)
        kvg = mla_kv_norm[l].reshape(1, kv_lora)
        ng = norm_g[l].reshape(4, 1, d)
        glu_b = s5_glu_b[l].reshape(1, dm)
        conv_w = ffn_conv_w[l]
        conv_b = ffn_conv_b[l].reshape(1, -1)
        dec = jnp.broadcast_to(ret_decay[l].reshape(2, n_heads, 1, 1), (2, n_heads, 1, LANES))
        rng = ret_norm_g[l].reshape(1, dm)
        dskip = s5_d[l].reshape(1, dm)
        toep, wd, wo5, av = _s5_prep(s5_lam_re[l], s5_lam_im[l], s5_log_dt[l], s5_b_re[l], s5_b_im[l],
                                     s5_c_re[l], s5_c_im[l])

        for grp in range(2):
            ctx = grp == 0
            x, h = xres[grp], hres[grp]
            nbat, slen = (bsz, seq) if ctx else (dbsz, dseq)
            m = nbat * slen
            mods, rpm = group_mods(l, ctx)

            zg, zu, zs = _inproj(h, w_a, l, 3 * d, dm)

            nseg = slen // S5_SEG
            npb = nbat * nseg
            u = zu.reshape(npb, S5_SEG, dm).transpose(1, 0, 2)
            u4 = u.reshape(S5_SEG // S5_T, S5_T, npb, dm)
            if ctx:
                y4, fin = _s5(u4, toep, wd, wo5, av, dskip, None)
                fin = fin.reshape(nb5, npb, 4, gb5, p5).transpose(1, 2, 0, 3, 4).reshape(npb, 4, g5, p5)
                outs["s5f"].append(jnp.stack([fin[:, 0], fin[:, 1]], axis=-1))
                outs["s5b"].append(jnp.stack([fin[:, 2], fin[:, 3]], axis=-1))
            else:
                s0 = jnp.stack([state_s5_fwd[:, l, :, :, 0], state_s5_fwd[:, l, :, :, 1],
                                state_s5_bwd[:, l, :, :, 0], state_s5_bwd[:, l, :, :, 1]], axis=1)
                s0 = s0.reshape(dbsz, 4, nb5, gb5 * p5).transpose(2, 0, 1, 3)
                y4 = _s5(u4, toep, wd, wo5, av, dskip, s0)
            ya_tm = _glu(y4.reshape(S5_SEG * npb, dm), glu_w, l, glu_b)
            ya = ya_tm.reshape(S5_SEG, npb, dm).transpose(1, 0, 2).reshape(m, dm)

            zs3 = zs.reshape(nbat, slen, -1)
            if ctx:
                yb, rf, rb = _retention(zs3, 0, dec, rng, None, None, n_heads)
                outs["retf"].append(rf)
                outs["retb"].append(rb)
            else:
                yb = _retention(zs3, 0, dec, rng, state_ret_fwd[:, l], state_ret_bwd[:, l], n_heads)
            yb = yb.reshape(m, dm)

            if ctx:
                q, k, v, ckv, kr = _mla_proj(h, w_m[l], qg, kvg, wq_ext[l], wkv[l], ones_tab, zeros_tab,
                                             n_heads, qscale)
                outs["ckv"].append(ckv.reshape(nbat, slen, kv_lora))
                outs["krope"].append(kr[:, :dr].reshape(nbat, slen, dr))
                k, v = k.reshape(nbat, slen, -1), v.reshape(nbat, slen, -1)
            else:
                q, k, v, _, _ = _mla_proj(h, w_m[l], qg, kvg, wq_ext[l], wkv[l], cos_l, sin_l, n_heads, qscale)
                past = cache_mla_ckv.shape[2]
                kr_pad = jnp.pad(cache_mla_krope[:, l], ((0, 0), (0, 0), (0, LANES - dr)))
                kc, vc = _cache_kv(cache_mla_ckv[:, l].reshape(nbat * past, kv_lora),
                                   kr_pad.reshape(nbat * past, LANES), wkv[l], n_heads)
                k = jnp.concatenate([k.reshape(nbat, slen, -1), kc.reshape(nbat, past, -1)], axis=1)
                v = jnp.concatenate([v.reshape(nbat, slen, -1), vc.reshape(nbat, past, -1)], axis=1)
            yc = _attention(q.reshape(nbat, slen, -1), k, v, n_heads).reshape(m, dm)

            merged = _merge(ya, yb, yc, zg, wb, l)
            x, h = _mm_res(merged, wo_, l, x, mods, 2, ng[1], rpm, nxt=(ng[2], mods, 3, 4))
            act = _ffn_up(h, wup, l, conv_w, conv_b, slen)
            if l + 1 < depth:
                x, h = _mm_res(act, wdn, l, x, mods, 5, ng[3], rpm,
                               nxt=(norm_g[l + 1, 0].reshape(1, d), group_mods(l + 1, ctx)[0], 0, 1))
            else:
                x, h = _mm_res(act, wdn, l, x, mods, 5, ng[3], rpm), None
            xres[grp], hres[grp] = x, h

    return (xres[0].reshape(bsz, seq, d), xres[1].reshape(dbsz, dseq, d),
            jnp.stack(outs["ckv"], axis=1), jnp.stack(outs["krope"], axis=1),
            jnp.stack(outs["retf"], axis=1), jnp.stack(outs["retb"], axis=1),
            jnp.stack(outs["s5f"], axis=1), jnp.stack(outs["s5b"], axis=1))
```

```python
import functools

import numpy as np
import jax
import jax.numpy as jnp
from jax import lax
from jax.experimental import pallas as pl
from jax.experimental.pallas import tpu as pltpu

F32 = jnp.float32
BF = jnp.bfloat16

NORM_EPS = 1e-6
ROPE_BASE = 10000.0
GRID_W = 64
LANES = 128
S5_T = 8
S5_SEG = 256
RET_CHUNK_MAX = 512
VMEM_LIMIT = 56 << 20


def _cparams(sem):
    return pltpu.CompilerParams(dimension_semantics=sem, vmem_limit_bytes=VMEM_LIMIT)


def _dot(a, b):
    return jnp.dot(a, b, preferred_element_type=F32)


def _dot_nt(a, b, precision=None):
    return lax.dot_general(a, b, (((1,), (1,)), ((), ())), precision=precision, preferred_element_type=F32)


def _rms(x, g):
    return x * lax.rsqrt(jnp.mean(x * x, axis=-1, keepdims=True) + NORM_EPS) * g


def _rms_mod(x, g, sc, sh):
    return _rms(x, g) * (1.0 + sc) + sh


def _sigmoid(x):
    return 1.0 / (1.0 + jnp.exp(-x))


def _mod_spec(chunk, d, rows_per_mod, tm):
    return pl.BlockSpec((1, 1, d), lambda i, *_: ((i * tm) // rows_per_mod, 0, chunk))


def _ada_kernel(c_ref, w_ref, b_ref, o_ref):
    c = c_ref[...]
    s = (c * _sigmoid(c)).astype(BF)
    o_ref[0] = _dot(s, w_ref[0].astype(BF)) + b_ref[0]


def _ada_mods(cvecs, ada_w, ada_b):
    depth, d, n = ada_w.shape
    tn = 1024
    return pl.pallas_call(
        _ada_kernel,
        grid=(depth, n // tn),
        in_specs=[pl.BlockSpec((8, d), lambda l, j: (0, 0)),
                  pl.BlockSpec((1, d, tn), lambda l, j: (l, 0, j)),
                  pl.BlockSpec((1, 1, tn), lambda l, j: (l, 0, j))],
        out_specs=pl.BlockSpec((1, 8, tn), lambda l, j: (l, 0, j)),
        out_shape=jax.ShapeDtypeStruct((depth, 8, n), F32),
        compiler_params=_cparams(("parallel", "parallel")),
        name="ada_mods",
    )(cvecs, ada_w, ada_b.reshape(depth, 1, n))


def _norm_mod_kernel(x_ref, g_ref, sh_ref, sc_ref, h_ref):
    h_ref[...] = _rms_mod(x_ref[...], g_ref[...], sc_ref[0], sh_ref[0]).astype(BF)


def _norm_mod(x, mods, ng0, rows_per_mod):
    m, d = x.shape
    tm = min(1024, m)
    return pl.pallas_call(
        _norm_mod_kernel,
        grid=(m // tm,),
        in_specs=[pl.BlockSpec((tm, d), lambda i: (i, 0)),
                  pl.BlockSpec((1, d), lambda i: (0, 0)),
                  _mod_spec(0, d, rows_per_mod, tm),
                  _mod_spec(1, d, rows_per_mod, tm)],
        out_specs=pl.BlockSpec((tm, d), lambda i: (i, 0)),
        out_shape=jax.ShapeDtypeStruct((m, d), BF),
        compiler_params=_cparams(("parallel",)),
        name="norm_mod",
    )(x, ng0, mods, mods)


def _inproj_kernel(h_ref, w_ref, og_ref, ou_ref, os_ref, *, n_gate_tiles, n_u_tiles):
    j = pl.program_id(1)

    @pl.when(j < n_gate_tiles)
    def _():
        og_ref[...] = _dot(h_ref[...], w_ref[...]).astype(BF)

    @pl.when((j >= n_gate_tiles) & (j < n_gate_tiles + n_u_tiles))
    def _():
        ou_ref[...] = _dot(h_ref[...], w_ref[...])

    @pl.when(j >= n_gate_tiles + n_u_tiles)
    def _():
        os_ref[...] = _dot(h_ref[...], w_ref[...])


def _inproj(h, w_a, layer, n_gate, n_u):
    m, d = h.shape
    n = w_a.shape[-1]
    tm, tn = min(1024, m), 1024
    ngt, nut = n_gate // tn, n_u // tn
    nst = (n - n_gate - n_u) // tn
    return pl.pallas_call(
        functools.partial(_inproj_kernel, n_gate_tiles=ngt, n_u_tiles=nut),
        grid=(m // tm, ngt + nut + nst),
        in_specs=[pl.BlockSpec((tm, d), lambda i, j: (i, 0)),
                  pl.BlockSpec((None, d, tn), lambda i, j: (layer, 0, j))],
        out_specs=[pl.BlockSpec((tm, tn), lambda i, j: (i, jnp.minimum(j, ngt - 1))),
                   pl.BlockSpec((tm, tn), lambda i, j: (i, jnp.clip(j - ngt, 0, nut - 1))),
                   pl.BlockSpec((tm, tn), lambda i, j: (i, jnp.maximum(j - ngt - nut, 0)))],
        out_shape=[jax.ShapeDtypeStruct((m, n_gate), BF),
                   jax.ShapeDtypeStruct((m, n_u), F32),
                   jax.ShapeDtypeStruct((m, n - n_gate - n_u), F32)],
        compiler_params=_cparams(("parallel", "arbitrary")),
        name="in_proj",
    )(h, w_a)


def _rope(seg, cosp, sinp):
    return seg * cosp + pltpu.roll(seg, LANES // 2, 1) * sinp


def _ones_column(rows):
    lane = lax.broadcasted_iota(jnp.int32, (rows, LANES), 1)
    return jnp.where(lane == 0, 1.0, 0.0).astype(BF)


def _store_kv(kv, krot, k_ref, v_ref, n_heads):
    hv = n_heads * LANES
    ones = _ones_column(kv.shape[0])
    for hh in range(n_heads):
        b0 = hh * 2 * LANES
        k_ref[:, b0:b0 + LANES] = kv[:, hh * LANES:(hh + 1) * LANES].astype(BF)
        k_ref[:, b0 + LANES:b0 + 2 * LANES] = krot
        v_ref[:, b0:b0 + LANES] = kv[:, hv + hh * LANES:hv + (hh + 1) * LANES].astype(BF)
        v_ref[:, b0 + LANES:b0 + 2 * LANES] = ones


def _mla_proj_kernel(h_ref, wm_ref, qg_ref, kvg_ref, wq_ref, wkv_ref, cos_ref, sin_ref,
                     q_ref, k_ref, v_ref, ckv_ref, kr_ref, *, q_lora, kv_lora, n_heads, scale):
    z = _dot(h_ref[...], wm_ref[...])
    cosp, sinp = cos_ref[...], sin_ref[...]

    cqn = _rms(z[:, :q_lora], qg_ref[...]).astype(BF)
    qraw = _dot(cqn, wq_ref[...])
    for hh in range(n_heads):
        b0 = hh * 2 * LANES
        q_ref[:, b0:b0 + LANES] = (qraw[:, b0:b0 + LANES] * scale).astype(BF)
        q_ref[:, b0 + LANES:b0 + 2 * LANES] = (_rope(qraw[:, b0 + LANES:b0 + 2 * LANES], cosp, sinp) * scale).astype(BF)

    ckv = _rms(z[:, q_lora:q_lora + kv_lora], kvg_ref[...])
    ckv_ref[...] = ckv
    kv = _dot(ckv.astype(BF), wkv_ref[...])
    kr = z[:, q_lora + kv_lora:]
    kr_ref[...] = kr
    _store_kv(kv, _rope(kr, cosp, sinp).astype(BF), k_ref, v_ref, n_heads)


def _mla_proj(h, w_m, qg, kvg, wq_ext, wkv, cosp, sinp, n_heads, scale):
    m, d = h.shape
    q_lora, kv_lora = wq_ext.shape[0], wkv.shape[0]
    tm = min(256, m)
    tab_tiles = cosp.shape[0] // tm
    c2 = lambda i: (0, 0)
    return pl.pallas_call(
        functools.partial(_mla_proj_kernel, q_lora=q_lora, kv_lora=kv_lora, n_heads=n_heads, scale=scale),
        grid=(m // tm,),
        in_specs=[pl.BlockSpec((tm, d), lambda i: (i, 0)),
                  pl.BlockSpec(w_m.shape, c2),
                  pl.BlockSpec((1, q_lora), c2),
                  pl.BlockSpec((1, kv_lora), c2),
                  pl.BlockSpec(wq_ext.shape, c2),
                  pl.BlockSpec(wkv.shape, c2),
                  pl.BlockSpec((tm, LANES), lambda i: (i % tab_tiles, 0)),
                  pl.BlockSpec((tm, LANES), lambda i: (i % tab_tiles, 0))],
        out_specs=[pl.BlockSpec((tm, 2 * n_heads * LANES), lambda i: (i, 0)),
                   pl.BlockSpec((tm, 2 * n_heads * LANES), lambda i: (i, 0)),
                   pl.BlockSpec((tm, 2 * n_heads * LANES), lambda i: (i, 0)),
                   pl.BlockSpec((tm, kv_lora), lambda i: (i, 0)),
                   pl.BlockSpec((tm, LANES), lambda i: (i, 0))],
        out_shape=[jax.ShapeDtypeStruct((m, 2 * n_heads * LANES), BF),
                   jax.ShapeDtypeStruct((m, 2 * n_heads * LANES), BF),
                   jax.ShapeDtypeStruct((m, 2 * n_heads * LANES), BF),
                   jax.ShapeDtypeStruct((m, kv_lora), F32),
                   jax.ShapeDtypeStruct((m, LANES), F32)],
        compiler_params=_cparams(("parallel",)),
        name="mla_proj",
    )(h, w_m, qg, kvg, wq_ext, wkv, cosp, sinp)


def _cache_kv_kernel(ckv_ref, kr_ref, wkv_ref, k_ref, v_ref, *, n_heads):
    kv = _dot(ckv_ref[...].astype(BF), wkv_ref[...])
    _store_kv(kv, kr_ref[...].astype(BF), k_ref, v_ref, n_heads)


def _cache_kv(ckv, kr_pad, wkv, n_heads):
    rows, kvl = ckv.shape
    return pl.pallas_call(
        functools.partial(_cache_kv_kernel, n_heads=n_heads),
        grid=(1,),
        in_specs=[pl.BlockSpec((rows, kvl), lambda i: (0, 0)),
                  pl.BlockSpec((rows, LANES), lambda i: (0, 0)),
                  pl.BlockSpec(wkv.shape, lambda i: (0, 0))],
        out_specs=[pl.BlockSpec((rows, 2 * n_heads * LANES), lambda i: (0, 0)),
                   pl.BlockSpec((rows, 2 * n_heads * LANES), lambda i: (0, 0))],
        out_shape=[jax.ShapeDtypeStruct((rows, 2 * n_heads * LANES), BF),
                   jax.ShapeDtypeStruct((rows, 2 * n_heads * LANES), BF)],
        compiler_params=_cparams(("arbitrary",)),
        name="mla_cache_kv",
    )(ckv, kr_pad, wkv)


ATTN_ROWS = 256


def _softmax_pv(q, k, v):
    s = _dot_nt(q, k)
    p = jnp.exp2(s - jnp.max(s, axis=-1, keepdims=True)).astype(BF)
    acc = _dot(p, v)
    return (acc[:, :LANES] / acc[:, LANES:LANES + 1]).astype(BF)


def _attn_kernel(q_ref, k_ref, v_ref, o_ref):
    k, v = k_ref[0], v_ref[0]
    for r0 in range(0, q_ref.shape[1], ATTN_ROWS):
        o_ref[0, r0:r0 + ATTN_ROWS] = _softmax_pv(q_ref[0, r0:r0 + ATTN_ROWS], k, v)


def _attn_heads_kernel(q_ref, k_ref, v_ref, o_ref, *, n_heads):
    for hh in range(n_heads):
        hk = slice(hh * 2 * LANES, (hh + 1) * 2 * LANES)
        o_ref[0, :, hh * LANES:(hh + 1) * LANES] = _softmax_pv(q_ref[0, :, hk], k_ref[0, :, hk], v_ref[0, :, hk])


def _attention(q, k, v, n_heads):
    b, l, _ = q.shape
    s = k.shape[1]
    hw = 2 * LANES
    out_shape = jax.ShapeDtypeStruct((b, l, n_heads * LANES), BF)
    if l * n_heads <= 4096:
        return pl.pallas_call(
            functools.partial(_attn_heads_kernel, n_heads=n_heads),
            grid=(b,),
            in_specs=[pl.BlockSpec((1, l, n_heads * hw), lambda bi: (bi, 0, 0)),
                      pl.BlockSpec((1, s, n_heads * hw), lambda bi: (bi, 0, 0)),
                      pl.BlockSpec((1, s, n_heads * hw), lambda bi: (bi, 0, 0))],
            out_specs=pl.BlockSpec((1, l, n_heads * LANES), lambda bi: (bi, 0, 0)),
            out_shape=out_shape,
            compiler_params=_cparams(("parallel",)),
            name="mla_attention_ctx",
        )(q, k, v)
    tq = min(2048, l)
    return pl.pallas_call(
        _attn_kernel,
        grid=(b, n_heads, l // tq),
        in_specs=[pl.BlockSpec((1, tq, hw), lambda bi, h, i: (bi, i, h)),
                  pl.BlockSpec((1, s, hw), lambda bi, h, i: (bi, 0, h)),
                  pl.BlockSpec((1, s, hw), lambda bi, h, i: (bi, 0, h))],
        out_specs=pl.BlockSpec((1, tq, LANES), lambda bi, h, i: (bi, i, h)),
        out_shape=out_shape,
        compiler_params=_cparams(("parallel", "parallel", "arbitrary")),
        name="mla_attention",
    )(q, k, v)


def _ret_kernel(*refs, with_state, n_chunks, c_len, n_hp, k_scale):
    if with_state:
        (q_ref, k_ref, v_ref, g_ref, dec_ref, ng_ref, s0f_ref, s0b_ref,
         y_ref, dm_ref, kvf_ref, kvb_ref, sf_ref, sb_ref) = refs
    else:
        (q_ref, k_ref, v_ref, g_ref, dec_ref, ng_ref,
         y_ref, ff_ref, fb_ref, dm_ref, kvf_ref, kvb_ref, sf_ref, sb_ref) = refs
    reps = c_len // LANES

    @pl.when(pl.program_id(1) == 0)
    def _():
        di = (lax.broadcasted_iota(jnp.int32, (c_len, c_len), 0)
              - lax.broadcasted_iota(jnp.int32, (c_len, c_len), 1)).astype(F32)
        for hp in range(n_hp):
            lgf = jnp.tile(-jnp.exp(dec_ref[0, hp]), (1, reps))
            lgb = jnp.tile(-jnp.exp(dec_ref[1, hp]), (1, reps))
            dm_ref[hp] = jnp.where(di > 0, jnp.exp(lgf * jnp.maximum(di, 0.0)),
                                   jnp.where(di < 0, jnp.exp(lgb * jnp.maximum(-di, 0.0)), 2.0))

    ii = lax.broadcasted_iota(jnp.int32, (c_len, LANES), 0).astype(F32)
    need_state = with_state or n_chunks > 1
    for hp in range(n_hp):
        cs = slice(hp * LANES, (hp + 1) * LANES)
        lgf = -jnp.exp(dec_ref[0, hp])
        lgb = -jnp.exp(dec_ref[1, hp])
        dk_f = jnp.exp(lgf * (c_len - 1.0 - ii))
        dk_b = jnp.exp(lgb * ii)
        ds_f = jnp.exp(lgf * c_len)
        ds_b = jnp.exp(lgb * c_len)

        for c in range(n_chunks):
            rows = slice(c * c_len, (c + 1) * c_len)
            kc = k_ref[0, rows, cs] * k_scale
            vc = v_ref[0, rows, cs].astype(BF)
            kvf_ref[c] = _dot((kc * dk_f).T.astype(BF), vc)
            kvb_ref[c] = _dot((kc * dk_b).T.astype(BF), vc)

        if with_state:
            s_f, s_b = s0f_ref[0, hp], s0b_ref[0, hp]
        else:
            s_f = s_b = jnp.zeros((LANES, LANES), F32)
        for c in range(n_chunks):
            sf_ref[c] = s_f
            s_f = ds_f * s_f + kvf_ref[c]
        for c in reversed(range(n_chunks)):
            sb_ref[c] = s_b
            s_b = ds_b * s_b + kvb_ref[c]
        if not with_state:
            ff_ref[0, hp] = s_f
            fb_ref[0, hp] = s_b

        ng = ng_ref[:, cs]
        dq_f = jnp.exp(lgf * (ii + 1.0))
        dq_b = jnp.exp(lgb * (c_len - ii))
        for c in range(n_chunks):
            rows = slice(c * c_len, (c + 1) * c_len)
            qc = q_ref[0, rows, cs]
            kc = (k_ref[0, rows, cs] * k_scale).astype(BF)
            vc = v_ref[0, rows, cs].astype(BF)
            att = _dot_nt(qc.astype(BF), kc) * dm_ref[hp]
            o = _dot(att.astype(BF), vc)
            if need_state:
                o = (o + _dot((qc * dq_f).astype(BF), sf_ref[c].astype(BF))
                     + _dot((qc * dq_b).astype(BF), sb_ref[c].astype(BF)))
            mu = jnp.mean(o, axis=-1, keepdims=True)
            oc = o - mu
            var = jnp.mean(oc * oc, axis=-1, keepdims=True)
            on = oc * lax.rsqrt(var + NORM_EPS)
            gg = g_ref[0, rows, cs]
            y_ref[0, rows, cs] = (gg * _sigmoid(gg) * (on * ng)).astype(BF)


def _retention(zs, col0, dec, ng, s0f, s0b, n_heads):
    b, l, _ = zs.shape
    c_len = min(RET_CHUNK_MAX, l)
    nc = l // c_len
    with_state = s0f is not None
    n_hp = 1 if nc > 1 else n_heads
    hw = n_hp * LANES
    cb = col0 // hw

    def zspec(seg):
        return pl.BlockSpec((1, l, hw), lambda hg, bi: (bi, 0, cb + seg * (n_heads // n_hp) + hg))

    st_spec = pl.BlockSpec((1, n_hp, LANES, LANES), lambda hg, bi: (bi, hg, 0, 0))
    in_specs = [zspec(0), zspec(1), zspec(2), zspec(3),
                pl.BlockSpec((2, n_hp, 1, LANES), lambda hg, bi: (0, hg, 0, 0)),
                pl.BlockSpec((1, hw), lambda hg, bi: (0, hg))]
    args = [zs, zs, zs, zs, dec, ng]
    y_spec = pl.BlockSpec((1, l, hw), lambda hg, bi: (bi, 0, hg))
    y_shape = jax.ShapeDtypeStruct((b, l, n_heads * LANES), BF)
    if with_state:
        in_specs += [st_spec, st_spec]
        args += [s0f, s0b]
        out_specs, out_shape = y_spec, y_shape
    else:
        st_shape = jax.ShapeDtypeStruct((b, n_heads, LANES, LANES), F32)
        out_specs, out_shape = [y_spec, st_spec, st_spec], [y_shape, st_shape, st_shape]
    scratch = ([pltpu.VMEM((n_hp, c_len, c_len), F32)]
               + [pltpu.VMEM((nc, LANES, LANES), F32) for _ in range(4)])
    return pl.pallas_call(
        functools.partial(_ret_kernel, with_state=with_state, n_chunks=nc, c_len=c_len, n_hp=n_hp,
                          k_scale=float(LANES) ** -0.5),
        grid=(n_heads // n_hp, b),
        in_specs=in_specs, out_specs=out_specs, out_shape=out_shape,
        scratch_shapes=scratch,
        compiler_params=_cparams(("arbitrary", "arbitrary")),
        name="retention",
    )(*args)


def _s5_prep_kernel(lre_ref, lim_ref, ldt_ref, btre_ref, btim_ref, cre_ref, cim_ref,
                    toep_ref, wd_ref, wo_ref, av_ref, *, ch, p, seg_chunks):
    t = S5_T
    gw = (LANES // ch) * p
    btre, btim, cre, cim = btre_ref[...], btim_ref[...], cre_ref[...], cim_ref[...]
    r1 = lax.broadcasted_iota(jnp.int32, (LANES, LANES), 0)
    c1 = lax.broadcasted_iota(jnp.int32, (LANES, LANES), 1)
    same_group = (r1 // ch) == (c1 // ch)
    lane_valid = c1 < p
    r5 = lax.broadcasted_iota(jnp.int32, (LANES, gw), 0)
    c5 = lax.broadcasted_iota(jnp.int32, (LANES, gw), 1)
    own_state = (r5 // ch) == (c5 // p)

    def spread(x):
        x2 = x + pltpu.roll(x, LANES // 2, 1)
        return jnp.where(own_state, jnp.concatenate([x2] * (gw // LANES), axis=1), 0.0)

    xs, ys, lags = [], [], []
    for d in range(2):
        lr, li, dt = lre_ref[d], lim_ref[d], jnp.exp(ldt_ref[d])
        ar, ai = lr * dt, li * dt
        mag = jnp.exp(ar)
        abr, abi = mag * jnp.cos(ai), mag * jnp.sin(ai)
        nr, ni = abr - 1.0, abi
        den = lr * lr + li * li
        cr = (nr * lr + ni * li) / den
        ci = (ni * lr - nr * li) / den
        pr, pi = jnp.ones_like(lr), jnp.zeros_like(lr)
        xd, yd, gd = [], [], []
        for kk in range(t + 1):
            if kk < t:
                wr, wi = cr * pr - ci * pi, cr * pi + ci * pr
                xr, xi = wr * btre - wi * btim, wr * btim + wi * btre
                xd.append((xr, xi))
                hi = lax.Precision.HIGHEST
                gd.append(jnp.where(same_group, _dot_nt(xr, cre, hi) - _dot_nt(xi, cim, hi), 0.0))
            if kk >= 1:
                yd.append((cre * pr - cim * pi, -(cre * pi + cim * pr)))
            if kk == t:
                qr, qi = jnp.where(lane_valid, pr, 0.0), jnp.where(lane_valid, pi, 0.0)
            pr, pi = pr * abr - pi * abi, pr * abi + pi * abr
        xs.append(xd)
        ys.append(yd)
        lags.append(gd)
        inv = 1.0 / ch
        av_ref[0, 2 * d:2 * d + 1] = jnp.sum(spread(qr), axis=0, keepdims=True) * inv
        av_ref[0, 2 * d + 1:2 * d + 2] = jnp.sum(spread(qi), axis=0, keepdims=True) * inv
        for _ in range(seg_chunks.bit_length() - 1):
            qr, qi = qr * qr - qi * qi, 2.0 * qr * qi
        av_ref[0, 4 + 2 * d:5 + 2 * d] = jnp.sum(spread(qr), axis=0, keepdims=True) * inv
        av_ref[0, 5 + 2 * d:6 + 2 * d] = jnp.sum(spread(qi), axis=0, keepdims=True) * inv

    for ti in range(t):
        for to in range(t):
            lag = to - ti
            tile = lags[0][lag] if lag > 0 else (lags[1][-lag] if lag < 0 else lags[0][0] + lags[1][0])
            toep_ref[0, ti * LANES:(ti + 1) * LANES, to * LANES:(to + 1) * LANES] = tile.astype(BF)
    for tt in range(t):
        rows = slice(tt * LANES, (tt + 1) * LANES)
        xf, xb = xs[0][t - 1 - tt], xs[1][tt]
        yf, yb = ys[0][tt], ys[1][t - 1 - tt]
        for m, (xv, yv) in enumerate(((xf[0], yf[0]), (xf[1], yf[1]), (xb[0], yb[0]), (xb[1], yb[1]))):
            wd_ref[0, rows, m * gw:(m + 1) * gw] = spread(xv).astype(BF)
            wo_ref[0, m * gw:(m + 1) * gw, rows] = spread(yv).T.astype(BF)


def _s5_prep(lam_re, lam_im, log_dt, b_re, b_im, c_re, c_im):
    _, g, p = lam_re.shape
    ch = b_re.shape[-1]
    assert p == LANES // 2 and LANES % ch == 0
    nb = g * ch // LANES
    gw = (LANES // ch) * p
    t = S5_T
    seg_chunks = S5_SEG // t
    assert seg_chunks & (seg_chunks - 1) == 0

    def prow(a, padval):
        return jnp.pad(jnp.repeat(a, ch, axis=1), ((0, 0), (0, 0), (0, LANES - p)), constant_values=padval)

    def wrow(a):
        return jnp.pad(a.reshape(g * ch, p), ((0, 0), (0, LANES - p)))

    lre = prow(lam_re, -0.5)
    lim = prow(lam_im, 0.0)
    ldt = prow(jnp.broadcast_to(log_dt[..., None], lam_re.shape), 0.0)
    lspec = pl.BlockSpec((2, LANES, LANES), lambda i: (0, i, 0))
    wspec = pl.BlockSpec((LANES, LANES), lambda i: (i, 0))
    return pl.pallas_call(
        functools.partial(_s5_prep_kernel, ch=ch, p=p, seg_chunks=seg_chunks),
        grid=(nb,),
        in_specs=[lspec, lspec, lspec, wspec, wspec, wspec, wspec],
        out_specs=[pl.BlockSpec((1, t * LANES, t * LANES), lambda i: (i, 0, 0)),
                   pl.BlockSpec((1, t * LANES, 4 * gw), lambda i: (i, 0, 0)),
                   pl.BlockSpec((1, 4 * gw, t * LANES), lambda i: (i, 0, 0)),
                   pl.BlockSpec((1, 8, gw), lambda i: (i, 0, 0))],
        out_shape=[jax.ShapeDtypeStruct((nb, t * LANES, t * LANES), BF),
                   jax.ShapeDtypeStruct((nb, t * LANES, 4 * gw), BF),
                   jax.ShapeDtypeStruct((nb, 4 * gw, t * LANES), BF),
                   jax.ShapeDtypeStruct((nb, 8, gw), F32)],
        compiler_params=_cparams(("parallel",)),
        name="s5_prep",
    )(lre, lim, ldt, wrow(jnp.swapaxes(b_re, 1, 2)), wrow(jnp.swapaxes(b_im, 1, 2)), wrow(c_re), wrow(c_im))


def _gelu(x):
    return 0.5 * x * (1.0 + jnp.tanh(0.7978845608028654 * (x + 0.044715 * (x * x * x))))


def _s5_kernel(*refs, chain):
    if chain:
        (u_ref, toep_ref, wd_ref, wo_ref, a_ref, dsk_ref, s0_ref,
         y_ref, lhs_ref, d_ref, s_ref, xs_ref) = refs
    else:
        (u_ref, toep_ref, wd_ref, wo_ref, a_ref, dsk_ref,
         y_ref, fin_ref, lhs_ref, d_ref, s_ref) = refs
    nc, t, pb, _ = u_ref.shape
    rows = nc * pb
    w = a_ref.shape[-1]
    for tt in range(t):
        lhs_ref[:, tt * LANES:(tt + 1) * LANES] = u_ref[:, tt].reshape(rows, LANES).astype(BF)
    d_ref[...] = _dot(lhs_ref[...], wd_ref[0])
    av = a_ref[0]
    afr, afi, abr, abi = av[0:1], av[1:2], av[2:3], av[3:4]

    def scan(init, store):
        sfr, sfi, sbr, sbi = init
        for c in range(nc):
            rf = slice(c * pb, (c + 1) * pb)
            rb = slice((nc - 1 - c) * pb, (nc - c) * pb)
            if store:
                s_ref[rf, 0:w] = sfr.astype(BF)
                s_ref[rf, w:2 * w] = sfi.astype(BF)
                s_ref[rb, 2 * w:3 * w] = sbr.astype(BF)
                s_ref[rb, 3 * w:4 * w] = sbi.astype(BF)
            sfr, sfi = (afr * sfr - afi * sfi + d_ref[rf, 0:w],
                        afr * sfi + afi * sfr + d_ref[rf, w:2 * w])
            sbr, sbi = (abr * sbr - abi * sbi + d_ref[rb, 2 * w:3 * w],
                        abr * sbi + abi * sbr + d_ref[rb, 3 * w:4 * w])
        return sfr, sfi, sbr, sbi

    zero = jnp.zeros((pb, w), F32)
    if not chain:
        fin = scan((zero, zero, zero, zero), True)
        for m in range(4):
            fin_ref[0, :, m * w:(m + 1) * w] = fin[m]
    else:
        efr, efi, ebr, ebi = scan((zero, zero, zero, zero), False)
        s0 = s0_ref[0, 0]
        xr, xi = s0[0:1], s0[1:2]
        for j in range(pb):
            xs_ref[j:j + 1, 0:w] = xr
            xs_ref[j:j + 1, w:2 * w] = xi
            xr, xi = (av[4:5] * xr - av[5:6] * xi + efr[j:j + 1],
                      av[4:5] * xi + av[5:6] * xr + efi[j:j + 1])
        xr, xi = s0[2:3], s0[3:4]
        for j in reversed(range(pb)):
            xs_ref[j:j + 1, 2 * w:3 * w] = xr
            xs_ref[j:j + 1, 3 * w:4 * w] = xi
            xr, xi = (av[6:7] * xr - av[7:8] * xi + ebr[j:j + 1],
                      av[6:7] * xi + av[7:8] * xr + ebi[j:j + 1])
        scan((xs_ref[:, 0:w], xs_ref[:, w:2 * w], xs_ref[:, 2 * w:3 * w], xs_ref[:, 3 * w:4 * w]), True)

    yv = _dot(lhs_ref[...], toep_ref[0]) + _dot(s_ref[...], wo_ref[0])
    dsk = dsk_ref[...]
    for tt in range(t):
        yt = yv[:, tt * LANES:(tt + 1) * LANES] + dsk * u_ref[:, tt].reshape(rows, LANES)
        y_ref[:, tt] = _gelu(yt).reshape(nc, pb, LANES)


def _s5(u4, toep, wd, wo, av, dskip, s0):
    nc, t, npb, nch = u4.shape
    nb = toep.shape[0]
    chain = s0 is not None
    pb = npb // s0.shape[1] if chain else min(16, npb)
    sw = wd.shape[-1]
    rows = nc * pb
    ublk = pl.BlockSpec((nc, t, pb, LANES), lambda kb, hh: (0, 0, hh, kb))

    def wspec(arr):
        return pl.BlockSpec((1,) + arr.shape[1:], lambda kb, hh: (kb, 0, 0))

    in_specs = [ublk, wspec(toep), wspec(wd), wspec(wo), wspec(av),
                pl.BlockSpec((1, LANES), lambda kb, hh: (0, kb))]
    args = [u4, toep, wd, wo, av, dskip]
    scratch = [pltpu.VMEM((rows, t * LANES), BF), pltpu.VMEM((rows, sw), F32), pltpu.VMEM((rows, sw), BF)]
    if chain:
        in_specs += [pl.BlockSpec((1, 1, 4, sw // 4), lambda kb, hh: (kb, hh, 0, 0))]
        args += [s0]
        out_specs = ublk
        out_shape = jax.ShapeDtypeStruct(u4.shape, F32)
        scratch.append(pltpu.VMEM((pb, sw), F32))
    else:
        out_specs = [ublk, pl.BlockSpec((1, pb, sw), lambda kb, hh: (kb, hh, 0))]
        out_shape = [jax.ShapeDtypeStruct(u4.shape, F32), jax.ShapeDtypeStruct((nb, npb, sw), F32)]
    return pl.pallas_call(
        functools.partial(_s5_kernel, chain=chain),
        grid=(nb, npb // pb),
        in_specs=in_specs, out_specs=out_specs, out_shape=out_shape,
        scratch_shapes=scratch,
        compiler_params=_cparams(("parallel", "parallel")),
        name="s5_chunked",
    )(*args)


def _glu_kernel(y_ref, w_ref, b_ref, o_ref):
    y = y_ref[...]
    o_ref[...] = (y * _sigmoid(_dot(y.astype(BF), w_ref[...]) + b_ref[...])).astype(BF)


def _glu(y, w, layer, b):
    m, n = y.shape
    tm = min(1024, m)
    return pl.pallas_call(
        _glu_kernel,
        grid=(m // tm,),
        in_specs=[pl.BlockSpec((tm, n), lambda i: (i, 0)),
                  pl.BlockSpec((None, n, n), lambda i: (layer, 0, 0)),
                  pl.BlockSpec((1, n), lambda i: (0, 0))],
        out_specs=pl.BlockSpec((tm, n), lambda i: (i, 0)),
        out_shape=jax.ShapeDtypeStruct((m, n), BF),
        compiler_params=_cparams(("parallel",)),
        name="s5_glu",
    )(y, w, b)


def _res_epilogue(acc_ref, x_ref, gate_ref, ng_ref, o_ref, nxt_refs):
    gn = gate_ref[0] * ng_ref[...]
    if nxt_refs is not None:
        ng2_ref, sh2_ref, sc2_ref, h_ref = nxt_refs
        g2 = ng2_ref[...] * (1.0 + sc2_ref[0])
        sh2 = sh2_ref[0]
    rows = 128
    for r0 in range(0, o_ref.shape[0], rows):
        acc = acc_ref[r0:r0 + rows]
        inv = lax.rsqrt(jnp.mean(acc * acc, axis=-1, keepdims=True) + NORM_EPS)
        xn = x_ref[r0:r0 + rows] + (acc * inv) * gn
        o_ref[r0:r0 + rows] = xn
        if nxt_refs is not None:
            inv2 = lax.rsqrt(jnp.mean(xn * xn, axis=-1, keepdims=True) + NORM_EPS)
            h_ref[r0:r0 + rows] = ((xn * inv2) * g2 + sh2).astype(BF)


def _merge_kernel(ya_ref, yb_ref, yc_ref, ga_ref, gb_ref, gc_ref, wb_ref, o_ref):
    acc = (_sigmoid(ga_ref[...].astype(F32)) * _dot(ya_ref[...], wb_ref[0])
           + _sigmoid(gb_ref[...].astype(F32)) * _dot(yb_ref[...], wb_ref[1])
           + _sigmoid(gc_ref[...].astype(F32)) * _dot(yc_ref[...], wb_ref[2]))
    o_ref[...] = acc.astype(BF)


def _merge(ya, yb, yc, zg, wb, layer):
    m, dm = ya.shape
    d = wb.shape[-1]
    tm, tn = min(1024, m), 512
    nt = d // tn
    ysp = pl.BlockSpec((tm, dm), lambda i, j: (i, 0))

    def gsp(br):
        return pl.BlockSpec((tm, tn), lambda i, j: (i, j + br * nt))

    return pl.pallas_call(
        _merge_kernel,
        grid=(m // tm, nt),
        in_specs=[ysp, ysp, ysp, gsp(0), gsp(1), gsp(2),
                  pl.BlockSpec((None, 3, dm, tn), lambda i, j: (layer, 0, 0, j))],
        out_specs=pl.BlockSpec((tm, tn), lambda i, j: (i, j)),
        out_shape=jax.ShapeDtypeStruct((m, d), BF),
        compiler_params=_cparams(("parallel", "parallel")),
        name="branch_merge",
    )(ya, yb, yc, zg, zg, zg, wb)


def _mm_res_kernel(*refs, with_next):
    if with_next:
        a_ref, w_ref, x_ref, gate_ref, ng_ref, ng2_ref, sh2_ref, sc2_ref, o_ref, h_ref, acc_ref = refs
        nxt_refs = (ng2_ref, sh2_ref, sc2_ref, h_ref)
    else:
        a_ref, w_ref, x_ref, gate_ref, ng_ref, o_ref, acc_ref = refs
        nxt_refs = None
    k = pl.program_id(1)

    @pl.when(k == 0)
    def _():
        acc_ref[...] = jnp.zeros_like(acc_ref)

    acc_ref[...] += _dot(a_ref[...], w_ref[...])

    @pl.when(k == pl.num_programs(1) - 1)
    def _():
        _res_epilogue(acc_ref, x_ref, gate_ref, ng_ref, o_ref, nxt_refs)


def _mm_res(a, w, layer, x, mods, gate_chunk, ng, rows_per_mod, nxt=None):
    m, kdim = a.shape
    d = w.shape[-1]
    tm = min(512, m)
    tk = kdim // 4 if kdim > 2048 else kdim // 2
    assert tk % LANES == 0
    in_specs = [pl.BlockSpec((tm, tk), lambda i, k: (i, k)),
                pl.BlockSpec((None, tk, d), lambda i, k: (layer, k, 0)),
                pl.BlockSpec((tm, d), lambda i, k: (i, 0)),
                _mod_spec(gate_chunk, d, rows_per_mod, tm),
                pl.BlockSpec((1, d), lambda i, k: (0, 0))]
    args = [a, w, x, mods, ng]
    x_spec = pl.BlockSpec((tm, d), lambda i, k: (i, 0))
    x_shape = jax.ShapeDtypeStruct((m, d), F32)
    if nxt is None:
        out_specs, out_shape = x_spec, x_shape
    else:
        ng2, mods2, sh_chunk, sc_chunk = nxt
        in_specs += [pl.BlockSpec((1, d), lambda i, k: (0, 0)),
                     _mod_spec(sh_chunk, d, rows_per_mod, tm),
                     _mod_spec(sc_chunk, d, rows_per_mod, tm)]
        args += [ng2, mods2, mods2]
        out_specs = [x_spec, pl.BlockSpec((tm, d), lambda i, k: (i, 0))]
        out_shape = [x_shape, jax.ShapeDtypeStruct((m, d), BF)]
    return pl.pallas_call(
        functools.partial(_mm_res_kernel, with_next=nxt is not None),
        grid=(m // tm, kdim // tk),
        in_specs=in_specs, out_specs=out_specs, out_shape=out_shape,
        scratch_shapes=[pltpu.VMEM((tm, d), F32)],
        compiler_params=_cparams(("parallel", "arbitrary")),
        name="matmul_norm_residual",
    )(*args)


FFN_HALO = 16


def _ffn_up_kernel(hp_ref, h_ref, hn_ref, wv_ref, wg_ref, cwv_ref, cwg_ref, cbv_ref, cbg_ref,
                   o_ref, hb_ref, uv_ref, ug_ref, *, tm, piece, seq_len):
    i, j = pl.program_id(0), pl.program_id(1)
    hl = FFN_HALO
    n_pieces = tm // piece
    stride = piece + hl

    @pl.when(j == 0)
    def _():
        zero = jnp.zeros(hp_ref.shape, BF)
        if n_pieces == 1 and seq_len > tm:
            hb_ref[0:hl] = jnp.where(lax.rem(i * tm, seq_len) != 0, hp_ref[...], zero)
            hb_ref[stride:stride + hl] = jnp.where(lax.rem((i + 1) * tm, seq_len) != 0, hn_ref[...], zero)
        else:
            for pc in range(n_pieces + 1):
                hb_ref[pc * stride:pc * stride + hl] = zero
        for pc in range(n_pieces):
            hb_ref[hl + pc * stride:hl + pc * stride + piece] = h_ref[pc * piece:(pc + 1) * piece]

    hb = hb_ref[...]
    ug_ref[...] = _dot(hb, wg_ref[...])
    uv_ref[...] = _dot(hb, wv_ref[...])
    cwv, cwg, cbv, cbg = cwv_ref[...], cwg_ref[...], cbv_ref[...], cbg_ref[...]
    for pc in range(n_pieces):
        b0 = hl + pc * stride

        def conv(u_ref, cw, cb):
            return (u_ref[b0 - 1:b0 - 1 + piece] * cw[0:1] + u_ref[b0:b0 + piece] * cw[1:2]
                    + u_ref[b0 + 1:b0 + 1 + piece] * cw[2:3] + cb)

        gate = conv(ug_ref, cwg, cbg)
        act = gate * _sigmoid(gate)
        o_ref[pc * piece:(pc + 1) * piece] = (act * conv(uv_ref, cwv, cbv)).astype(BF)


def _ffn_up(h, w_up, layer, conv_w, conv_b, seq_len):
    m, d = h.shape
    f = w_up.shape[-1] // 2
    tm, tf = min(1024, m), 512
    piece = min(seq_len, tm)
    assert tm % piece == 0 and seq_len % piece == 0
    nf = f // tf
    hl = FFN_HALO
    nblk = m // hl
    rows = (tm // piece) * (piece + hl) + hl

    return pl.pallas_call(
        functools.partial(_ffn_up_kernel, tm=tm, piece=piece, seq_len=seq_len),
        grid=(m // tm, nf),
        in_specs=[pl.BlockSpec((hl, d), lambda i, j: (jnp.maximum(i * (tm // hl) - 1, 0), 0)),
                  pl.BlockSpec((tm, d), lambda i, j: (i, 0)),
                  pl.BlockSpec((hl, d), lambda i, j: (jnp.minimum((i + 1) * (tm // hl), nblk - 1), 0)),
                  pl.BlockSpec((None, d, tf), lambda i, j: (layer, 0, j)),
                  pl.BlockSpec((None, d, tf), lambda i, j: (layer, 0, j + nf)),
                  pl.BlockSpec((3, tf), lambda i, j: (0, j)),
                  pl.BlockSpec((3, tf), lambda i, j: (0, j + nf)),
                  pl.BlockSpec((1, tf), lambda i, j: (0, j)),
                  pl.BlockSpec((1, tf), lambda i, j: (0, j + nf))],
        out_specs=pl.BlockSpec((tm, tf), lambda i, j: (i, j)),
        out_shape=jax.ShapeDtypeStruct((m, f), BF),
        scratch_shapes=[pltpu.VMEM((rows, d), BF), pltpu.VMEM((rows, tf), F32), pltpu.VMEM((rows, tf), F32)],
        compiler_params=_cparams(("parallel", "arbitrary")),
        name="ffn_up_conv",
    )(h, h, h, w_up, w_up, conv_w, conv_w, conv_b, conv_b)


def _pair_swap(w):
    r = w.reshape(w.shape[:-1] + (w.shape[-1] // 2, 2))
    return jnp.stack([-r[..., 1], r[..., 0]], axis=-1).reshape(w.shape)


def _rope_tables(n_tokens, dr):
    rows = n_tokens // GRID_W
    row = jnp.broadcast_to(jnp.arange(rows, dtype=F32)[:, None], (rows, GRID_W)).reshape(-1)
    col = jnp.broadcast_to(jnp.arange(GRID_W, dtype=F32)[None, :], (rows, GRID_W)).reshape(-1)
    nf = dr // 4
    inv = ROPE_BASE ** (-jnp.arange(nf, dtype=F32) / nf)
    ang = jnp.concatenate([row[:, None] * inv, col[:, None] * inv], axis=-1)
    ang = jnp.repeat(ang, 2, axis=-1)
    pad = jnp.zeros((n_tokens, LANES - dr), F32)
    return jnp.concatenate([jnp.cos(ang), pad], axis=-1), jnp.concatenate([jnp.sin(ang), pad], axis=-1)


def kernel(x_prompt, x_sample, cache_mla_ckv, cache_mla_krope, state_ret_fwd, state_ret_bwd, state_s5_fwd, state_s5_bwd, c, c_ctx, ada_w, ada_b, norm_g, w_in, s5_lam_re, s5_lam_im, s5_log_dt, s5_b_re, s5_b_im, s5_c_re, s5_c_im, s5_d, s5_glu_w, s5_glu_b, ret_decay, ret_norm_g, mla_q_norm, mla_kv_norm, mla_w_uq, mla_w_uk, mla_w_uv, w_branch, w_out, ffn_w_up, ffn_conv_w, ffn_conv_b, ffn_w_down):
    bsz, seq, d = x_prompt.shape
    dbsz, dseq, _ = x_sample.shape
    depth = ada_w.shape[0]
    dm = d // 2
    n_heads = ret_decay.shape[-1]
    dr = cache_mla_krope.shape[-1]
    q_lora, kv_lora = mla_w_uq.shape[1], mla_w_uk.shape[1]
    g5, p5, ch5 = s5_b_re.shape[1:]
    assert dm == n_heads * LANES and dr == LANES // 2 and mla_w_uk.shape[-1] == dm
    assert seq == S5_SEG and dseq % S5_SEG == 0
    qscale = float(LANES + dr) ** -0.5 * 1.4426950408889634

    cvecs = jnp.concatenate([c_ctx[None], c, jnp.zeros((8 - 1 - dbsz, d), F32)], axis=0)
    mods_all = _ada_mods(cvecs, ada_w, ada_b)

    cos_l, sin_l = _rope_tables(dseq, dr)
    ones_tab = jnp.concatenate([jnp.ones((256, dr), F32), jnp.zeros((256, LANES - dr), F32)], axis=-1)
    zeros_tab = jnp.zeros((256, LANES), F32)

    cw = np.cumsum([0, dm, dm, dm, dm, dm, q_lora, kv_lora, dr, d, d, d])
    w_a = jnp.concatenate([w_in[:, :, cw[8]:cw[11]], w_in[:, :, cw[0]:cw[5]]], axis=2).astype(BF)
    w_kr = w_in[:, :, cw[7]:cw[8]]
    w_m = jnp.concatenate([w_in[:, :, cw[5]:cw[7]], w_kr, _pair_swap(w_kr)], axis=2).astype(BF)
    uq = mla_w_uq.reshape(depth, q_lora, n_heads, LANES + dr)
    wq_ext = jnp.concatenate([uq[..., :LANES], uq[..., LANES:], _pair_swap(uq[..., LANES:])],
                             axis=-1).reshape(depth, q_lora, n_heads * 2 * LANES).astype(BF)
    wkv = jnp.concatenate([mla_w_uk, mla_w_uv], axis=2).astype(BF)
    glu_w = s5_glu_w.astype(BF)
    wb = w_branch.astype(BF)
    wo_ = w_out.astype(BF)
    wup = ffn_w_up.astype(BF)
    wdn = ffn_w_down.astype(BF)

    outs = {k: [] for k in ("ckv", "krope", "retf", "retb", "s5f", "s5b")}
    nb5 = g5 * ch5 // LANES
    gb5 = LANES // ch5

    def group_mods(l, ctx):
        if ctx:
            return mods_all[l, 0:1].reshape(1, 1, -1), bsz * seq
        return mods_all[l, 1:1 + dbsz].reshape(dbsz, 1, -1), dseq

    xres = [x_prompt.reshape(bsz * seq, d), x_sample.reshape(dbsz * dseq, d)]
    hres = []
    for grp in range(2):
        mods, rpm = group_mods(0, grp == 0)
        hres.append(_norm_mod(xres[grp], mods, norm_g[0, 0].reshape(1, d), rpm))

    for l in range(depth):
        qg = mla_q_norm[l].reshape(1, q_lora)
        kvg = mla_kv_norm[l].reshape(1, kv_lora)
        ng = norm_g[l].reshape(4, 1, d)
        glu_b = s5_glu_b[l].reshape(1, dm)
        conv_w = ffn_conv_w[l]
        conv_b = ffn_conv_b[l].reshape(1, -1)
        dec = jnp.broadcast_to(ret_decay[l].reshape(2, n_heads, 1, 1), (2, n_heads, 1, LANES))
        rng = ret_norm_g[l].reshape(1, dm)
        dskip = s5_d[l].reshape(1, dm)
        toep, wd, wo5, av = _s5_prep(s5_lam_re[l], s5_lam_im[l], s5_log_dt[l], s5_b_re[l], s5_b_im[l],
                                     s5_c_re[l], s5_c_im[l])

        for grp in range(2):
            ctx = grp == 0
            x, h = xres[grp], hres[grp]
            nbat, slen = (bsz, seq) if ctx else (dbsz, dseq)
            m = nbat * slen
            mods, rpm = group_mods(l, ctx)

            zg, zu, zs = _inproj(h, w_a, l, 3 * d, dm)

            nseg = slen // S5_SEG
            npb = nbat * nseg
            u = zu.reshape(npb, S5_SEG, dm).transpose(1, 0, 2)
            u4 = u.reshape(S5_SEG // S5_T, S5_T, npb, dm)
            if ctx:
                y4, fin = _s5(u4, toep, wd, wo5, av, dskip, None)
                fin = fin.reshape(nb5, npb, 4, gb5, p5).transpose(1, 2, 0, 3, 4).reshape(npb, 4, g5, p5)
                outs["s5f"].append(jnp.stack([fin[:, 0], fin[:, 1]], axis=-1))
                outs["s5b"].append(jnp.stack([fin[:, 2], fin[:, 3]], axis=-1))
            else:
                s0 = jnp.stack([state_s5_fwd[:, l, :, :, 0], state_s5_fwd[:, l, :, :, 1],
                                state_s5_bwd[:, l, :, :, 0], state_s5_bwd[:, l, :, :, 1]], axis=1)
                s0 = s0.reshape(dbsz, 4, nb5, gb5 * p5).transpose(2, 0, 1, 3)
                y4 = _s5(u4, toep, wd, wo5, av, dskip, s0)
            ya_tm = _glu(y4.reshape(S5_SEG * npb, dm), glu_w, l, glu_b)
            ya = ya_tm.reshape(S5_SEG, npb, dm).transpose(1, 0, 2).reshape(m, dm)

            zs3 = zs.reshape(nbat, slen, -1)
            if ctx:
                yb, rf, rb = _retention(zs3, 0, dec, rng, None, None, n_heads)
                outs["retf"].append(rf)
                outs["retb"].append(rb)
            else:
                yb = _retention(zs3, 0, dec, rng, state_ret_fwd[:, l], state_ret_bwd[:, l], n_heads)
            yb = yb.reshape(m, dm)

            if ctx:
                q, k, v, ckv, kr = _mla_proj(h, w_m[l], qg, kvg, wq_ext[l], wkv[l], ones_tab, zeros_tab,
                                             n_heads, qscale)
                outs["ckv"].append(ckv.reshape(nbat, slen, kv_lora))
                outs["krope"].append(kr[:, :dr].reshape(nbat, slen, dr))
                k, v = k.reshape(nbat, slen, -1), v.reshape(nbat, slen, -1)
            else:
                q, k, v, _, _ = _mla_proj(h, w_m[l], qg, kvg, wq_ext[l], wkv[l], cos_l, sin_l, n_heads, qscale)
                past = cache_mla_ckv.shape[2]
                kr_pad = jnp.pad(cache_mla_krope[:, l], ((0, 0), (0, 0), (0, LANES - dr)))
                kc, vc = _cache_kv(cache_mla_ckv[:, l].reshape(nbat * past, kv_lora),
                                   kr_pad.reshape(nbat * past, LANES), wkv[l], n_heads)
                k = jnp.concatenate([k.reshape(nbat, slen, -1), kc.reshape(nbat, past, -1)], axis=1)
                v = jnp.concatenate([v.reshape(nbat, slen, -1), vc.reshape(nbat, past, -1)], axis=1)
            yc = _attention(q.reshape(nbat, slen, -1), k, v, n_heads).reshape(m, dm)

            merged = _merge(ya, yb, yc, zg, wb, l)
            x, h = _mm_res(merged, wo_, l, x, mods, 2, ng[1], rpm, nxt=(ng[2], mods, 3, 4))
            act = _ffn_up(h, wup, l, conv_w, conv_b, slen)
            if l + 1 < depth:
                x, h = _mm_res(act, wdn, l, x, mods, 5, ng[3], rpm,
                               nxt=(norm_g[l + 1, 0].reshape(1, d), group_mods(l + 1, ctx)[0], 0, 1))
            else:
                x, h = _mm_res(act, wdn, l, x, mods, 5, ng[3], rpm), None
            xres[grp], hres[grp] = x, h

    return (xres[0].reshape(bsz, seq, d), xres[1].reshape(dbsz, dseq, d),
            jnp.stack(outs["ckv"], axis=1), jnp.stack(outs["krope"], axis=1),
            jnp.stack(outs["retf"], axis=1), jnp.stack(outs["retb"], axis=1),
            jnp.stack(outs["s5f"], axis=1), jnp.stack(outs["s5b"], axis=1))
```

```python
import functools

import numpy as np
import jax
import jax.numpy as jnp
from jax import lax
from jax.experimental import pallas as pl
from jax.experimental.pallas import tpu as pltpu

F32 = jnp.float32
BF = jnp.bfloat16

NORM_EPS = 1e-6
ROPE_BASE = 10000.0
GRID_W = 64
LANES = 128
S5_T = 8
S5_SEG = 256
RET_CHUNK_MAX = 512
VMEM_LIMIT = 56 << 20


def _cparams(sem):
    return pltpu.CompilerParams(dimension_semantics=sem, vmem_limit_bytes=VMEM_LIMIT)


def _dot(a, b):
    return jnp.dot(a, b, preferred_element_type=F32)


def _dot_nt(a, b, precision=None):
    return lax.dot_general(a, b, (((1,), (1,)), ((), ())), precision=precision, preferred_element_type=F32)


def _rms(x, g):
    return x * lax.rsqrt(jnp.mean(x * x, axis=-1, keepdims=True) + NORM_EPS) * g


def _rms_mod(x, g, sc, sh):
    return _rms(x, g) * (1.0 + sc) + sh


def _sigmoid(x):
    return 1.0 / (1.0 + jnp.exp(-x))


def _mod_spec(chunk, d, rows_per_mod, tm):
    return pl.BlockSpec((1, 1, d), lambda i, *_: ((i * tm) // rows_per_mod, 0, chunk))


def _ada_kernel(c_ref, w_ref, b_ref, o_ref):
    c = c_ref[...]
    s = (c * _sigmoid(c)).astype(BF)
    o_ref[0] = _dot(s, w_ref[0].astype(BF)) + b_ref[0]


def _ada_mods(cvecs, ada_w, ada_b):
    depth, d, n = ada_w.shape
    tn = 1024
    return pl.pallas_call(
        _ada_kernel,
        grid=(depth, n // tn),
        in_specs=[pl.BlockSpec((8, d), lambda l, j: (0, 0)),
                  pl.BlockSpec((1, d, tn), lambda l, j: (l, 0, j)),
                  pl.BlockSpec((1, 1, tn), lambda l, j: (l, 0, j))],
        out_specs=pl.BlockSpec((1, 8, tn), lambda l, j: (l, 0, j)),
        out_shape=jax.ShapeDtypeStruct((depth, 8, n), F32),
        compiler_params=_cparams(("parallel", "parallel")),
        name="ada_mods",
    )(cvecs, ada_w, ada_b.reshape(depth, 1, n))


def _norm_mod_kernel(x_ref, g_ref, sh_ref, sc_ref, h_ref):
    h_ref[...] = _rms_mod(x_ref[...], g_ref[...], sc_ref[0], sh_ref[0]).astype(BF)


def _norm_mod(x, mods, ng0, rows_per_mod):
    m, d = x.shape
    tm = min(1024, m)
    return pl.pallas_call(
        _norm_mod_kernel,
        grid=(m // tm,),
        in_specs=[pl.BlockSpec((tm, d), lambda i: (i, 0)),
                  pl.BlockSpec((1, d), lambda i: (0, 0)),
                  _mod_spec(0, d, rows_per_mod, tm),
                  _mod_spec(1, d, rows_per_mod, tm)],
        out_specs=pl.BlockSpec((tm, d), lambda i: (i, 0)),
        out_shape=jax.ShapeDtypeStruct((m, d), BF),
        compiler_params=_cparams(("parallel",)),
        name="norm_mod",
    )(x, ng0, mods, mods)


def _inproj_kernel(h_ref, w_ref, og_ref, ou_ref, os_ref, *, n_gate_tiles, n_u_tiles):
    j = pl.program_id(1)

    @pl.when(j < n_gate_tiles)
    def _():
        og_ref[...] = _dot(h_ref[...], w_ref[...]).astype(BF)

    @pl.when((j >= n_gate_tiles) & (j < n_gate_tiles + n_u_tiles))
    def _():
        ou_ref[...] = _dot(h_ref[...], w_ref[...])

    @pl.when(j >= n_gate_tiles + n_u_tiles)
    def _():
        os_ref[...] = _dot(h_ref[...], w_ref[...])


def _inproj(h, w_a, layer, n_gate, n_u):
    m, d = h.shape
    n = w_a.shape[-1]
    tm, tn = min(1024, m), 1024
    ngt, nut = n_gate // tn, n_u // tn
    nst = (n - n_gate - n_u) // tn
    return pl.pallas_call(
        functools.partial(_inproj_kernel, n_gate_tiles=ngt, n_u_tiles=nut),
        grid=(m // tm, ngt + nut + nst),
        in_specs=[pl.BlockSpec((tm, d), lambda i, j: (i, 0)),
                  pl.BlockSpec((None, d, tn), lambda i, j: (layer, 0, j))],
        out_specs=[pl.BlockSpec((tm, tn), lambda i, j: (i, jnp.minimum(j, ngt - 1))),
                   pl.BlockSpec((tm, tn), lambda i, j: (i, jnp.clip(j - ngt, 0, nut - 1))),
                   pl.BlockSpec((tm, tn), lambda i, j: (i, jnp.maximum(j - ngt - nut, 0)))],
        out_shape=[jax.ShapeDtypeStruct((m, n_gate), BF),
                   jax.ShapeDtypeStruct((m, n_u), F32),
                   jax.ShapeDtypeStruct((m, n - n_gate - n_u), F32)],
        compiler_params=_cparams(("parallel", "arbitrary")),
        name="in_proj",
    )(h, w_a)


def _rope(seg, cosp, sinp):
    return seg * cosp + pltpu.roll(seg, LANES // 2, 1) * sinp


def _ones_column(rows):
    lane = lax.broadcasted_iota(jnp.int32, (rows, LANES), 1)
    return jnp.where(lane == 0, 1.0, 0.0).astype(BF)


def _store_kv(kv, krot, k_ref, v_ref, n_heads):
    hv = n_heads * LANES
    ones = _ones_column(kv.shape[0])
    for hh in range(n_heads):
        b0 = hh * 2 * LANES
        k_ref[:, b0:b0 + LANES] = kv[:, hh * LANES:(hh + 1) * LANES].astype(BF)
        k_ref[:, b0 + LANES:b0 + 2 * LANES] = krot
        v_ref[:, b0:b0 + LANES] = kv[:, hv + hh * LANES:hv + (hh + 1) * LANES].astype(BF)
        v_ref[:, b0 + LANES:b0 + 2 * LANES] = ones


def _mla_proj_kernel(h_ref, wm_ref, qg_ref, kvg_ref, wq_ref, wkv_ref, cos_ref, sin_ref,
                     q_ref, k_ref, v_ref, ckv_ref, kr_ref, *, q_lora, kv_lora, n_heads, scale):
    z = _dot(h_ref[...], wm_ref[...])
    cosp, sinp = cos_ref[...], sin_ref[...]

    cqn = _rms(z[:, :q_lora], qg_ref[...]).astype(BF)
    qraw = _dot(cqn, wq_ref[...])
    for hh in range(n_heads):
        b0 = hh * 2 * LANES
        q_ref[:, b0:b0 + LANES] = (qraw[:, b0:b0 + LANES] * scale).astype(BF)
        q_ref[:, b0 + LANES:b0 + 2 * LANES] = (_rope(qraw[:, b0 + LANES:b0 + 2 * LANES], cosp, sinp) * scale).astype(BF)

    ckv = _rms(z[:, q_lora:q_lora + kv_lora], kvg_ref[...])
    ckv_ref[...] = ckv
    kv = _dot(ckv.astype(BF), wkv_ref[...])
    kr = z[:, q_lora + kv_lora:]
    kr_ref[...] = kr
    _store_kv(kv, _rope(kr, cosp, sinp).astype(BF), k_ref, v_ref, n_heads)


def _mla_proj(h, w_m, qg, kvg, wq_ext, wkv, cosp, sinp, n_heads, scale):
    m, d = h.shape
    q_lora, kv_lora = wq_ext.shape[0], wkv.shape[0]
    tm = min(256, m)
    tab_tiles = cosp.shape[0] // tm
    c2 = lambda i: (0, 0)
    return pl.pallas_call(
        functools.partial(_mla_proj_kernel, q_lora=q_lora, kv_lora=kv_lora, n_heads=n_heads, scale=scale),
        grid=(m // tm,),
        in_specs=[pl.BlockSpec((tm, d), lambda i: (i, 0)),
                  pl.BlockSpec(w_m.shape, c2),
                  pl.BlockSpec((1, q_lora), c2),
                  pl.BlockSpec((1, kv_lora), c2),
                  pl.BlockSpec(wq_ext.shape, c2),
                  pl.BlockSpec(wkv.shape, c2),
                  pl.BlockSpec((tm, LANES), lambda i: (i % tab_tiles, 0)),
                  pl.BlockSpec((tm, LANES), lambda i: (i % tab_tiles, 0))],
        out_specs=[pl.BlockSpec((tm, 2 * n_heads * LANES), lambda i: (i, 0)),
                   pl.BlockSpec((tm, 2 * n_heads * LANES), lambda i: (i, 0)),
                   pl.BlockSpec((tm, 2 * n_heads * LANES), lambda i: (i, 0)),
                   pl.BlockSpec((tm, kv_lora), lambda i: (i, 0)),
                   pl.BlockSpec((tm, LANES), lambda i: (i, 0))],
        out_shape=[jax.ShapeDtypeStruct((m, 2 * n_heads * LANES), BF),
                   jax.ShapeDtypeStruct((m, 2 * n_heads * LANES), BF),
                   jax.ShapeDtypeStruct((m, 2 * n_heads * LANES), BF),
                   jax.ShapeDtypeStruct((m, kv_lora), F32),
                   jax.ShapeDtypeStruct((m, LANES), F32)],
        compiler_params=_cparams(("parallel",)),
        name="mla_proj",
    )(h, w_m, qg, kvg, wq_ext, wkv, cosp, sinp)


def _cache_kv_kernel(ckv_ref, kr_ref, wkv_ref, k_ref, v_ref, *, n_heads):
    kv = _dot(ckv_ref[...].astype(BF), wkv_ref[...])
    _store_kv(kv, kr_ref[...].astype(BF), k_ref, v_ref, n_heads)


def _cache_kv(ckv, kr_pad, wkv, n_heads):
    rows, kvl = ckv.shape
    return pl.pallas_call(
        functools.partial(_cache_kv_kernel, n_heads=n_heads),
        grid=(1,),
        in_specs=[pl.BlockSpec((rows, kvl), lambda i: (0, 0)),
                  pl.BlockSpec((rows, LANES), lambda i: (0, 0)),
                  pl.BlockSpec(wkv.shape, lambda i: (0, 0))],
        out_specs=[pl.BlockSpec((rows, 2 * n_heads * LANES), lambda i: (0, 0)),
                   pl.BlockSpec((rows, 2 * n_heads * LANES), lambda i: (0, 0))],
        out_shape=[jax.ShapeDtypeStruct((rows, 2 * n_heads * LANES), BF),
                   jax.ShapeDtypeStruct((rows, 2 * n_heads * LANES), BF)],
        compiler_params=_cparams(("arbitrary",)),
        name="mla_cache_kv",
    )(ckv, kr_pad, wkv)


ATTN_ROWS = 256


def _softmax_pv(q, k, v):
    s = _dot_nt(q, k)
    p = jnp.exp2(s - jnp.max(s, axis=-1, keepdims=True)).astype(BF)
    acc = _dot(p, v)
    return (acc[:, :LANES] / acc[:, LANES:LANES + 1]).astype(BF)


def _attn_kernel(q_ref, k_ref, v_ref, o_ref):
    k, v = k_ref[0], v_ref[0]
    for r0 in range(0, q_ref.shape[1], ATTN_ROWS):
        o_ref[0, r0:r0 + ATTN_ROWS] = _softmax_pv(q_ref[0, r0:r0 + ATTN_ROWS], k, v)


def _attn_heads_kernel(q_ref, k_ref, v_ref, o_ref, *, n_heads):
    for hh in range(n_heads):
        hk = slice(hh * 2 * LANES, (hh + 1) * 2 * LANES)
        o_ref[0, :, hh * LANES:(hh + 1) * LANES] = _softmax_pv(q_ref[0, :, hk], k_ref[0, :, hk], v_ref[0, :, hk])


def _attention(q, k, v, n_heads):
    b, l, _ = q.shape
    s = k.shape[1]
    hw = 2 * LANES
    out_shape = jax.ShapeDtypeStruct((b, l, n_heads * LANES), BF)
    if l * n_heads <= 4096:
        return pl.pallas_call(
            functools.partial(_attn_heads_kernel, n_heads=n_heads),
            grid=(b,),
            in_specs=[pl.BlockSpec((1, l, n_heads * hw), lambda bi: (bi, 0, 0)),
                      pl.BlockSpec((1, s, n_heads * hw), lambda bi: (bi, 0, 0)),
                      pl.BlockSpec((1, s, n_heads * hw), lambda bi: (bi, 0, 0))],
            out_specs=pl.BlockSpec((1, l, n_heads * LANES), lambda bi: (bi, 0, 0)),
            out_shape=out_shape,
            compiler_params=_cparams(("parallel",)),
            name="mla_attention_ctx",
        )(q, k, v)
    tq = min(2048, l)
    return pl.pallas_call(
        _attn_kernel,
        grid=(b, n_heads, l // tq),
        in_specs=[pl.BlockSpec((1, tq, hw), lambda bi, h, i: (bi, i, h)),
                  pl.BlockSpec((1, s, hw), lambda bi, h, i: (bi, 0, h)),
                  pl.BlockSpec((1, s, hw), lambda bi, h, i: (bi, 0, h))],
        out_specs=pl.BlockSpec((1, tq, LANES), lambda bi, h, i: (bi, i, h)),
        out_shape=out_shape,
        compiler_params=_cparams(("parallel", "parallel", "arbitrary")),
        name="mla_attention",
    )(q, k, v)


def _ret_kernel(*refs, with_state, n_chunks, c_len, n_hp, k_scale):
    if with_state:
        (q_ref, k_ref, v_ref, g_ref, dec_ref, ng_ref, s0f_ref, s0b_ref,
         y_ref, dm_ref, kvf_ref, kvb_ref, sf_ref, sb_ref) = refs
    else:
        (q_ref, k_ref, v_ref, g_ref, dec_ref, ng_ref,
         y_ref, ff_ref, fb_ref, dm_ref, kvf_ref, kvb_ref, sf_ref, sb_ref) = refs
    reps = c_len // LANES

    @pl.when(pl.program_id(1) == 0)
    def _():
        di = (lax.broadcasted_iota(jnp.int32, (c_len, c_len), 0)
              - lax.broadcasted_iota(jnp.int32, (c_len, c_len), 1)).astype(F32)
        for hp in range(n_hp):
            lgf = jnp.tile(-jnp.exp(dec_ref[0, hp]), (1, reps))
            lgb = jnp.tile(-jnp.exp(dec_ref[1, hp]), (1, reps))
            dm_ref[hp] = jnp.where(di > 0, jnp.exp(lgf * jnp.maximum(di, 0.0)),
                                   jnp.where(di < 0, jnp.exp(lgb * jnp.maximum(-di, 0.0)), 2.0))

    ii = lax.broadcasted_iota(jnp.int32, (c_len, LANES), 0).astype(F32)
    need_state = with_state or n_chunks > 1
    for hp in range(n_hp):
        cs = slice(hp * LANES, (hp + 1) * LANES)
        lgf = -jnp.exp(dec_ref[0, hp])
        lgb = -jnp.exp(dec_ref[1, hp])
        dk_f = jnp.exp(lgf * (c_len - 1.0 - ii))
        dk_b = jnp.exp(lgb * ii)
        ds_f = jnp.exp(lgf * c_len)
        ds_b = jnp.exp(lgb * c_len)

        for c in range(n_chunks):
            rows = slice(c * c_len, (c + 1) * c_len)
            kc = k_ref[0, rows, cs] * k_scale
            vc = v_ref[0, rows, cs].astype(BF)
            kvf_ref[c] = _dot((kc * dk_f).T.astype(BF), vc)
            kvb_ref[c] = _dot((kc * dk_b).T.astype(BF), vc)

        if with_state:
            s_f, s_b = s0f_ref[0, hp], s0b_ref[0, hp]
        else:
            s_f = s_b = jnp.zeros((LANES, LANES), F32)
        for c in range(n_chunks):
            sf_ref[c] = s_f
            s_f = ds_f * s_f + kvf_ref[c]
        for c in reversed(range(n_chunks)):
            sb_ref[c] = s_b
            s_b = ds_b * s_b + kvb_ref[c]
        if not with_state:
            ff_ref[0, hp] = s_f
            fb_ref[0, hp] = s_b

        ng = ng_ref[:, cs]
        dq_f = jnp.exp(lgf * (ii + 1.0))
        dq_b = jnp.exp(lgb * (c_len - ii))
        for c in range(n_chunks):
            rows = slice(c * c_len, (c + 1) * c_len)
            qc = q_ref[0, rows, cs]
            kc = (k_ref[0, rows, cs] * k_scale).astype(BF)
            vc = v_ref[0, rows, cs].astype(BF)
            att = _dot_nt(qc.astype(BF), kc) * dm_ref[hp]
            o = _dot(att.astype(BF), vc)
            if need_state:
                o = (o + _dot((qc * dq_f).astype(BF), sf_ref[c].astype(BF))
                     + _dot((qc * dq_b).astype(BF), sb_ref[c].astype(BF)))
            mu = jnp.mean(o, axis=-1, keepdims=True)
            oc = o - mu
            var = jnp.mean(oc * oc, axis=-1, keepdims=True)
            on = oc * lax.rsqrt(var + NORM_EPS)
            gg = g_ref[0, rows, cs]
            y_ref[0, rows, cs] = (gg * _sigmoid(gg) * (on * ng)).astype(BF)


def _retention(zs, col0, dec, ng, s0f, s0b, n_heads):
    b, l, _ = zs.shape
    c_len = min(RET_CHUNK_MAX, l)
    nc = l // c_len
    with_state = s0f is not None
    n_hp = 1 if nc > 1 else n_heads
    hw = n_hp * LANES
    cb = col0 // hw

    def zspec(seg):
        return pl.BlockSpec((1, l, hw), lambda hg, bi: (bi, 0, cb + seg * (n_heads // n_hp) + hg))

    st_spec = pl.BlockSpec((1, n_hp, LANES, LANES), lambda hg, bi: (bi, hg, 0, 0))
    in_specs = [zspec(0), zspec(1), zspec(2), zspec(3),
                pl.BlockSpec((2, n_hp, 1, LANES), lambda hg, bi: (0, hg, 0, 0)),
                pl.BlockSpec((1, hw), lambda hg, bi: (0, hg))]
    args = [zs, zs, zs, zs, dec, ng]
    y_spec = pl.BlockSpec((1, l, hw), lambda hg, bi: (bi, 0, hg))
    y_shape = jax.ShapeDtypeStruct((b, l, n_heads * LANES), BF)
    if with_state:
        in_specs += [st_spec, st_spec]
        args += [s0f, s0b]
        out_specs, out_shape = y_spec, y_shape
    else:
        st_shape = jax.ShapeDtypeStruct((b, n_heads, LANES, LANES), F32)
        out_specs, out_shape = [y_spec, st_spec, st_spec], [y_shape, st_shape, st_shape]
    scratch = ([pltpu.VMEM((n_hp, c_len, c_len), F32)]
               + [pltpu.VMEM((nc, LANES, LANES), F32) for _ in range(4)])
    return pl.pallas_call(
        functools.partial(_ret_kernel, with_state=with_state, n_chunks=nc, c_len=c_len, n_hp=n_hp,
                          k_scale=float(LANES) ** -0.5),
        grid=(n_heads // n_hp, b),
        in_specs=in_specs, out_specs=out_specs, out_shape=out_shape,
        scratch_shapes=scratch,
        compiler_params=_cparams(("arbitrary", "arbitrary")),
        name="retention",
    )(*args)


def _s5_prep_kernel(lre_ref, lim_ref, ldt_ref, btre_ref, btim_ref, cre_ref, cim_ref,
                    toep_ref, wd_ref, wo_ref, av_ref, *, ch, p, seg_chunks):
    t = S5_T
    gw = (LANES // ch) * p
    btre, btim, cre, cim = btre_ref[...], btim_ref[...], cre_ref[...], cim_ref[...]
    r1 = lax.broadcasted_iota(jnp.int32, (LANES, LANES), 0)
    c1 = lax.broadcasted_iota(jnp.int32, (LANES, LANES), 1)
    same_group = (r1 // ch) == (c1 // ch)
    lane_valid = c1 < p
    r5 = lax.broadcasted_iota(jnp.int32, (LANES, gw), 0)
    c5 = lax.broadcasted_iota(jnp.int32, (LANES, gw), 1)
    own_state = (r5 // ch) == (c5 // p)

    def spread(x):
        x2 = x + pltpu.roll(x, LANES // 2, 1)
        return jnp.where(own_state, jnp.concatenate([x2] * (gw // LANES), axis=1), 0.0)

    xs, ys, lags = [], [], []
    for d in range(2):
        lr, li, dt = lre_ref[d], lim_ref[d], jnp.exp(ldt_ref[d])
        ar, ai = lr * dt, li * dt
        mag = jnp.exp(ar)
        abr, abi = mag * jnp.cos(ai), mag * jnp.sin(ai)
        nr, ni = abr - 1.0, abi
        den = lr * lr + li * li
        cr = (nr * lr + ni * li) / den
        ci = (ni * lr - nr * li) / den
        pr, pi = jnp.ones_like(lr), jnp.zeros_like(lr)
        xd, yd, gd = [], [], []
        for kk in range(t + 1):
            if kk < t:
                wr, wi = cr * pr - ci * pi, cr * pi + ci * pr
                xr, xi = wr * btre - wi * btim, wr * btim + wi * btre
                xd.append((xr, xi))
                hi = lax.Precision.HIGHEST
                gd.append(jnp.where(same_group, _dot_nt(xr, cre, hi) - _dot_nt(xi, cim, hi), 0.0))
            if kk >= 1:
                yd.append((cre * pr - cim * pi, -(cre * pi + cim * pr)))
            if kk == t:
                qr, qi = jnp.where(lane_valid, pr, 0.0), jnp.where(lane_valid, pi, 0.0)
            pr, pi = pr * abr - pi * abi, pr * abi + pi * abr
        xs.append(xd)
        ys.append(yd)
        lags.append(gd)
        inv = 1.0 / ch
        av_ref[0, 2 * d:2 * d + 1] = jnp.sum(spread(qr), axis=0, keepdims=True) * inv
        av_ref[0, 2 * d + 1:2 * d + 2] = jnp.sum(spread(qi), axis=0, keepdims=True) * inv
        for _ in range(seg_chunks.bit_length() - 1):
            qr, qi = qr * qr - qi * qi, 2.0 * qr * qi
        av_ref[0, 4 + 2 * d:5 + 2 * d] = jnp.sum(spread(qr), axis=0, keepdims=True) * inv
        av_ref[0, 5 + 2 * d:6 + 2 * d] = jnp.sum(spread(qi), axis=0, keepdims=True) * inv

    for ti in range(t):
        for to in range(t):
            lag = to - ti
            tile = lags[0][lag] if lag > 0 else (lags[1][-lag] if lag < 0 else lags[0][0] + lags[1][0])
            toep_ref[0, ti * LANES:(ti + 1) * LANES, to * LANES:(to + 1) * LANES] = tile.astype(BF)
    for tt in range(t):
        rows = slice(tt * LANES, (tt + 1) * LANES)
        xf, xb = xs[0][t - 1 - tt], xs[1][tt]
        yf, yb = ys[0][tt], ys[1][t - 1 - tt]
        for m, (xv, yv) in enumerate(((xf[0], yf[0]), (xf[1], yf[1]), (xb[0], yb[0]), (xb[1], yb[1]))):
            wd_ref[0, rows, m * gw:(m + 1) * gw] = spread(xv).astype(BF)
            wo_ref[0, m * gw:(m + 1) * gw, rows] = spread(yv).T.astype(BF)


def _s5_prep(lam_re, lam_im, log_dt, b_re, b_im, c_re, c_im):
    _, g, p = lam_re.shape
    ch = b_re.shape[-1]
    assert p == LANES // 2 and LANES % ch == 0
    nb = g * ch // LANES
    gw = (LANES // ch) * p
    t = S5_T
    seg_chunks = S5_SEG // t
    assert seg_chunks & (seg_chunks - 1) == 0

    def prow(a, padval):
        return jnp.pad(jnp.repeat(a, ch, axis=1), ((0, 0), (0, 0), (0, LANES - p)), constant_values=padval)

    def wrow(a):
        return jnp.pad(a.reshape(g * ch, p), ((0, 0), (0, LANES - p)))

    lre = prow(lam_re, -0.5)
    lim = prow(lam_im, 0.0)
    ldt = prow(jnp.broadcast_to(log_dt[..., None], lam_re.shape), 0.0)
    lspec = pl.BlockSpec((2, LANES, LANES), lambda i: (0, i, 0))
    wspec = pl.BlockSpec((LANES, LANES), lambda i: (i, 0))
    return pl.pallas_call(
        functools.partial(_s5_prep_kernel, ch=ch, p=p, seg_chunks=seg_chunks),
        grid=(nb,),
        in_specs=[lspec, lspec, lspec, wspec, wspec, wspec, wspec],
        out_specs=[pl.BlockSpec((1, t * LANES, t * LANES), lambda i: (i, 0, 0)),
                   pl.BlockSpec((1, t * LANES, 4 * gw), lambda i: (i, 0, 0)),
                   pl.BlockSpec((1, 4 * gw, t * LANES), lambda i: (i, 0, 0)),
                   pl.BlockSpec((1, 8, gw), lambda i: (i, 0, 0))],
        out_shape=[jax.ShapeDtypeStruct((nb, t * LANES, t * LANES), BF),
                   jax.ShapeDtypeStruct((nb, t * LANES, 4 * gw), BF),
                   jax.ShapeDtypeStruct((nb, 4 * gw, t * LANES), BF),
                   jax.ShapeDtypeStruct((nb, 8, gw), F32)],
        compiler_params=_cparams(("parallel",)),
        name="s5_prep",
    )(lre, lim, ldt, wrow(jnp.swapaxes(b_re, 1, 2)), wrow(jnp.swapaxes(b_im, 1, 2)), wrow(c_re), wrow(c_im))


def _gelu(x):
    return 0.5 * x * (1.0 + jnp.tanh(0.7978845608028654 * (x + 0.044715 * (x * x * x))))


def _s5_kernel(*refs, chain):
    if chain:
        (u_ref, toep_ref, wd_ref, wo_ref, a_ref, dsk_ref, s0_ref,
         y_ref, lhs_ref, d_ref, s_ref, xs_ref) = refs
    else:
        (u_ref, toep_ref, wd_ref, wo_ref, a_ref, dsk_ref,
         y_ref, fin_ref, lhs_ref, d_ref, s_ref) = refs
    nc, t, pb, _ = u_ref.shape
    rows = nc * pb
    w = a_ref.shape[-1]
    for tt in range(t):
        lhs_ref[:, tt * LANES:(tt + 1) * LANES] = u_ref[:, tt].reshape(rows, LANES).astype(BF)
    d_ref[...] = _dot(lhs_ref[...], wd_ref[0])
    av = a_ref[0]
    afr, afi, abr, abi = av[0:1], av[1:2], av[2:3], av[3:4]

    def scan(init, store):
        sfr, sfi, sbr, sbi = init
        for c in range(nc):
            rf = slice(c * pb, (c + 1) * pb)
            rb = slice((nc - 1 - c) * pb, (nc - c) * pb)
            if store:
                s_ref[rf, 0:w] = sfr.astype(BF)
                s_ref[rf, w:2 * w] = sfi.astype(BF)
                s_ref[rb, 2 * w:3 * w] = sbr.astype(BF)
                s_ref[rb, 3 * w:4 * w] = sbi.astype(BF)
            sfr, sfi = (afr * sfr - afi * sfi + d_ref[rf, 0:w],
                        afr * sfi + afi * sfr + d_ref[rf, w:2 * w])
            sbr, sbi = (abr * sbr - abi * sbi + d_ref[rb, 2 * w:3 * w],
                        abr * sbi + abi * sbr + d_ref[rb, 3 * w:4 * w])
        return sfr, sfi, sbr, sbi

    zero = jnp.zeros((pb, w), F32)
    if not chain:
        fin = scan((zero, zero, zero, zero), True)
        for m in range(4):
            fin_ref[0, :, m * w:(m + 1) * w] = fin[m]
    else:
        efr, efi, ebr, ebi = scan((zero, zero, zero, zero), False)
        s0 = s0_ref[0, 0]
        xr, xi = s0[0:1], s0[1:2]
        for j in range(pb):
            xs_ref[j:j + 1, 0:w] = xr
            xs_ref[j:j + 1, w:2 * w] = xi
            xr, xi = (av[4:5] * xr - av[5:6] * xi + efr[j:j + 1],
                      av[4:5] * xi + av[5:6] * xr + efi[j:j + 1])
        xr, xi = s0[2:3], s0[3:4]
        for j in reversed(range(pb)):
            xs_ref[j:j + 1, 2 * w:3 * w] = xr
            xs_ref[j:j + 1, 3 * w:4 * w] = xi
            xr, xi = (av[6:7] * xr - av[7:8] * xi + ebr[j:j + 1],
                      av[6:7] * xi + av[7:8] * xr + ebi[j:j + 1])
        scan((xs_ref[:, 0:w], xs_ref[:, w:2 * w], xs_ref[:, 2 * w:3 * w], xs_ref[:, 3 * w:4 * w]), True)

    yv = _dot(lhs_ref[...], toep_ref[0]) + _dot(s_ref[...], wo_ref[0])
    dsk = dsk_ref[...]
    for tt in range(t):
        yt = yv[:, tt * LANES:(tt + 1) * LANES] + dsk * u_ref[:, tt].reshape(rows, LANES)
        y_ref[:, tt] = _gelu(yt).reshape(nc, pb, LANES)


def _s5(u4, toep, wd, wo, av, dskip, s0):
    nc, t, npb, nch = u4.shape
    nb = toep.shape[0]
    chain = s0 is not None
    pb = npb // s0.shape[1] if chain else min(16, npb)
    sw = wd.shape[-1]
    rows = nc * pb
    ublk = pl.BlockSpec((nc, t, pb, LANES), lambda kb, hh: (0, 0, hh, kb))

    def wspec(arr):
        return pl.BlockSpec((1,) + arr.shape[1:], lambda kb, hh: (kb, 0, 0))

    in_specs = [ublk, wspec(toep), wspec(wd), wspec(wo), wspec(av),
                pl.BlockSpec((1, LANES), lambda kb, hh: (0, kb))]
    args = [u4, toep, wd, wo, av, dskip]
    scratch = [pltpu.VMEM((rows, t * LANES), BF), pltpu.VMEM((rows, sw), F32), pltpu.VMEM((rows, sw), BF)]
    if chain:
        in_specs += [pl.BlockSpec((1, 1, 4, sw // 4), lambda kb, hh: (kb, hh, 0, 0))]
        args += [s0]
        out_specs = ublk
        out_shape = jax.ShapeDtypeStruct(u4.shape, F32)
        scratch.append(pltpu.VMEM((pb, sw), F32))
    else:
        out_specs = [ublk, pl.BlockSpec((1, pb, sw), lambda kb, hh: (kb, hh, 0))]
        out_shape = [jax.ShapeDtypeStruct(u4.shape, F32), jax.ShapeDtypeStruct((nb, npb, sw), F32)]
    return pl.pallas_call(
        functools.partial(_s5_kernel, chain=chain),
        grid=(nb, npb // pb),
        in_specs=in_specs, out_specs=out_specs, out_shape=out_shape,
        scratch_shapes=scratch,
        compiler_params=_cparams(("parallel", "parallel")),
        name="s5_chunked",
    )(*args)


def _glu_kernel(y_ref, w_ref, b_ref, o_ref):
    y = y_ref[...]
    o_ref[...] = (y * _sigmoid(_dot(y.astype(BF), w_ref[...]) + b_ref[...])).astype(BF)


def _glu(y, w, layer, b):
    m, n = y.shape
    tm = min(1024, m)
    return pl.pallas_call(
        _glu_kernel,
        grid=(m // tm,),
        in_specs=[pl.BlockSpec((tm, n), lambda i: (i, 0)),
                  pl.BlockSpec((None, n, n), lambda i: (layer, 0, 0)),
                  pl.BlockSpec((1, n), lambda i: (0, 0))],
        out_specs=pl.BlockSpec((tm, n), lambda i: (i, 0)),
        out_shape=jax.ShapeDtypeStruct((m, n), BF),
        compiler_params=_cparams(("parallel",)),
        name="s5_glu",
    )(y, w, b)


def _res_epilogue(acc_ref, x_ref, gate_ref, ng_ref, o_ref, nxt_refs):
    gn = gate_ref[0] * ng_ref[...]
    if nxt_refs is not None:
        ng2_ref, sh2_ref, sc2_ref, h_ref = nxt_refs
        g2 = ng2_ref[...] * (1.0 + sc2_ref[0])
        sh2 = sh2_ref[0]
    rows = 128
    for r0 in range(0, o_ref.shape[0], rows):
        acc = acc_ref[r0:r0 + rows]
        inv = lax.rsqrt(jnp.mean(acc * acc, axis=-1, keepdims=True) + NORM_EPS)
        xn = x_ref[r0:r0 + rows] + (acc * inv) * gn
        o_ref[r0:r0 + rows] = xn
        if nxt_refs is not None:
            inv2 = lax.rsqrt(jnp.mean(xn * xn, axis=-1, keepdims=True) + NORM_EPS)
            h_ref[r0:r0 + rows] = ((xn * inv2) * g2 + sh2).astype(BF)


def _merge_kernel(ya_ref, yb_ref, yc_ref, ga_ref, gb_ref, gc_ref, wb_ref, o_ref):
    acc = (_sigmoid(ga_ref[...].astype(F32)) * _dot(ya_ref[...], wb_ref[0])
           + _sigmoid(gb_ref[...].astype(F32)) * _dot(yb_ref[...], wb_ref[1])
           + _sigmoid(gc_ref[...].astype(F32)) * _dot(yc_ref[...], wb_ref[2]))
    o_ref[...] = acc.astype(BF)


def _merge(ya, yb, yc, zg, wb, layer):
    m, dm = ya.shape
    d = wb.shape[-1]
    tm, tn = min(1024, m), 512
    nt = d // tn
    ysp = pl.BlockSpec((tm, dm), lambda i, j: (i, 0))

    def gsp(br):
        return pl.BlockSpec((tm, tn), lambda i, j: (i, j + br * nt))

    return pl.pallas_call(
        _merge_kernel,
        grid=(m // tm, nt),
        in_specs=[ysp, ysp, ysp, gsp(0), gsp(1), gsp(2),
                  pl.BlockSpec((None, 3, dm, tn), lambda i, j: (layer, 0, 0, j))],
        out_specs=pl.BlockSpec((tm, tn), lambda i, j: (i, j)),
        out_shape=jax.ShapeDtypeStruct((m, d), BF),
        compiler_params=_cparams(("parallel", "parallel")),
        name="branch_merge",
    )(ya, yb, yc, zg, zg, zg, wb)


def _mm_res_kernel(*refs, with_next, single_step):
    if with_next:
        a_ref, w_ref, x_ref, gate_ref, ng_ref, ng2_ref, sh2_ref, sc2_ref, o_ref, h_ref, acc_ref = refs
        nxt_refs = (ng2_ref, sh2_ref, sc2_ref, h_ref)
    else:
        a_ref, w_ref, x_ref, gate_ref, ng_ref, o_ref, acc_ref = refs
        nxt_refs = None
    if single_step:
        acc_ref[...] = _dot(a_ref[...], w_ref[...])
        _res_epilogue(acc_ref, x_ref, gate_ref, ng_ref, o_ref, nxt_refs)
        return
    k = pl.program_id(1)

    @pl.when(k == 0)
    def _():
        acc_ref[...] = jnp.zeros_like(acc_ref)

    acc_ref[...] += _dot(a_ref[...], w_ref[...])

    @pl.when(k == pl.num_programs(1) - 1)
    def _():
        _res_epilogue(acc_ref, x_ref, gate_ref, ng_ref, o_ref, nxt_refs)


def _mm_res(a, w, layer, x, mods, gate_chunk, ng, rows_per_mod, nxt=None):
    m, kdim = a.shape
    d = w.shape[-1]
    tm = min(512, m)
    tk = kdim // 4 if kdim > 2048 else kdim
    assert tk % LANES == 0
    in_specs = [pl.BlockSpec((tm, tk), lambda i, k: (i, k)),
                pl.BlockSpec((None, tk, d), lambda i, k: (layer, k, 0)),
                pl.BlockSpec((tm, d), lambda i, k: (i, 0)),
                _mod_spec(gate_chunk, d, rows_per_mod, tm),
                pl.BlockSpec((1, d), lambda i, k: (0, 0))]
    args = [a, w, x, mods, ng]
    x_spec = pl.BlockSpec((tm, d), lambda i, k: (i, 0))
    x_shape = jax.ShapeDtypeStruct((m, d), F32)
    if nxt is None:
        out_specs, out_shape = x_spec, x_shape
    else:
        ng2, mods2, sh_chunk, sc_chunk = nxt
        in_specs += [pl.BlockSpec((1, d), lambda i, k: (0, 0)),
                     _mod_spec(sh_chunk, d, rows_per_mod, tm),
                     _mod_spec(sc_chunk, d, rows_per_mod, tm)]
        args += [ng2, mods2, mods2]
        out_specs = [x_spec, pl.BlockSpec((tm, d), lambda i, k: (i, 0))]
        out_shape = [x_shape, jax.ShapeDtypeStruct((m, d), BF)]
    return pl.pallas_call(
        functools.partial(_mm_res_kernel, with_next=nxt is not None, single_step=kdim == tk),
        grid=(m // tm, kdim // tk),
        in_specs=in_specs, out_specs=out_specs, out_shape=out_shape,
        scratch_shapes=[pltpu.VMEM((tm, d), F32)],
        compiler_params=_cparams(("parallel", "arbitrary")),
        name="matmul_norm_residual",
    )(*args)


FFN_HALO = 16


def _ffn_up_kernel(hp_ref, h_ref, hn_ref, wv_ref, wg_ref, cwv_ref, cwg_ref, cbv_ref, cbg_ref,
                   o_ref, hb_ref, uv_ref, ug_ref, *, tm, piece, seq_len):
    i, j = pl.program_id(0), pl.program_id(1)
    hl = FFN_HALO
    n_pieces = tm // piece
    stride = piece + hl

    @pl.when(j == 0)
    def _():
        zero = jnp.zeros(hp_ref.shape, BF)
        if n_pieces == 1 and seq_len > tm:
            hb_ref[0:hl] = jnp.where(lax.rem(i * tm, seq_len) != 0, hp_ref[...], zero)
            hb_ref[stride:stride + hl] = jnp.where(lax.rem((i + 1) * tm, seq_len) != 0, hn_ref[...], zero)
        else:
            for pc in range(n_pieces + 1):
                hb_ref[pc * stride:pc * stride + hl] = zero
        for pc in range(n_pieces):
            hb_ref[hl + pc * stride:hl + pc * stride + piece] = h_ref[pc * piece:(pc + 1) * piece]

    hb = hb_ref[...]
    ug_ref[...] = _dot(hb, wg_ref[...])
    uv_ref[...] = _dot(hb, wv_ref[...])
    cwv, cwg, cbv, cbg = cwv_ref[...], cwg_ref[...], cbv_ref[...], cbg_ref[...]
    for pc in range(n_pieces):
        b0 = hl + pc * stride

        def conv(u_ref, cw, cb):
            return (u_ref[b0 - 1:b0 - 1 + piece] * cw[0:1] + u_ref[b0:b0 + piece] * cw[1:2]
                    + u_ref[b0 + 1:b0 + 1 + piece] * cw[2:3] + cb)

        gate = conv(ug_ref, cwg, cbg)
        act = gate * _sigmoid(gate)
        o_ref[pc * piece:(pc + 1) * piece] = (act * conv(uv_ref, cwv, cbv)).astype(BF)


def _ffn_up(h, w_up, layer, conv_w, conv_b, seq_len):
    m, d = h.shape
    f = w_up.shape[-1] // 2
    tm, tf = min(1024, m), 512
    piece = min(seq_len, tm)
    assert tm % piece == 0 and seq_len % piece == 0
    nf = f // tf
    hl = FFN_HALO
    nblk = m // hl
    rows = (tm // piece) * (piece + hl) + hl

    return pl.pallas_call(
        functools.partial(_ffn_up_kernel, tm=tm, piece=piece, seq_len=seq_len),
        grid=(m // tm, nf),
        in_specs=[pl.BlockSpec((hl, d), lambda i, j: (jnp.maximum(i * (tm // hl) - 1, 0), 0)),
                  pl.BlockSpec((tm, d), lambda i, j: (i, 0)),
                  pl.BlockSpec((hl, d), lambda i, j: (jnp.minimum((i + 1) * (tm // hl), nblk - 1), 0)),
                  pl.BlockSpec((None, d, tf), lambda i, j: (layer, 0, j)),
                  pl.BlockSpec((None, d, tf), lambda i, j: (layer, 0, j + nf)),
                  pl.BlockSpec((3, tf), lambda i, j: (0, j)),
                  pl.BlockSpec((3, tf), lambda i, j: (0, j + nf)),
                  pl.BlockSpec((1, tf), lambda i, j: (0, j)),
                  pl.BlockSpec((1, tf), lambda i, j: (0, j + nf))],
        out_specs=pl.BlockSpec((tm, tf), lambda i, j: (i, j)),
        out_shape=jax.ShapeDtypeStruct((m, f), BF),
        scratch_shapes=[pltpu.VMEM((rows, d), BF), pltpu.VMEM((rows, tf), F32), pltpu.VMEM((rows, tf), F32)],
        compiler_params=_cparams(("parallel", "arbitrary")),
        name="ffn_up_conv",
    )(h, h, h, w_up, w_up, conv_w, conv_w, conv_b, conv_b)


def _pair_swap(w):
    r = w.reshape(w.shape[:-1] + (w.shape[-1] // 2, 2))
    return jnp.stack([-r[..., 1], r[..., 0]], axis=-1).reshape(w.shape)


def _rope_tables(n_tokens, dr):
    rows = n_tokens // GRID_W
    row = jnp.broadcast_to(jnp.arange(rows, dtype=F32)[:, None], (rows, GRID_W)).reshape(-1)
    col = jnp.broadcast_to(jnp.arange(GRID_W, dtype=F32)[None, :], (rows, GRID_W)).reshape(-1)
    nf = dr // 4
    inv = ROPE_BASE ** (-jnp.arange(nf, dtype=F32) / nf)
    ang = jnp.concatenate([row[:, None] * inv, col[:, None] * inv], axis=-1)
    ang = jnp.repeat(ang, 2, axis=-1)
    pad = jnp.zeros((n_tokens, LANES - dr), F32)
    return jnp.concatenate([jnp.cos(ang), pad], axis=-1), jnp.concatenate([jnp.sin(ang), pad], axis=-1)


def kernel(x_prompt, x_sample, cache_mla_ckv, cache_mla_krope, state_ret_fwd, state_ret_bwd, state_s5_fwd, state_s5_bwd, c, c_ctx, ada_w, ada_b, norm_g, w_in, s5_lam_re, s5_lam_im, s5_log_dt, s5_b_re, s5_b_im, s5_c_re, s5_c_im, s5_d, s5_glu_w, s5_glu_b, ret_decay, ret_norm_g, mla_q_norm, mla_kv_norm, mla_w_uq, mla_w_uk, mla_w_uv, w_branch, w_out, ffn_w_up, ffn_conv_w, ffn_conv_b, ffn_w_down):
    bsz, seq, d = x_prompt.shape
    dbsz, dseq, _ = x_sample.shape
    depth = ada_w.shape[0]
    dm = d // 2
    n_heads = ret_decay.shape[-1]
    dr = cache_mla_krope.shape[-1]
    q_lora, kv_lora = mla_w_uq.shape[1], mla_w_uk.shape[1]
    g5, p5, ch5 = s5_b_re.shape[1:]
    assert dm == n_heads * LANES and dr == LANES // 2 and mla_w_uk.shape[-1] == dm
    assert seq == S5_SEG and dseq % S5_SEG == 0
    qscale = float(LANES + dr) ** -0.5 * 1.4426950408889634

    cvecs = jnp.concatenate([c_ctx[None], c, jnp.zeros((8 - 1 - dbsz, d), F32)], axis=0)
    mods_all = _ada_mods(cvecs, ada_w, ada_b)

    cos_l, sin_l = _rope_tables(dseq, dr)
    ones_tab = jnp.concatenate([jnp.ones((256, dr), F32), jnp.zeros((256, LANES - dr), F32)], axis=-1)
    zeros_tab = jnp.zeros((256, LANES), F32)

    cw = np.cumsum([0, dm, dm, dm, dm, dm, q_lora, kv_lora, dr, d, d, d])
    w_a = jnp.concatenate([w_in[:, :, cw[8]:cw[11]], w_in[:, :, cw[0]:cw[5]]], axis=2).astype(BF)
    w_kr = w_in[:, :, cw[7]:cw[8]]
    w_m = jnp.concatenate([w_in[:, :, cw[5]:cw[7]], w_kr, _pair_swap(w_kr)], axis=2).astype(BF)
    uq = mla_w_uq.reshape(depth, q_lora, n_heads, LANES + dr)
    wq_ext = jnp.concatenate([uq[..., :LANES], uq[..., LANES:], _pair_swap(uq[..., LANES:])],
                             axis=-1).reshape(depth, q_lora, n_heads * 2 * LANES).astype(BF)
    wkv = jnp.concatenate([mla_w_uk, mla_w_uv], axis=2).astype(BF)
    glu_w = s5_glu_w.astype(BF)
    wb = w_branch.astype(BF)
    wo_ = w_out.astype(BF)
    wup = ffn_w_up.astype(BF)
    wdn = ffn_w_down.astype(BF)

    outs = {k: [] for k in ("ckv", "krope", "retf", "retb", "s5f", "s5b")}
    nb5 = g5 * ch5 // LANES
    gb5 = LANES // ch5

    def group_mods(l, ctx):
        if ctx:
            return mods_all[l, 0:1].reshape(1, 1, -1), bsz * seq
        return mods_all[l, 1:1 + dbsz].reshape(dbsz, 1, -1), dseq

    xres = [x_prompt.reshape(bsz * seq, d), x_sample.reshape(dbsz * dseq, d)]
    hres = []
    for grp in range(2):
        mods, rpm = group_mods(0, grp == 0)
        hres.append(_norm_mod(xres[grp], mods, norm_g[0, 0].reshape(1, d), rpm))

    for l in range(depth):
        qg = mla_q_norm[l].reshape(1, q_lora)
        kvg = mla_kv_norm[l].reshape(1, kv_lora)
        ng = norm_g[l].reshape(4, 1, d)
        glu_b = s5_glu_b[l].reshape(1, dm)
        conv_w = ffn_conv_w[l]
        conv_b = ffn_conv_b[l].reshape(1, -1)
        dec = jnp.broadcast_to(ret_decay[l].reshape(2, n_heads, 1, 1), (2, n_heads, 1, LANES))
        rng = ret_norm_g[l].reshape(1, dm)
        dskip = s5_d[l].reshape(1, dm)
        toep, wd, wo5, av = _s5_prep(s5_lam_re[l], s5_lam_im[l], s5_log_dt[l], s5_b_re[l], s5_b_im[l],
                                     s5_c_re[l], s5_c_im[l])

        for grp in range(2):
            ctx = grp == 0
            x, h = xres[grp], hres[grp]
            nbat, slen = (bsz, seq) if ctx else (dbsz, dseq)
            m = nbat * slen
            mods, rpm = group_mods(l, ctx)

            zg, zu, zs = _inproj(h, w_a, l, 3 * d, dm)

            nseg = slen // S5_SEG
            npb = nbat * nseg
            u = zu.reshape(npb, S5_SEG, dm).transpose(1, 0, 2)
            u4 = u.reshape(S5_SEG // S5_T, S5_T, npb, dm)
            if ctx:
                y4, fin = _s5(u4, toep, wd, wo5, av, dskip, None)
                fin = fin.reshape(nb5, npb, 4, gb5, p5).transpose(1, 2, 0, 3, 4).reshape(npb, 4, g5, p5)
                outs["s5f"].append(jnp.stack([fin[:, 0], fin[:, 1]], axis=-1))
                outs["s5b"].append(jnp.stack([fin[:, 2], fin[:, 3]], axis=-1))
            else:
                s0 = jnp.stack([state_s5_fwd[:, l, :, :, 0], state_s5_fwd[:, l, :, :, 1],
                                state_s5_bwd[:, l, :, :, 0], state_s5_bwd[:, l, :, :, 1]], axis=1)
                s0 = s0.reshape(dbsz, 4, nb5, gb5 * p5).transpose(2, 0, 1, 3)
                y4 = _s5(u4, toep, wd, wo5, av, dskip, s0)
            ya_tm = _glu(y4.reshape(S5_SEG * npb, dm), glu_w, l, glu_b)
            ya = ya_tm.reshape(S5_SEG, npb, dm).transpose(1, 0, 2).reshape(m, dm)

            zs3 = zs.reshape(nbat, slen, -1)
            if ctx:
                yb, rf, rb = _retention(zs3, 0, dec, rng, None, None, n_heads)
                outs["retf"].append(rf)
                outs["retb"].append(rb)
            else:
                yb = _retention(zs3, 0, dec, rng, state_ret_fwd[:, l], state_ret_bwd[:, l], n_heads)
            yb = yb.reshape(m, dm)

            if ctx:
                q, k, v, ckv, kr = _mla_proj(h, w_m[l], qg, kvg, wq_ext[l], wkv[l], ones_tab, zeros_tab,
                                             n_heads, qscale)
                outs["ckv"].append(ckv.reshape(nbat, slen, kv_lora))
                outs["krope"].append(kr[:, :dr].reshape(nbat, slen, dr))
                k, v = k.reshape(nbat, slen, -1), v.reshape(nbat, slen, -1)
            else:
                q, k, v, _, _ = _mla_proj(h, w_m[l], qg, kvg, wq_ext[l], wkv[l], cos_l, sin_l, n_heads, qscale)
                past = cache_mla_ckv.shape[2]
                kr_pad = jnp.pad(cache_mla_krope[:, l], ((0, 0), (0, 0), (0, LANES - dr)))
                kc, vc = _cache_kv(cache_mla_ckv[:, l].reshape(nbat * past, kv_lora),
                                   kr_pad.reshape(nbat * past, LANES), wkv[l], n_heads)
                k = jnp.concatenate([k.reshape(nbat, slen, -1), kc.reshape(nbat, past, -1)], axis=1)
                v = jnp.concatenate([v.reshape(nbat, slen, -1), vc.reshape(nbat, past, -1)], axis=1)
            yc = _attention(q.reshape(nbat, slen, -1), k, v, n_heads).reshape(m, dm)

            merged = _merge(ya, yb, yc, zg, wb, l)
            x, h = _mm_res(merged, wo_, l, x, mods, 2, ng[1], rpm, nxt=(ng[2], mods, 3, 4))
            act = _ffn_up(h, wup, l, conv_w, conv_b, slen)
            if l + 1 < depth:
                x, h = _mm_res(act, wdn, l, x, mods, 5, ng[3], rpm,
                               nxt=(norm_g[l + 1, 0].reshape(1, d), group_mods(l + 1, ctx)[0], 0, 1))
            else:
                x, h = _mm_res(act, wdn, l, x, mods, 5, ng[3], rpm), None
            xres[grp], hres[grp] = x, h

    return (xres[0].reshape(bsz, seq, d), xres[1].reshape(dbsz, dseq, d),
            jnp.stack(outs["ckv"], axis=1), jnp.stack(outs["krope"], axis=1),
            jnp.stack(outs["retf"], axis=1), jnp.stack(outs["retb"], axis=1),
            jnp.stack(outs["s5f"], axis=1), jnp.stack(outs["s5b"], axis=1))
```
